```python
import math
import jax, jax.numpy as jnp
from jax import lax
import numpy as np

D_MODEL = 4096
BATCH = 4
SEQ = 2048
DEPTH = 2
DEC_BATCH = 8
DEC_SEQ = 1
PAST_LEN = 16384
PAGE_SIZE = 128

D_SSM = D_MODEL // 2
SSM_GROUP = 16
SSM_GROUPS = D_SSM // SSM_GROUP
SSM_STATE = 64
N_HEADS = 16
HEAD_DIM = 128
N_KV_HEADS = 2
Q_PER_KV = N_HEADS // N_KV_HEADS
D_NSA = N_HEADS * HEAD_DIM
D_KV = 2 * N_KV_HEADS * HEAD_DIM
CMP_STRIDE = 16
CMP_BLOCK = 2 * CMP_STRIDE
CMP_HIDDEN = HEAD_DIM
SEL_BLOCK = 64
N_SEL = 16
N_LOCAL_SEL = 2
WINDOW = 512
Q_BLOCK = 128
D_FF = 3 * D_MODEL
N_EXPERTS = 8
TOP_K = 2
D_FF_EXPERT = 14336
N_DENSE = (DEPTH + 1) // 2
N_MOE = DEPTH // 2
RMS_EPS = 1e-6
NEG_INF = -1e30
BIG = 1e30
IN_SPLITS = (D_SSM, D_NSA, D_KV, D_KV, D_KV, 3 * N_HEADS, D_MODEL, D_MODEL)
D_IN = D_SSM + D_NSA + 3 * D_KV + 3 * N_HEADS + 2 * D_MODEL

kernel_name = 'hybrid_s5_nsa_griffin_moe_step'


def rmsnorm(x, g):
    xf = x.astype(jnp.float32)
    xf = xf * lax.rsqrt(jnp.mean(xf * xf, axis=-1, keepdims=True) + RMS_EPS)
    return (xf * g.astype(jnp.float32)).astype(x.dtype)


def masked_softmax(s, mask):
    s = jnp.where(mask, s, NEG_INF)
    p = jnp.exp(s - jnp.max(s, axis=-1, keepdims=True)) * mask
    return p / jnp.maximum(jnp.sum(p, axis=-1, keepdims=True), 1e-30)


def split_in(h, w_in):
    bsz, t = h.shape[0], h.shape[1]
    bounds = [int(v) for v in np.cumsum(IN_SPLITS)[:-1]]
    u, q, kvc, kvs, kvw, gl, g_ssm, g_nsa = jnp.split(h @ w_in, bounds, axis=-1)
    kv = lambda a: a.reshape(bsz, t, 2, N_KV_HEADS, HEAD_DIM)
    return (u, q.reshape(bsz, t, N_HEADS, HEAD_DIM), kv(kvc), kv(kvs), kv(kvw),
            gl.reshape(bsz, t, N_HEADS, 3), g_ssm, g_nsa)


def _linear_combine(left, right):
    a_l, b_l = left
    a_r, b_r = right
    return a_l * a_r, a_r * b_l + b_r


def s5_branch(u, s0_re, s0_im, a_re, a_im, b_re, b_im, c_re, c_im, d, log_dt, w_glu, b_glu):
    f = jnp.float32
    bsz, t = u.shape[0], u.shape[1]
    uf = u.astype(f).reshape(bsz, t, SSM_GROUPS, SSM_GROUP)
    lam = lax.complex(a_re.astype(f), a_im.astype(f))
    dt = jnp.exp(log_dt.astype(f))[:, None]
    lam_bar = jnp.exp(lam * dt)
    b_bar = ((lam_bar - 1.0) / lam)[..., None] * lax.complex(b_re.astype(f), b_im.astype(f))
    bu = jnp.einsum('btgc,gpc->btgp', uf.astype(jnp.complex64), b_bar)
    if s0_re is not None:
        bu = bu.at[:, 0].add(lam_bar * lax.complex(s0_re.astype(f), s0_im.astype(f)))
    decay = jnp.broadcast_to(lam_bar, bu.shape)
    _, states = lax.associative_scan(_linear_combine, (decay, bu), axis=1)
    c = lax.complex(c_re.astype(f), c_im.astype(f))
    y = jnp.real(jnp.einsum('btgp,gcp->btgc', states, c)) + d.astype(f).reshape(SSM_GROUPS, SSM_GROUP) * uf
    z = jax.nn.gelu(y.reshape(bsz, t, D_SSM))
    out = z * jax.nn.sigmoid(z @ w_glu.astype(f) + b_glu.astype(f))
    last = states[:, -1]
    return out.astype(u.dtype), jnp.real(last), jnp.imag(last)


def compress_blocks(seq, pe, w1, b1, w2):
    bsz, t = seq.shape[0], seq.shape[1]
    n_chunks = t // CMP_STRIDE
    chunks = seq[:, :n_chunks * CMP_STRIDE].reshape(bsz, n_chunks, CMP_STRIDE, N_KV_HEADS, HEAD_DIM)
    lead = jnp.einsum('bnjhd,jde->bnhe', chunks + pe[:CMP_STRIDE, None, :], w1[:CMP_STRIDE])
    trail = jnp.einsum('bnjhd,jde->bnhe', chunks + pe[CMP_STRIDE:, None, :], w1[CMP_STRIDE:])
    hidden = jax.nn.gelu(lead[:, :-1] + trail[:, 1:] + b1)
    return jnp.einsum('bnhe,ed->bnhd', hidden, w2)


def cmp_end_positions(n_cmp):
    return jnp.arange(n_cmp) * CMP_STRIDE + (CMP_BLOCK - 1)


def overlap_matrix(n_cmp, n_sel):
    start = jnp.arange(n_cmp)[:, None] * CMP_STRIDE
    js = jnp.arange(n_sel)[None, :] * SEL_BLOCK
    return ((start < js + SEL_BLOCK) & (start + CMP_BLOCK > js)).astype(jnp.float32)


def to_sel_blocks(seq):
    bsz, t = seq.shape[0], seq.shape[1]
    n_blk = -(-t // SEL_BLOCK)
    seq = jnp.pad(seq, ((0, 0), (0, n_blk * SEL_BLOCK - t), (0, 0), (0, 0)))
    return seq.reshape(bsz, n_blk, SEL_BLOCK, N_KV_HEADS, HEAD_DIM).transpose(0, 3, 1, 2, 4)


def nsa_attend(q, gate_logits, q_pos, kc, vc, c_end, ks_blk, vs_blk, kw, vw, kw_pos):
    bsz, tq = q.shape[0], q.shape[1]
    scale = HEAD_DIM ** -0.5
    qg = q.reshape(bsz, tq, N_KV_HEADS, Q_PER_KV, HEAD_DIM)
    s_c = jnp.einsum('bthgd,bnhd->bhgtn', qg, kc).astype(jnp.float32) * scale
    p_c = masked_softmax(s_c, c_end[None, :] <= q_pos[:, None])
    o_c = jnp.einsum('bhgtn,bnhd->bthgd', p_c, vc)
    n_blk = ks_blk.shape[2]
    imp = jnp.einsum('bhgtn,nj->bhtj', p_c, overlap_matrix(kc.shape[1], n_blk))
    blk = jnp.arange(n_blk)[None, :]
    cur = (q_pos // SEL_BLOCK)[:, None]
    forced = (blk == 0) | ((blk <= cur) & (blk > cur - N_LOCAL_SEL))
    score = jnp.where(blk > cur, -BIG, jnp.where(forced, BIG, imp))
    _, idx = lax.top_k(score, min(N_SEL, n_blk))
    b_ix = jnp.arange(bsz)[:, None, None, None]
    h_ix = jnp.arange(N_KV_HEADS)[None, :, None, None]
    n_pick = idx.shape[-1]
    ks_g = ks_blk[b_ix, h_ix, idx].reshape(bsz, N_KV_HEADS, tq, n_pick * SEL_BLOCK, HEAD_DIM)
    vs_g = vs_blk[b_ix, h_ix, idx].reshape(bsz, N_KV_HEADS, tq, n_pick * SEL_BLOCK, HEAD_DIM)
    tok_pos = (idx[..., None] * SEL_BLOCK + jnp.arange(SEL_BLOCK)).reshape(bsz, N_KV_HEADS, tq, n_pick * SEL_BLOCK)
    s_s = jnp.einsum('bthgd,bhtkd->bhgtk', qg, ks_g).astype(jnp.float32) * scale
    p_s = masked_softmax(s_s, (tok_pos <= q_pos[:, None])[:, :, None])
    o_s = jnp.einsum('bhgtk,bhtkd->bthgd', p_s, vs_g)
    s_w = jnp.einsum('bthgd,bshd->bhgts', qg, kw).astype(jnp.float32) * scale
    dist = q_pos[:, None] - kw_pos[None, :]
    p_w = masked_softmax(s_w, (dist >= 0) & (dist < WINDOW) & (kw_pos[None, :] >= 0))
    o_w = jnp.einsum('bhgts,bshd->bthgd', p_w, vw)
    g = jax.nn.sigmoid(gate_logits.astype(jnp.float32)).reshape(bsz, tq, N_KV_HEADS, Q_PER_KV, 3)
    o = g[..., 0:1] * o_c + g[..., 1:2] * o_s + g[..., 2:3] * o_w
    return o.reshape(bsz, tq, D_NSA).astype(q.dtype)


def nsa_prompt(q, gl, kc, vc, ks_blk, vs_blk, kv_w):
    bsz, t = q.shape[0], q.shape[1]
    c_end = cmp_end_positions(kc.shape[1])
    kw_pad = jnp.pad(kv_w, ((0, 0), (WINDOW, 0), (0, 0), (0, 0), (0, 0)))

    def query_block(qb):
        start = qb * Q_BLOCK
        q_blk = lax.dynamic_slice_in_dim(q, start, Q_BLOCK, axis=1)
        g_blk = lax.dynamic_slice_in_dim(gl, start, Q_BLOCK, axis=1)
        band = lax.dynamic_slice_in_dim(kw_pad, start, WINDOW + Q_BLOCK, axis=1)
        q_pos = start + jnp.arange(Q_BLOCK)
        kw_pos = start - WINDOW + jnp.arange(WINDOW + Q_BLOCK)
        return nsa_attend(q_blk, g_blk, q_pos, kc, vc, c_end, ks_blk, vs_blk,
                          band[:, :, 0], band[:, :, 1], kw_pos)

    out = lax.map(query_block, jnp.arange(t // Q_BLOCK))
    return out.transpose(1, 0, 2, 3).reshape(bsz, t, D_NSA)


def merge(ssm_out, nsa_out, g_ssm, g_nsa, w_br_ssm, w_br_nsa, w_out):
    y = jax.nn.sigmoid(g_ssm) * (ssm_out @ w_br_ssm) + jax.nn.sigmoid(g_nsa) * (nsa_out @ w_br_nsa)
    return y @ w_out


def swiglu(h, wg, wu, wd):
    return (jax.nn.silu(h @ wg) * (h @ wu)) @ wd


def moe_swiglu(h, w_router, wg, wu, wd):
    logits = (h @ w_router).astype(jnp.float32)
    top_val, top_idx = lax.top_k(logits, TOP_K)
    gates = jax.nn.softmax(top_val, axis=-1)
    combine = jnp.sum(jax.nn.one_hot(top_idx, N_EXPERTS, dtype=jnp.float32) * gates[..., None], axis=-2)
    out = jnp.zeros(h.shape, jnp.float32)
    for e in range(N_EXPERTS):
        out = out + combine[..., e:e + 1] * swiglu(h, wg[e], wu[e], wd[e])
    return out.astype(h.dtype)


def setup_inputs(seed: int = 0) -> dict:
    key = jax.random.key(seed)
    ks = list(jax.random.split(key, 48))
    nxt = lambda: ks.pop()
    nrm = lambda shape, scale: jax.random.normal(nxt(), shape, jnp.float32) * scale
    n_pages = PAST_LEN // PAGE_SIZE
    n_used = DEC_BATCH * n_pages
    n_pool = n_used + max(1, n_used // 4)
    win_buf = min(WINDOW, PAST_LEN)
    kv_pool = (DEPTH, n_pool, PAGE_SIZE, 2, N_KV_HEADS, HEAD_DIM)
    page_table = jax.random.permutation(nxt(), n_pool)[:n_used].reshape(DEC_BATCH, n_pages).astype(jnp.int32)
    return {
        'x_prompt': nrm((BATCH, SEQ, D_MODEL), 1.0),
        'x_sample': nrm((DEC_BATCH, DEC_SEQ, D_MODEL), 1.0),
        'cache_cmp': nrm(kv_pool, 1.0),
        'cache_sel': nrm(kv_pool, 1.0),
        'cache_win': nrm((DEPTH, DEC_BATCH, win_buf, 2, N_KV_HEADS, HEAD_DIM), 1.0),
        'state_ssm_re': nrm((DEPTH, DEC_BATCH, SSM_GROUPS, SSM_STATE), 0.5),
        'state_ssm_im': nrm((DEPTH, DEC_BATCH, SSM_GROUPS, SSM_STATE), 0.5),
        'page_table': page_table,
        'norm_mix': 1.0 + nrm((DEPTH, D_MODEL), 0.01),
        'w_in': nrm((DEPTH, D_MODEL, D_IN), D_MODEL ** -0.5),
        'ssm_a_re': -0.5 + nrm((DEPTH, SSM_GROUPS, SSM_STATE), 0.01),
        'ssm_a_im': jnp.pi * jnp.arange(SSM_STATE, dtype=jnp.float32) + nrm((DEPTH, SSM_GROUPS, SSM_STATE), 0.01),
        'ssm_b_re': nrm((DEPTH, SSM_GROUPS, SSM_STATE, SSM_GROUP), (2 * SSM_GROUP) ** -0.5),
        'ssm_b_im': nrm((DEPTH, SSM_GROUPS, SSM_STATE, SSM_GROUP), (2 * SSM_GROUP) ** -0.5),
        'ssm_c_re': nrm((DEPTH, SSM_GROUPS, SSM_GROUP, SSM_STATE), (2 * SSM_STATE) ** -0.5),
        'ssm_c_im': nrm((DEPTH, SSM_GROUPS, SSM_GROUP, SSM_STATE), (2 * SSM_STATE) ** -0.5),
        'ssm_d': nrm((DEPTH, D_SSM), 0.5),
        'ssm_log_dt': jax.random.uniform(nxt(), (DEPTH, SSM_GROUPS), jnp.float32, math.log(1e-3), math.log(1e-1)),
        'ssm_w_glu': nrm((DEPTH, D_SSM, D_SSM), D_SSM ** -0.5),
        'ssm_b_glu': nrm((DEPTH, D_SSM), 0.01),
        'cmp_pe': nrm((DEPTH, 2, CMP_BLOCK, HEAD_DIM), 0.1),
        'cmp_w1': nrm((DEPTH, 2, CMP_BLOCK, HEAD_DIM, CMP_HIDDEN), (CMP_BLOCK * HEAD_DIM) ** -0.5),
        'cmp_b1': nrm((DEPTH, 2, CMP_HIDDEN), 0.01),
        'cmp_w2': nrm((DEPTH, 2, CMP_HIDDEN, HEAD_DIM), CMP_HIDDEN ** -0.5),
        'w_br_ssm': nrm((DEPTH, D_SSM, D_MODEL), D_SSM ** -0.5),
        'w_br_nsa': nrm((DEPTH, D_NSA, D_MODEL), D_NSA ** -0.5),
        'w_out': nrm((DEPTH, D_MODEL, D_MODEL), D_MODEL ** -0.5),
        'norm_ffn': 1.0 + nrm((DEPTH, D_MODEL), 0.01),
        'ffn_w_gate': nrm((N_DENSE, D_MODEL, D_FF), D_MODEL ** -0.5),
        'ffn_w_up': nrm((N_DENSE, D_MODEL, D_FF), D_MODEL ** -0.5),
        'ffn_w_down': nrm((N_DENSE, D_FF, D_MODEL), D_FF ** -0.5),
        'moe_router': nrm((N_MOE, D_MODEL, N_EXPERTS), D_MODEL ** -0.5),
        'moe_w_gate': nrm((N_MOE, N_EXPERTS, D_MODEL, D_FF_EXPERT), D_MODEL ** -0.5),
        'moe_w_up': nrm((N_MOE, N_EXPERTS, D_MODEL, D_FF_EXPERT), D_MODEL ** -0.5),
        'moe_w_down': nrm((N_MOE, N_EXPERTS, D_FF_EXPERT, D_MODEL), D_FF_EXPERT ** -0.5),
        'norm_final': 1.0 + nrm((D_MODEL,), 0.01),
    }


def reference(x_prompt, x_sample, cache_cmp, cache_sel, cache_win, state_ssm_re, state_ssm_im, page_table,
              norm_mix, w_in, ssm_a_re, ssm_a_im, ssm_b_re, ssm_b_im, ssm_c_re, ssm_c_im, ssm_d, ssm_log_dt,
              ssm_w_glu, ssm_b_glu, cmp_pe, cmp_w1, cmp_b1, cmp_w2, w_br_ssm, w_br_nsa, w_out,
              norm_ffn, ffn_w_gate, ffn_w_up, ffn_w_down, moe_router, moe_w_gate, moe_w_up, moe_w_down,
              norm_final):
    db, n_pages = page_table.shape
    past_len = n_pages * cache_cmp.shape[2]
    win_buf = cache_win.shape[2]
    xp, xs = x_prompt, x_sample
    tp, ts = xp.shape[1], xs.shape[1]
    cmp_p, sel_p, win_p, sre_p, sim_p = [], [], [], [], []
    cmp_s, sel_s, win_s, sre_s, sim_s = [], [], [], [], []
    for layer in range(DEPTH):
        ssm_w = (ssm_a_re[layer], ssm_a_im[layer], ssm_b_re[layer], ssm_b_im[layer], ssm_c_re[layer],
                 ssm_c_im[layer], ssm_d[layer], ssm_log_dt[layer], ssm_w_glu[layer], ssm_b_glu[layer])
        cmp_k = (cmp_pe[layer, 0], cmp_w1[layer, 0], cmp_b1[layer, 0], cmp_w2[layer, 0])
        cmp_v = (cmp_pe[layer, 1], cmp_w1[layer, 1], cmp_b1[layer, 1], cmp_w2[layer, 1])
        out_w = (w_br_ssm[layer], w_br_nsa[layer], w_out[layer])

        hp = rmsnorm(xp, norm_mix[layer])
        u, q, kvc, kvs, kvw, gl, g_ssm, g_nsa = split_in(hp, w_in[layer])
        ssm_out, s_re, s_im = s5_branch(u, None, None, *ssm_w)
        kc = compress_blocks(kvc[:, :, 0], *cmp_k)
        vc = compress_blocks(kvc[:, :, 1], *cmp_v)
        nsa_out = nsa_prompt(q, gl, kc, vc, to_sel_blocks(kvs[:, :, 0]), to_sel_blocks(kvs[:, :, 1]), kvw)
        xp = xp + merge(ssm_out, nsa_out, g_ssm, g_nsa, *out_w)
        cmp_p.append(kvc)
        sel_p.append(kvs)
        win_p.append(kvw[:, tp - min(WINDOW, tp):])
        sre_p.append(s_re)
        sim_p.append(s_im)

        hs = rmsnorm(xs, norm_mix[layer])
        u, q, kvc, kvs, kvw, gl, g_ssm, g_nsa = split_in(hs, w_in[layer])
        ssm_out, s_re, s_im = s5_branch(u, state_ssm_re[layer], state_ssm_im[layer], *ssm_w)
        past_c = cache_cmp[layer, page_table].reshape(db, past_len, 2, N_KV_HEADS, HEAD_DIM)
        past_s = cache_sel[layer, page_table].reshape(db, past_len, 2, N_KV_HEADS, HEAD_DIM)
        seq_c = jnp.concatenate([past_c, kvc], axis=1)
        seq_s = jnp.concatenate([past_s, kvs], axis=1)
        win = jnp.concatenate([cache_win[layer], kvw], axis=1)
        kc = compress_blocks(seq_c[:, :, 0], *cmp_k)
        vc = compress_blocks(seq_c[:, :, 1], *cmp_v)
        q_pos = past_len + jnp.arange(ts)
        kw_pos = past_len - win_buf + jnp.arange(win_buf + ts)
        nsa_out = nsa_attend(q, gl, q_pos, kc, vc, cmp_end_positions(kc.shape[1]),
                             to_sel_blocks(seq_s[:, :, 0]), to_sel_blocks(seq_s[:, :, 1]),
                             win[:, :, 0], win[:, :, 1], kw_pos)
        xs = xs + merge(ssm_out, nsa_out, g_ssm, g_nsa, *out_w)
        cmp_s.append(kvc)
        sel_s.append(kvs)
        win_s.append(win[:, ts:])
        sre_s.append(s_re)
        sim_s.append(s_im)

        hp = rmsnorm(xp, norm_ffn[layer])
        hs = rmsnorm(xs, norm_ffn[layer])
        if layer % 2 == 0:
            i = layer // 2
            xp = xp + swiglu(hp, ffn_w_gate[i], ffn_w_up[i], ffn_w_down[i])
            xs = xs + swiglu(hs, ffn_w_gate[i], ffn_w_up[i], ffn_w_down[i])
        else:
            i = layer // 2
            xp = xp + moe_swiglu(hp, moe_router[i], moe_w_gate[i], moe_w_up[i], moe_w_down[i])
            xs = xs + moe_swiglu(hs, moe_router[i], moe_w_gate[i], moe_w_up[i], moe_w_down[i])

    y_prompt = rmsnorm(xp, norm_final)
    y_sample = rmsnorm(xs, norm_final)
    return (y_prompt, y_sample,
            jnp.stack(cmp_p), jnp.stack(sel_p), jnp.stack(win_p), jnp.stack(sre_p), jnp.stack(sim_p),
            jnp.stack(cmp_s), jnp.stack(sel_s), jnp.stack(win_s), jnp.stack(sre_s), jnp.stack(sim_s))
```

```python
import functools
import math

import numpy as np
import jax
import jax.numpy as jnp
from jax import lax
from jax.experimental import pallas as pl
from jax.experimental.pallas import tpu as pltpu

F32 = jnp.float32
BF16 = jnp.bfloat16

SSM_GROUP = 16
N_HEADS = 16
HEAD_DIM = 128
N_KV_HEADS = 2
Q_PER_KV = N_HEADS // N_KV_HEADS
CMP_STRIDE = 16
CMP_BLOCK = 2 * CMP_STRIDE
SEL_BLOCK = 64
N_SEL = 16
N_LOCAL_SEL = 2
WINDOW = 512
Q_BLOCK = 128
TOP_K = 2
RMS_EPS = 1e-6
NEG_INF = -1e30
BIG = 1e30

LANES = 128
SUBLANES_F32 = 8
SUBLANES_BF16 = 16
VMEM_BYTES_V7X = 64 * 1024 * 1024
VMEM_LIMIT_CAP = VMEM_BYTES_V7X - 6 * 1024 * 1024
ROW_PAD = 64


def _vmem_limit(estimate_bytes):
    return int(min(max(estimate_bytes * 5 // 4 + (4 << 20), 16 << 20), VMEM_LIMIT_CAP))


def _pick_tile(total, target, mult):
    best = None
    for t in range(mult, min(total, target) + 1, mult):
        if total % t == 0:
            best = t
    assert best is not None, (total, target, mult)
    return best


def _nbytes(shape, dtype):
    n = 1
    for s in shape:
        if s is not None:
            n *= s
    return n * jnp.dtype(dtype).itemsize


def _gelu_tanh(x):
    return 0.5 * x * (1.0 + jnp.tanh(math.sqrt(2.0 / math.pi) * (x + 0.044715 * (x * x * x))))


def _sigmoid(x):
    return 1.0 / (1.0 + jnp.exp(-x))


def _fused_matmul(*, xs, dots, extras=(), epilogue, M, N, K, tm, tn, tk, out_dtype,
                  out_rows=None, out_map=None, order="mn", prefetch=(), x_bias=None,
                  n_used_tiles=False, name):
    assert M % tm == 0 and N % tn == 0 and K % tk == 0, (M, N, K, tm, tn, tk)
    nm, nn, nk = M // tm, N // tn, K // tk
    nd, nx, ne, npf = len(dots), len(xs), len(extras), len(prefetch)
    use_acc = nk > 1
    out_rows = M if out_rows is None else out_rows

    def wrap(f):
        if order == "mn":
            return lambda a, b, k, *pf: f(a, b, k, *pf)
        return lambda a, b, k, *pf: f(b, a, k, *pf)

    def body(*refs):
        pf = refs[:npf]
        x_refs = refs[npf:npf + nx]
        pos = npf + nx
        xb_ref = None
        if x_bias is not None:
            xb_ref = refs[pos]
            pos += 1
        w_refs = refs[pos:pos + nd]
        pos += nd
        e_refs = refs[pos:pos + ne]
        pos += ne
        o_ref = refs[pos]
        acc_refs = refs[pos + 1:]
        m_id = pl.program_id(0 if order == "mn" else 1)

        def compute():
            xv = []
            for xr in x_refs:
                v = xr[...]
                if xb_ref is not None:
                    v = v + xb_ref[...]
                xv.append(v.astype(BF16))
            parts = [jnp.dot(xv[d[0]], w_refs[i][...].astype(BF16), preferred_element_type=F32)
                     for i, d in enumerate(dots)]
            if not use_acc:
                o_ref[...] = epilogue(parts, [e[...] for e in e_refs]).astype(out_dtype)
                return
            k = pl.program_id(2)

            @pl.when(k == 0)
            def _():
                for i in range(nd):
                    acc_refs[i][...] = parts[i]

            @pl.when(k > 0)
            def _():
                for i in range(nd):
                    acc_refs[i][...] += parts[i]

            @pl.when(k == nk - 1)
            def _():
                o_ref[...] = epilogue([a[...] for a in acc_refs],
                                      [e[...] for e in e_refs]).astype(out_dtype)

        if n_used_tiles:
            pl.when(m_id < pf[-1][0])(compute)
        else:
            compute()

    in_specs = [pl.BlockSpec((tm, tk), wrap(f)) for _, f in xs]
    args = [a for a, _ in xs]
    est = sum(2 * _nbytes((tm, tk), a.dtype) for a, _ in xs)
    if x_bias is not None:
        in_specs.append(pl.BlockSpec((1, tk), wrap(lambda m, n, k, *pf: (0, k))))
        args.append(x_bias)
    for _, w, bs, f in dots:
        in_specs.append(pl.BlockSpec(bs, wrap(f)))
        args.append(w)
        est += 2 * _nbytes(bs, w.dtype) + _nbytes(bs, BF16)
    for a, bs, f in extras:
        in_specs.append(pl.BlockSpec(bs, wrap(f)))
        args.append(a)
        est += 2 * _nbytes(bs, a.dtype)
    if out_map is None:
        out_map = lambda m, n, k, *pf: (m, n)
    est += 2 * _nbytes((tm, tn), out_dtype) + (2 + nd) * _nbytes((tm, tn), F32)
    scratch = [pltpu.VMEM((tm, tn), F32) for _ in range(nd)] if use_acc else []
    grid = (nm, nn, nk) if order == "mn" else (nn, nm, nk)
    return pl.pallas_call(
        body,
        grid_spec=pltpu.PrefetchScalarGridSpec(
            num_scalar_prefetch=npf, grid=grid, in_specs=in_specs,
            out_specs=pl.BlockSpec((tm, tn), wrap(out_map)), scratch_shapes=scratch),
        out_shape=jax.ShapeDtypeStruct((out_rows, N), out_dtype),
        compiler_params=pltpu.CompilerParams(
            dimension_semantics=("parallel", "parallel", "arbitrary"),
            vmem_limit_bytes=_vmem_limit(est)),
        name=name,
    )(*prefetch, *args)


def _first(parts, extras):
    return parts[0]


def _dense(x, w, w_lead, col0, N, *, tm, tn, epilogue=_first, extras=(), out_dtype=F32, name):
    M, K = x.shape
    assert col0 % tn == 0
    c0 = col0 // tn
    nl = len(w_lead)
    return _fused_matmul(
        xs=[(x, lambda m, n, k, *pf: (m, 0))],
        dots=[(0, w, (None,) * nl + (K, tn), lambda m, n, k, *pf: tuple(w_lead) + (0, c0 + n))],
        extras=extras, epilogue=epilogue, M=M, N=N, K=K, tm=tm, tn=tn, tk=K,
        out_dtype=out_dtype, name=name)


def _rmsnorm(x, g, out_dtype, name):
    M, D = x.shape
    tr = _pick_tile(M, 256, SUBLANES_BF16)

    def body(x_ref, g_ref, o_ref):
        xf = x_ref[...]
        ms = jnp.mean(xf * xf, axis=-1, keepdims=True)
        o_ref[...] = (xf * lax.rsqrt(ms + RMS_EPS) * g_ref[...]).astype(out_dtype)

    return pl.pallas_call(
        body, grid=(M // tr,),
        in_specs=[pl.BlockSpec((tr, D), lambda i: (i, 0)), pl.BlockSpec((1, D), lambda i: (0, 0))],
        out_specs=pl.BlockSpec((tr, D), lambda i: (i, 0)),
        out_shape=jax.ShapeDtypeStruct((M, D), out_dtype),
        compiler_params=pltpu.CompilerParams(dimension_semantics=("parallel",)),
        name=name,
    )(x, g.reshape(1, D))


def _s5_discretize(a_re, a_im, log_dt_b, b_re_t, b_im_t):
    C, GP = b_re_t.shape

    def body(ar_ref, ai_ref, ld_ref, br_ref, bi_ref, lr_ref, li_ref, obr_ref, obi_ref):
        ar, ai = ar_ref[...], ai_ref[...]
        dt = jnp.exp(ld_ref[...])
        mag = jnp.exp(ar * dt)
        ang = ai * dt
        lr = mag * jnp.cos(ang)
        li = mag * jnp.sin(ang)
        lr_ref[...] = lr
        li_ref[...] = li
        x, y = lr - 1.0, li
        den = ar * ar + ai * ai
        cr = (x * ar + y * ai) / den
        ci = (y * ar - x * ai) / den
        br, bi = br_ref[...], bi_ref[...]
        obr_ref[...] = cr * br - ci * bi
        obi_ref[...] = cr * bi + ci * br

    vec = jax.ShapeDtypeStruct((1, GP), F32)
    mat = jax.ShapeDtypeStruct((C, GP), F32)
    return pl.pallas_call(body, out_shape=(vec, vec, mat, mat), name="s5_discretize")(
        a_re, a_im, log_dt_b, b_re_t, b_im_t)


def _s5_scan(bu3, lam_re, lam_im, s0_re, s0_im, *, row0, nb, T, Tc, prev=None, name):
    Mp, R2, _ = bu3.shape
    R = R2 // 2
    assert T % Tc == 0 and row0 % Tc == 0
    nc = T // Tc
    blk0 = row0 // Tc

    def body(bre, bim, lre, lim, s0r, s0i, *rest):
        ore, oim, fre, fim, st = rest[-5:]
        c = pl.program_id(1)

        @pl.when(c == 0)
        def _():
            st[0] = s0r[...]
            st[1] = s0i[...]

        lr, li = lre[...], lim[...]

        def step(t, carry):
            sr, si = carry
            nr = lr * sr - li * si + bre[t]
            ni = lr * si + li * sr + bim[t]
            ore[t] = nr
            oim[t] = ni
            return nr, ni

        sr, si = lax.fori_loop(0, Tc, step, (st[0], st[1]))
        st[0] = sr
        st[1] = si
        fre[...] = sr
        fim[...] = si

    row_map_re = lambda b, c: (blk0 + b * nc + c, 0, 0)
    row_map_im = lambda b, c: (blk0 + b * nc + c, 1, 0)
    in_specs = [pl.BlockSpec((Tc, R, LANES), row_map_re), pl.BlockSpec((Tc, R, LANES), row_map_im),
                pl.BlockSpec((R, LANES), lambda b, c: (0, 0)), pl.BlockSpec((R, LANES), lambda b, c: (0, 0)),
                pl.BlockSpec((None, R, LANES), lambda b, c: (b, 0, 0)),
                pl.BlockSpec((None, R, LANES), lambda b, c: (b, 0, 0))]
    args = [bu3, bu3, lam_re, lam_im, s0_re, s0_im]
    aliases = {}
    if prev is not None:
        in_specs += [pl.BlockSpec(memory_space=pl.ANY), pl.BlockSpec(memory_space=pl.ANY)]
        args += list(prev)
        aliases = {6: 0, 7: 1}
    st_shape = jax.ShapeDtypeStruct((Mp, R, LANES), F32)
    fin_shape = jax.ShapeDtypeStruct((nb, R, LANES), F32)
    return pl.pallas_call(
        body, grid=(nb, nc), in_specs=in_specs,
        out_specs=[pl.BlockSpec((Tc, R, LANES), row_map_re), pl.BlockSpec((Tc, R, LANES), row_map_re),
                   pl.BlockSpec((None, R, LANES), lambda b, c: (b, 0, 0)),
                   pl.BlockSpec((None, R, LANES), lambda b, c: (b, 0, 0))],
        out_shape=(st_shape, st_shape, fin_shape, fin_shape),
        scratch_shapes=[pltpu.VMEM((2, R, LANES), F32)],
        input_output_aliases=aliases,
        compiler_params=pltpu.CompilerParams(dimension_semantics=("parallel", "arbitrary")),
        name=name,
    )(*args)


GROUP_TILE = 16


def _block_diag_tiles(w, inner_first):
    eye = jnp.eye(GROUP_TILE, dtype=w.dtype)
    nt, gt, a, b = w.shape
    return jnp.einsum("tgab,gh->tgahb", w, eye).reshape(nt, gt * a, gt * b)


def _s5_branch(uq, u_cols, s0_re, s0_im, n_prompt, n_seq, seq_len, params, layer, tm):
    (a_re, a_im, b_re, b_im, c_re, c_im, d, log_dt, w_glu, b_glu) = params
    Mp = uq.shape[0]
    G, P = a_re.shape[1], a_re.shape[2]
    C = SSM_GROUP
    GP = G * P
    R = GP // LANES
    ngt = G // GROUP_TILE
    kx, nxw = GROUP_TILE * C, GROUP_TILE * P

    lam_re, lam_im, bb_re, bb_im = _s5_discretize(
        a_re[layer].reshape(1, GP), a_im[layer].reshape(1, GP),
        jnp.broadcast_to(log_dt[layer][:, None], (G, P)).reshape(1, GP),
        b_re[layer].transpose(2, 0, 1).reshape(C, GP), b_im[layer].transpose(2, 0, 1).reshape(C, GP))

    def b_tiles(bb):
        t = bb.reshape(C, ngt, GROUP_TILE, P).transpose(1, 2, 0, 3)
        return _block_diag_tiles(t, True)

    w_b = jnp.stack([b_tiles(bb_re), b_tiles(bb_im)])

    def c_tiles(cc):
        t = cc.reshape(ngt, GROUP_TILE, C, P).transpose(0, 1, 3, 2)
        return _block_diag_tiles(t, True)

    w_c = jnp.stack([c_tiles(c_re[layer]), -c_tiles(c_im[layer])])

    bu = _fused_matmul(
        xs=[(uq, lambda m, n, k, *pf: (m, n % ngt))],
        dots=[(0, w_b, (None, None, kx, nxw), lambda m, n, k, *pf: (n // ngt, n % ngt, 0, 0))],
        epilogue=_first, M=Mp, N=2 * GP, K=kx, tm=tm, tn=nxw, tk=kx, out_dtype=F32, name="s5_bu")
    bu3 = bu.reshape(Mp, 2 * R, LANES)
    lam_re2, lam_im2 = lam_re.reshape(R, LANES), lam_im.reshape(R, LANES)
    zeros = jnp.zeros((n_seq, R, LANES), F32)
    tc = _pick_tile(seq_len, 64, 1)
    s_re, s_im, fp_re, fp_im = _s5_scan(bu3, lam_re2, lam_im2, zeros, zeros, row0=0, nb=n_seq,
                                        T=seq_len, Tc=tc, name="s5_scan_prompt")
    n_tail = Mp - n_prompt
    s_re, s_im, fs_re, fs_im = _s5_scan(bu3, lam_re2, lam_im2, s0_re, s0_im, row0=n_prompt, nb=n_tail,
                                        T=1, Tc=1, prev=(s_re, s_im), name="s5_scan_sample")

    d2 = d[layer].reshape(1, u_cols)
    ny = GROUP_TILE * C
    z = _fused_matmul(
        xs=[(s_re.reshape(Mp, GP), lambda m, n, k, *pf: (m, n)),
            (s_im.reshape(Mp, GP), lambda m, n, k, *pf: (m, n))],
        dots=[(0, w_c, (None, None, nxw, ny), lambda m, n, k, *pf: (0, n, 0, 0)),
              (1, w_c, (None, None, nxw, ny), lambda m, n, k, *pf: (1, n, 0, 0))],
        extras=[(uq, (tm, ny), lambda m, n, k, *pf: (m, n)), (d2, (1, ny), lambda m, n, k, *pf: (0, n))],
        epilogue=lambda p, e: _gelu_tanh(p[0] + p[1] + e[1] * e[0]),
        M=Mp, N=u_cols, K=nxw, tm=tm, tn=ny, tk=nxw, out_dtype=F32, name="s5_y")

    tn = _pick_tile(u_cols, 512, LANES)
    ssm_out = _dense(z, w_glu, (layer,), 0, u_cols, tm=tm, tn=tn,
                     extras=[(z, (tm, tn), lambda m, n, k, *pf: (m, n)),
                             (b_glu[layer].reshape(1, u_cols), (1, tn), lambda m, n, k, *pf: (0, n))],
                     epilogue=lambda p, e: e[0] * _sigmoid(p[0] + e[1]),
                     out_dtype=BF16, name="s5_glu")
    return ssm_out, (fp_re, fp_im), (fs_re, fs_im)


KV_COLS = 2 * N_KV_HEADS * HEAD_DIM
CHUNK_COLS = CMP_STRIDE * KV_COLS


def _compress_lead_trail(srcs, grid, pe, w1, rows_per_src, n_rows, prefetch=(), name="cmp_lt"):
    ns = len(srcs)
    npf = len(prefetch)
    rows = ns * rows_per_src
    kd = CMP_STRIDE * HEAD_DIM

    def body(*refs):
        x_refs = refs[npf:npf + ns]
        pe_ref, w_ref, o_ref = refs[npf + ns:]
        for kv in range(2):
            per_head = []
            for h in range(N_KV_HEADS):
                c0 = (kv * N_KV_HEADS + h) * HEAD_DIM
                per_src = [jnp.concatenate([xr[:, j * KV_COLS + c0: j * KV_COLS + c0 + HEAD_DIM]
                                            for j in range(CMP_STRIDE)], axis=1) for xr in x_refs]
                per_head.append(per_src[0] if ns == 1 else jnp.concatenate(per_src, axis=0))
            xs = jnp.concatenate(per_head, axis=0)
            for lt in range(2):
                xv = (xs + pe_ref[kv, lt]).astype(BF16)
                r = jnp.dot(xv, w_ref[kv, lt].astype(BF16), preferred_element_type=F32)
                for h in range(N_KV_HEADS):
                    o_ref[kv, lt, h] = r[h * rows:(h + 1) * rows]

    in_specs = [pl.BlockSpec(bs, f) for _, bs, f in srcs]
    in_specs += [pl.BlockSpec((2, 2, 1, kd), lambda *a: (0, 0, 0, 0)),
                 pl.BlockSpec((2, 2, kd, HEAD_DIM), lambda *a: (0, 0, 0, 0))]
    step_of = lambda *a: a[0] * grid[1] + a[1]
    return pl.pallas_call(
        body,
        grid_spec=pltpu.PrefetchScalarGridSpec(
            num_scalar_prefetch=npf, grid=grid, in_specs=in_specs,
            out_specs=pl.BlockSpec((2, 2, N_KV_HEADS, rows, HEAD_DIM),
                                   lambda *a: (0, 0, 0, step_of(*a), 0))),
        out_shape=jax.ShapeDtypeStruct((2, 2, N_KV_HEADS, n_rows, HEAD_DIM), F32),
        compiler_params=pltpu.CompilerParams(
            dimension_semantics=("parallel", "arbitrary"), vmem_limit_bytes=48 << 20),
        name=name,
    )(*prefetch, *[a for a, _, _ in srcs], pe, w1)


def _compress_finish(lt, n_seq, n_chunk, b1, w2, name):
    lead = lt[:, 0]
    trail = lt[:, 1].reshape(2, N_KV_HEADS, n_seq, n_chunk, HEAD_DIM)
    trail = jnp.roll(trail, -1, axis=3).reshape(lead.shape)
    rows = N_KV_HEADS * n_seq * n_chunk
    lead = lead.reshape(2, rows, HEAD_DIM)
    trail = trail.reshape(2, rows, HEAD_DIM)
    rt = _pick_tile(rows, 1024, SUBLANES_F32)

    def body(a_ref, t_ref, b_ref, w_ref, o_ref):
        hid = _gelu_tanh(a_ref[...] + t_ref[...] + b_ref[...])
        o_ref[...] = jnp.dot(hid.astype(BF16), w_ref[...].astype(BF16), preferred_element_type=F32)

    out = pl.pallas_call(
        body, grid=(2, rows // rt),
        in_specs=[pl.BlockSpec((None, rt, HEAD_DIM), lambda kv, r: (kv, r, 0)),
                  pl.BlockSpec((None, rt, HEAD_DIM), lambda kv, r: (kv, r, 0)),
                  pl.BlockSpec((None, 1, HEAD_DIM), lambda kv, r: (kv, 0, 0)),
                  pl.BlockSpec((None, HEAD_DIM, HEAD_DIM), lambda kv, r: (kv, 0, 0))],
        out_specs=pl.BlockSpec((None, rt, HEAD_DIM), lambda kv, r: (kv, r, 0)),
        out_shape=jax.ShapeDtypeStruct((2, rows, HEAD_DIM), F32),
        compiler_params=pltpu.CompilerParams(dimension_semantics=("parallel", "parallel")),
        name=name,
    )(lead, trail, b1.reshape(2, 1, HEAD_DIM), w2)
    return out.reshape(2, N_KV_HEADS, n_seq, n_chunk, HEAD_DIM)


def _overlap_matrix(n_cmp, n_blk, rows, cols):
    start = np.arange(rows)[:, None] * CMP_STRIDE
    js = np.arange(cols)[None, :] * SEL_BLOCK
    ov = (start < js + SEL_BLOCK) & (start + CMP_BLOCK > js)
    ov &= (np.arange(rows)[:, None] < n_cmp) & (np.arange(cols)[None, :] < n_blk)
    return jnp.asarray(ov.astype(np.float32))


def _masked_softmax(s, valid):
    s = jnp.where(valid, s, NEG_INF)
    p = jnp.exp(s - jnp.max(s, axis=-1, keepdims=True)) * jnp.where(valid, 1.0, 0.0)
    return p / jnp.maximum(jnp.sum(p, axis=-1, keepdims=True), 1e-30)


_NT = (((1,), (1,)), ((), ()))


def _nsa_prompt(uq, q_blk0, glg, cmp_kv, kvs, kvw, *, n_seq, seq_len, out_rows):
    T = seq_len
    QB = Q_BLOCK
    HG = Q_PER_KV
    HD = HEAD_DIM
    nqb = T // QB
    ncp = cmp_kv.shape[3]
    n_cmp = T // CMP_STRIDE - 1
    n_blk = -(-T // SEL_BLOCK)
    n_pick = min(N_SEL, n_blk)
    assert T % QB == 0 and T % SEL_BLOCK == 0 and n_blk <= LANES and ncp <= LANES
    CH = min(512, T)
    assert T % CH == 0
    WB = min(WINDOW + QB, T)
    scale = HD ** -0.5
    blk_of_key = np.arange(T) // SEL_BLOCK
    expand = (np.arange(LANES)[:, None] == blk_of_key[None, :]).astype(np.float32)
    expand = jnp.asarray(expand.reshape(LANES, T // CH, CH).transpose(1, 0, 2), dtype=BF16)
    ov = _overlap_matrix(n_cmp, n_blk, ncp, LANES)

    def body(q_ref, gl_ref, kc_ref, vc_ref, ks_ref, vs_ref, kw_ref, vw_ref, e_ref, ov_ref, o_ref,
             m_ref, l_ref, acc_ref):
        i = pl.program_id(2)
        start = i * QB
        q = q_ref[...]
        q_st = jnp.concatenate([q[:, g * HD:(g + 1) * HD] for g in range(HG)], axis=0).astype(BF16)
        qpos = start + lax.broadcasted_iota(jnp.int32, (QB, 1), 0)

        s = lax.dot_general(q_st, kc_ref[...].astype(BF16), _NT, preferred_element_type=F32) * scale
        n_idx = lax.broadcasted_iota(jnp.int32, (QB, ncp), 1)
        cvalid = (n_idx * CMP_STRIDE + (CMP_BLOCK - 1) <= qpos) & (n_idx < n_cmp)
        p_c = _masked_softmax(s.reshape(HG, QB, ncp), cvalid)
        o_c = jnp.dot(p_c.reshape(HG * QB, ncp).astype(BF16), vc_ref[...].astype(BF16),
                      preferred_element_type=F32).reshape(HG, QB, HD)

        imp = jnp.dot(jnp.sum(p_c, axis=0), ov_ref[...], preferred_element_type=F32,
                      precision=lax.Precision.HIGHEST)
        blk = lax.broadcasted_iota(jnp.int32, (QB, LANES), 1)
        cur = qpos // SEL_BLOCK
        forced = (blk == 0) | ((blk <= cur) & (blk > cur - N_LOCAL_SEL))
        score = jnp.where(blk > cur, -BIG, jnp.where(forced, BIG, imp))
        rank = jnp.zeros((QB, LANES), F32)
        for j2 in range(n_blk):
            col = score[:, j2:j2 + 1]
            ahead = (col > score) | ((col == score) & (blk > j2))
            rank = rank + jnp.where(ahead, 1.0, 0.0)
        sel = jnp.where((rank < n_pick) & (blk < n_blk), 1.0, 0.0).astype(BF16)

        m_ref[...] = jnp.full(m_ref.shape, NEG_INF, F32)
        l_ref[...] = jnp.zeros(l_ref.shape, F32)
        acc_ref[...] = jnp.zeros(acc_ref.shape, F32)

        def chunk(c, carry):
            k0 = pl.multiple_of(c * CH, CH)
            kch = ks_ref[pl.ds(k0, CH), :].astype(BF16)
            vch = vs_ref[pl.ds(k0, CH), :].astype(BF16)
            sc = lax.dot_general(q_st, kch, _NT, preferred_element_type=F32) * scale
            picked = jnp.dot(sel, e_ref[c], preferred_element_type=F32)
            kpos = k0 + lax.broadcasted_iota(jnp.int32, (QB, CH), 1)
            valid = (picked > 0.5) & (kpos <= qpos)
            s3 = jnp.where(valid, sc.reshape(HG, QB, CH), NEG_INF)
            m_old = m_ref[...]
            m_new = jnp.maximum(m_old, jnp.max(s3, axis=-1, keepdims=True))
            p = jnp.exp(s3 - m_new) * jnp.where(valid, 1.0, 0.0)
            alpha = jnp.exp(m_old - m_new)
            l_ref[...] = alpha * l_ref[...] + jnp.sum(p, axis=-1, keepdims=True)
            pv = jnp.dot(p.reshape(HG * QB, CH).astype(BF16), vch, preferred_element_type=F32)
            acc_ref[...] = alpha * acc_ref[...] + pv.reshape(HG, QB, HD)
            m_ref[...] = m_new
            return carry

        lax.fori_loop(0, (start + QB + CH - 1) // CH, chunk, 0)
        o_s = acc_ref[...] / jnp.maximum(l_ref[...], 1e-30)

        w0 = pl.multiple_of(jnp.minimum(jnp.maximum(start - WINDOW, 0), T - WB), QB)
        sw = lax.dot_general(q_st, kw_ref[pl.ds(w0, WB), :].astype(BF16), _NT,
                             preferred_element_type=F32) * scale
        dist = qpos - (w0 + lax.broadcasted_iota(jnp.int32, (QB, WB), 1))
        p_w = _masked_softmax(sw.reshape(HG, QB, WB), (dist >= 0) & (dist < WINDOW))
        o_w = jnp.dot(p_w.reshape(HG * QB, WB).astype(BF16), vw_ref[pl.ds(w0, WB), :].astype(BF16),
                      preferred_element_type=F32).reshape(HG, QB, HD)

        gate = _sigmoid(gl_ref[...])
        for g in range(HG):
            og = (gate[:, g:g + 1] * o_c[g] + gate[:, HG + g:HG + g + 1] * o_s[g]
                  + gate[:, 2 * HG + g:2 * HG + g + 1] * o_w[g])
            o_ref[:, g * HD:(g + 1) * HD] = og.astype(o_ref.dtype)

    row = lambda b, h, i: b * nqb + i
    kv_spec = lambda kv: pl.BlockSpec((T, HD), lambda b, h, i: (b, kv * N_KV_HEADS + h))
    return pl.pallas_call(
        body, grid=(n_seq, N_KV_HEADS, nqb),
        in_specs=[pl.BlockSpec((QB, HG * HD), lambda b, h, i: (row(b, h, i), q_blk0 + h)),
                  pl.BlockSpec((QB, LANES), lambda b, h, i: (row(b, h, i), h)),
                  pl.BlockSpec((None, None, None, ncp, HD), lambda b, h, i: (0, h, b, 0, 0)),
                  pl.BlockSpec((None, None, None, ncp, HD), lambda b, h, i: (1, h, b, 0, 0)),
                  kv_spec(0), kv_spec(1), kv_spec(0), kv_spec(1),
                  pl.BlockSpec((T // CH, LANES, CH), lambda b, h, i: (0, 0, 0)),
                  pl.BlockSpec((ncp, LANES), lambda b, h, i: (0, 0))],
        out_specs=pl.BlockSpec((QB, HG * HD), lambda b, h, i: (row(b, h, i), h)),
        out_shape=jax.ShapeDtypeStruct((out_rows, N_HEADS * HD), BF16),
        scratch_shapes=[pltpu.VMEM((HG, QB, 1), F32), pltpu.VMEM((HG, QB, 1), F32),
                        pltpu.VMEM((HG, QB, HD), F32)],
        compiler_params=pltpu.CompilerParams(
            dimension_semantics=("parallel", "parallel", "arbitrary"), vmem_limit_bytes=48 << 20),
        name="nsa_prompt",
    )(uq, glg, cmp_kv, cmp_kv, kvs, kvs, kvw, kvw, expand, ov)


def _nsa_sample_select(q_s, cmp_kv, past_len):
    S = q_s.shape[0]
    HG, HD = Q_PER_KV, HEAD_DIM
    nc = cmp_kv.shape[3]
    n_cmp = nc - 1
    q_pos = past_len
    n_blk = -(-(past_len + 1) // SEL_BLOCK)
    nbp = -(-n_blk // LANES) * LANES
    n_pick = min(N_SEL, n_blk)
    cur = q_pos // SEL_BLOCK
    scale = HD ** -0.5
    ov = _overlap_matrix(n_cmp, n_blk, nc, nbp)

    def body(q_ref, kc_ref, vc_ref, ov_ref, oc_ref, idx_ref):
        q = q_ref[...].astype(BF16)
        s = lax.dot_general(q, kc_ref[...].astype(BF16), _NT, preferred_element_type=F32) * scale
        n_idx = lax.broadcasted_iota(jnp.int32, (HG, nc), 1)
        cvalid = (n_idx * CMP_STRIDE + (CMP_BLOCK - 1) <= q_pos) & (n_idx < n_cmp)
        p_c = _masked_softmax(s, cvalid)
        oc_ref[...] = jnp.dot(p_c.astype(BF16), vc_ref[...].astype(BF16), preferred_element_type=F32)
        imp = jnp.sum(jnp.dot(p_c, ov_ref[...], preferred_element_type=F32,
                              precision=lax.Precision.HIGHEST), axis=0, keepdims=True)
        blk = lax.broadcasted_iota(jnp.int32, (1, nbp), 1)
        forced = (blk == 0) | ((blk <= cur) & (blk > cur - N_LOCAL_SEL))
        score = jnp.where(blk > cur, -BIG, jnp.where(forced, BIG, imp))
        col = jnp.broadcast_to(score, (LANES, nbp)).T[:, 0:1]
        jp = lax.broadcasted_iota(jnp.int32, (nbp, nbp), 0)
        jj = lax.broadcasted_iota(jnp.int32, (nbp, nbp), 1)
        ahead = ((col > score) | ((col == score) & (jp < jj))) & (jp < n_blk)
        rank = jnp.sum(jnp.where(ahead, 1.0, 0.0), axis=0, keepdims=True)
        lane = lax.broadcasted_iota(jnp.int32, (1, LANES), 1)
        out = jnp.zeros((1, LANES), jnp.int32)
        for r in range(n_pick):
            hit = (rank == float(r)) & (blk < n_blk)
            idx_r = jnp.sum(jnp.where(hit, blk, 0), axis=1, keepdims=True)
            out = jnp.where(lane == r, idx_r, out)
        idx_ref[...] = jnp.broadcast_to(out, (SUBLANES_F32, LANES))

    o_c, idx = pl.pallas_call(
        body, grid=(S, N_KV_HEADS),
        in_specs=[pl.BlockSpec((None, None, HG, HD), lambda b, h: (b, h, 0, 0)),
                  pl.BlockSpec((None, None, None, nc, HD), lambda b, h: (0, h, b, 0, 0)),
                  pl.BlockSpec((None, None, None, nc, HD), lambda b, h: (1, h, b, 0, 0)),
                  pl.BlockSpec((nc, nbp), lambda b, h: (0, 0))],
        out_specs=[pl.BlockSpec((None, None, HG, HD), lambda b, h: (b, h, 0, 0)),
                   pl.BlockSpec((None, None, SUBLANES_F32, LANES), lambda b, h: (b, h, 0, 0))],
        out_shape=(jax.ShapeDtypeStruct((S, N_KV_HEADS, HG, HD), F32),
                   jax.ShapeDtypeStruct((S, N_KV_HEADS, SUBLANES_F32, LANES), jnp.int32)),
        compiler_params=pltpu.CompilerParams(dimension_semantics=("parallel", "parallel")),
        name="nsa_sample_select",
    )(q_s, cmp_kv, cmp_kv, ov)
    return o_c, idx[:, :, 0, :n_pick], n_pick


def _nsa_sample_attend(q_s, o_c, gate_s, src, live, cache_sel_v, layer, new_sel, cache_win_v, new_win,
                       n_pick):
    S = q_s.shape[0]
    HG, HD = Q_PER_KV, HEAD_DIM
    wb = cache_win_v.shape[2]
    scale = HD ** -0.5
    nkeys = n_pick * SEL_BLOCK

    def body(src_ref, live_ref, q_ref, oc_ref, g_ref, *rest):
        k_refs = rest[:n_pick]
        v_refs = rest[n_pick:2 * n_pick]
        kn_ref, vn_ref, kw_ref, vw_ref, kwn_ref, vwn_ref, o_ref = rest[2 * n_pick:]
        b, h = pl.program_id(0), pl.program_id(1)
        base = (b * N_KV_HEADS + h) * n_pick
        q = q_ref[...]
        qb = q.astype(BF16)

        def with_new_key(s, valid, s_new, v_old, v_new):
            sm = jnp.where(valid, s, NEG_INF)
            m = jnp.maximum(jnp.max(sm, axis=-1, keepdims=True), s_new)
            p = jnp.exp(sm - m) * jnp.where(valid, 1.0, 0.0)
            p_new = jnp.exp(s_new - m)
            den = jnp.sum(p, axis=-1, keepdims=True) + p_new
            return (jnp.dot(p.astype(BF16), v_old, preferred_element_type=F32) + p_new * v_new) / den

        keys = jnp.concatenate([r[...] for r in k_refs], axis=0).astype(BF16)
        vals = jnp.concatenate([r[...] for r in v_refs], axis=0).astype(BF16)
        s = lax.dot_general(qb, keys, _NT, preferred_element_type=F32) * scale
        lane = lax.broadcasted_iota(jnp.int32, (1, nkeys), 1)
        livef = jnp.zeros((1, nkeys), F32)
        for r in range(n_pick):
            livef = jnp.where((lane >= r * SEL_BLOCK) & (lane < (r + 1) * SEL_BLOCK),
                              live_ref[base + r].astype(F32), livef)
        s_new = jnp.sum(q * kn_ref[...], axis=-1, keepdims=True) * scale
        o_s = with_new_key(s, livef > 0.5, s_new, vals, vn_ref[...])

        sw = lax.dot_general(qb, kw_ref[...].astype(BF16), _NT, preferred_element_type=F32) * scale
        wrow = lax.broadcasted_iota(jnp.int32, (1, wb), 1)
        sw_new = jnp.sum(q * kwn_ref[...], axis=-1, keepdims=True) * scale
        o_w = with_new_key(sw, wrow > wb - WINDOW, sw_new, vw_ref[...].astype(BF16), vwn_ref[...])

        gate = _sigmoid(g_ref[...])
        o_ref[...] = gate[:, 0:1] * oc_ref[...] + gate[:, 1:2] * o_s + gate[:, 2:3] * o_w

    def blk_spec(r, kv):
        return pl.BlockSpec((None, None, SEL_BLOCK, HD),
                            lambda b, h, src_ref, live_ref: (layer, src_ref[(b * N_KV_HEADS + h) * n_pick + r],
                                                             0, kv * N_KV_HEADS + h))

    qspec = pl.BlockSpec((None, None, HG, HD), lambda b, h, *_: (b, h, 0, 0))
    gspec = pl.BlockSpec((None, None, HG, LANES), lambda b, h, *_: (b, h, 0, 0))
    new_spec = lambda kv: pl.BlockSpec((None, None, 1, HD), lambda b, h, *_: (b, kv * N_KV_HEADS + h, 0, 0))
    win_spec = lambda kv: pl.BlockSpec((None, None, wb, HD), lambda b, h, *_: (layer, b, 0, kv * N_KV_HEADS + h))
    in_specs = ([qspec, qspec, gspec] + [blk_spec(r, 0) for r in range(n_pick)]
                + [blk_spec(r, 1) for r in range(n_pick)]
                + [new_spec(0), new_spec(1), win_spec(0), win_spec(1), new_spec(0), new_spec(1)])
    return pl.pallas_call(
        body,
        grid_spec=pltpu.PrefetchScalarGridSpec(
            num_scalar_prefetch=2, grid=(S, N_KV_HEADS), in_specs=in_specs,
            out_specs=pl.BlockSpec((None, None, HG, HD), lambda b, h, *_: (b, h, 0, 0))),
        out_shape=jax.ShapeDtypeStruct((S, N_KV_HEADS, HG, HD), F32),
        compiler_params=pltpu.CompilerParams(dimension_semantics=("parallel", "parallel")),
        name="nsa_sample_attend",
    )(src, live, q_s, o_c, gate_s, *([cache_sel_v] * (2 * n_pick)), new_sel, new_sel,
      cache_win_v, cache_win_v, new_win, new_win)


def _nsa_sample(q_s, gate_s, new_sel, new_win, cache_cmp, cache_sel, cache_win, page_table, layer,
                pe, w1, b1, w2):
    S, n_pages = page_table.shape
    depth, n_pool, page = cache_cmp.shape[:3]
    past_len = n_pages * page
    assert page % CMP_STRIDE == 0 and page % SEL_BLOCK == 0
    cpp = page // CMP_STRIDE
    bpp = page // SEL_BLOCK
    pps = _pick_tile(n_pages, 16, 1)
    cache_v = cache_cmp.reshape(depth, n_pool, cpp, CHUNK_COLS)
    pt_flat = page_table.reshape(-1)
    srcs = [(cache_v, (None, None, cpp, CHUNK_COLS),
             (lambda b, s, pt, p=p: (layer, pt[b * n_pages + s * pps + p], 0, 0))) for p in range(pps)]
    nc = n_pages * cpp
    lt = _compress_lead_trail(srcs, (S, n_pages // pps), pe, w1, cpp, S * nc, prefetch=(pt_flat,),
                              name="cmp_lt_sample")
    cmp_kv = _compress_finish(lt, S, nc, b1, w2, "cmp_finish_sample")
    o_c, idx, n_pick = _nsa_sample_select(q_s, cmp_kv, past_len)

    n_past_blk = past_len // SEL_BLOCK
    idc = jnp.minimum(idx, n_past_blk - 1)
    pages = jnp.take_along_axis(page_table[:, None, :], idc // bpp, axis=2)
    src = (pages * bpp + idc % bpp).astype(jnp.int32).reshape(-1)
    live = (idx < n_past_blk).astype(jnp.int32).reshape(-1)
    wb = cache_win.shape[2]
    out = _nsa_sample_attend(
        q_s, o_c, gate_s, src, live, cache_sel.reshape(depth, n_pool * bpp, SEL_BLOCK, KV_COLS), layer,
        new_sel, cache_win.reshape(depth, S, wb, KV_COLS), new_win, n_pick)
    return out.reshape(S, N_HEADS * HEAD_DIM)


MOE_ROW_TILE = 512


def _moe_route(x, g, w_router):
    M, D = x.shape
    E = w_router.shape[1]
    tr = _pick_tile(M, 256, SUBLANES_BF16)
    wr = jnp.pad(w_router, ((0, 0), (0, LANES - E)))

    def body(x_ref, g_ref, w_ref, h_ref, r_ref):
        xf = x_ref[...]
        ms = jnp.mean(xf * xf, axis=-1, keepdims=True)
        hf = xf * lax.rsqrt(ms + RMS_EPS) * g_ref[...]
        h_ref[...] = hf.astype(BF16)
        logits = jnp.dot(hf, w_ref[...], preferred_element_type=F32, precision=lax.Precision.HIGHEST)
        lane = lax.broadcasted_iota(jnp.int32, (tr, LANES), 1)
        logits = jnp.where(lane < E, logits, -jnp.inf)
        v1 = jnp.max(logits, axis=-1, keepdims=True)
        i1 = jnp.min(jnp.where(logits == v1, lane, LANES), axis=-1, keepdims=True)
        rest = jnp.where(lane == i1, -jnp.inf, logits)
        v2 = jnp.max(rest, axis=-1, keepdims=True)
        i2 = jnp.min(jnp.where(rest == v2, lane, LANES), axis=-1, keepdims=True)
        e2 = jnp.exp(v2 - v1)
        g1 = 1.0 / (1.0 + e2)
        out = jnp.where(lane == 0, i1.astype(F32), jnp.where(lane == 1, i2.astype(F32),
                        jnp.where(lane == 2, g1, jnp.where(lane == 3, e2 * g1, 0.0))))
        r_ref[...] = out

    return pl.pallas_call(
        body, grid=(M // tr,),
        in_specs=[pl.BlockSpec((tr, D), lambda i: (i, 0)), pl.BlockSpec((1, D), lambda i: (0, 0)),
                  pl.BlockSpec((D, LANES), lambda i: (0, 0))],
        out_specs=[pl.BlockSpec((tr, D), lambda i: (i, 0)), pl.BlockSpec((tr, LANES), lambda i: (i, 0))],
        out_shape=(jax.ShapeDtypeStruct((M, D), BF16), jax.ShapeDtypeStruct((M, LANES), F32)),
        compiler_params=pltpu.CompilerParams(dimension_semantics=("parallel",)),
        name="moe_route",
    )(x, g.reshape(1, D), wr)


def _moe(x_all, n_real, g, w_router, wg, wu, wd, li):
    Mp, D = x_all.shape
    E, _, F = wg.shape[1:]
    tm = MOE_ROW_TILE
    h, route = _moe_route(x_all, g, w_router[li])

    r = route[:n_real]
    e_flat = r[:, :TOP_K].astype(jnp.int32).reshape(-1)
    gates = r[:, TOP_K:2 * TOP_K].reshape(-1)
    n_pairs = n_real * TOP_K
    onehot = (e_flat[:, None] == jnp.arange(E)[None, :]).astype(jnp.int32)
    counts = jnp.sum(onehot, axis=0)
    within = jnp.take_along_axis(jnp.cumsum(onehot, axis=0) - onehot, e_flat[:, None], axis=1)[:, 0]
    padded = (counts + tm - 1) // tm * tm
    ends = jnp.cumsum(padded)
    dest = (ends - padded)[e_flat] + within
    n_tiles = -(-n_pairs // tm) + E
    rows = n_tiles * tm
    src_tok = jnp.zeros((rows,), jnp.int32).at[dest].set(jnp.arange(n_pairs, dtype=jnp.int32) // TOP_K)
    row_gate = jnp.zeros((rows,), F32).at[dest].set(gates)
    n_used = (ends[-1] // tm).astype(jnp.int32)
    tile_ids = jnp.minimum(jnp.arange(n_tiles, dtype=jnp.int32), n_used - 1)
    tile_expert = jnp.minimum(jnp.searchsorted(ends, tile_ids * tm, side="right"), E - 1).astype(jnp.int32)
    pf = (tile_expert, n_used.reshape(1))
    xs = jnp.take(h, src_tok, axis=0)

    tn = _pick_tile(F, 512, LANES)
    wmap = lambda m, n, k, te, nu: (li, te[m], 0, n)
    hmid = _fused_matmul(
        xs=[(xs, lambda m, n, k, *pf: (m, 0))],
        dots=[(0, wg, (None, None, D, tn), wmap), (0, wu, (None, None, D, tn), wmap)],
        epilogue=lambda p, e: p[0] * _sigmoid(p[0]) * p[1],
        M=rows, N=F, K=D, tm=tm, tn=tn, tk=D, out_dtype=BF16, order="nm", prefetch=pf,
        n_used_tiles=True, name="moe_up")

    kc = _pick_tile(F, 4096, LANES)
    nkc = F // kc
    tnd = _pick_tile(D, 512, LANES)
    gate_b = jnp.broadcast_to(row_gate[:, None], (rows, LANES))
    y = None
    for j in range(nkc):
        extras = [(gate_b, (tm, LANES), lambda m, n, k, *pf: (m, 0))]
        if y is not None:
            extras.append((y, (tm, tnd), lambda m, n, k, *pf: (m, n)))
        last = j == nkc - 1

        def epi(p, e, last=last, first=(y is None)):
            acc = p[0] if first else e[1] + p[0]
            return acc * e[0][:, 0:1] if last else acc

        y = _fused_matmul(
            xs=[(hmid, lambda m, n, k, *pf, j=j: (m, j))],
            dots=[(0, wd, (None, None, kc, tnd), lambda m, n, k, te, nu, j=j: (li, te[m], j, n))],
            extras=extras, epilogue=epi, M=rows, N=D, K=kc, tm=tm, tn=tnd, tk=kc, out_dtype=F32,
            order="nm", prefetch=pf, n_used_tiles=True, name=f"moe_down{j}")

    d2 = dest.reshape(n_real, TOP_K)
    y_tok = jnp.take(y, d2[:, 0], axis=0)
    for k in range(1, TOP_K):
        y_tok = y_tok + jnp.take(y, d2[:, k], axis=0)
    return x_all.at[:n_real].add(y_tok)


def _gate_logit_weights(w_gl):
    d = w_gl.shape[0]
    w = w_gl.reshape(d, N_KV_HEADS, Q_PER_KV, 3).transpose(0, 1, 3, 2).reshape(d, N_KV_HEADS, 3 * Q_PER_KV)
    return jnp.pad(w, ((0, 0), (0, 0), (0, LANES - 3 * Q_PER_KV))).reshape(d, N_KV_HEADS * LANES)


def kernel(x_prompt, x_sample, cache_cmp, cache_sel, cache_win, state_ssm_re, state_ssm_im, page_table,
           norm_mix, w_in, ssm_a_re, ssm_a_im, ssm_b_re, ssm_b_im, ssm_c_re, ssm_c_im, ssm_d, ssm_log_dt,
           ssm_w_glu, ssm_b_glu, cmp_pe, cmp_w1, cmp_b1, cmp_w2, w_br_ssm, w_br_nsa, w_out,
           norm_ffn, ffn_w_gate, ffn_w_up, ffn_w_down, moe_router, moe_w_gate, moe_w_up, moe_w_down,
           norm_final):
    B, T, D = x_prompt.shape
    S = x_sample.shape[0]
    assert x_sample.shape[1] == 1
    depth = w_in.shape[0]
    n_prompt = B * T
    n_real = n_prompt + S
    Mp = -(-n_real // ROW_PAD) * ROW_PAD
    n_tail = Mp - n_prompt
    d_ssm = ssm_d.shape[1]
    d_nsa = N_HEADS * HEAD_DIM
    G, P = ssm_a_re.shape[1], ssm_a_re.shape[2]
    R = G * P // LANES
    win_buf = cache_win.shape[2]
    o_kv = d_ssm + d_nsa
    o_gl = o_kv + 3 * KV_COLS
    o_g = o_gl + 3 * N_HEADS
    assert d_ssm % (Q_PER_KV * HEAD_DIM) == 0 and w_in.shape[2] == o_g + 2 * D

    tm_big = _pick_tile(Mp, 1376, SUBLANES_BF16)
    tm_mid = _pick_tile(Mp, 688, SUBLANES_BF16)
    tn = 512

    x_all = jnp.concatenate([x_prompt.reshape(n_prompt, D), x_sample.reshape(S, D),
                             jnp.zeros((Mp - n_real, D), F32)], axis=0)
    w_gates = w_in[:, :, o_g:]
    pad_tail = lambda a: jnp.pad(a.reshape(S, -1), ((0, n_tail - S), (0, 0)))

    outs = {k: [] for k in ("cmp_p", "sel_p", "win_p", "sre_p", "sim_p",
                            "cmp_s", "sel_s", "win_s", "sre_s", "sim_s")}
    for l in range(depth):
        h = _rmsnorm(x_all, norm_mix[l], BF16, "norm_mix")
        uq = _dense(h, w_in, (l,), 0, o_kv, tm=tm_big, tn=tn, name="in_uq")
        kvc, kvs, kvw = [_dense(h, w_in, (l,), o_kv + i * KV_COLS, KV_COLS, tm=tm_big, tn=KV_COLS,
                                name=f"in_kv{i}") for i in range(3)]
        glg = _dense(h, _gate_logit_weights(w_in[l][:, o_gl:o_g]), (), 0, N_KV_HEADS * LANES,
                     tm=tm_big, tn=N_KV_HEADS * LANES, name="in_gate_logits")
        gates = _dense(h, w_gates, (l,), 0, 2 * D, tm=tm_big, tn=tn,
                       epilogue=lambda p, e: _sigmoid(p[0]), name="in_merge_gates")

        s0 = [pad_tail(s[l]).reshape(n_tail, R, LANES) for s in (state_ssm_re, state_ssm_im)]
        ssm_params = (ssm_a_re, ssm_a_im, ssm_b_re, ssm_b_im, ssm_c_re, ssm_c_im, ssm_d, ssm_log_dt,
                      ssm_w_glu, ssm_b_glu)
        ssm_out, fin_p, fin_s = _s5_branch(uq, d_ssm, s0[0], s0[1], n_prompt, B, T, ssm_params, l, tm_mid)

        kd = CMP_STRIDE * HEAD_DIM
        pe = cmp_pe[l].reshape(2, 2, 1, kd)
        w1 = cmp_w1[l].reshape(2, 2, kd, HEAD_DIM)
        n_chunk = T // CMP_STRIDE
        rb = _pick_tile(n_prompt // CMP_STRIDE, 128, SUBLANES_F32)
        lt = _compress_lead_trail(
            [(kvc.reshape(Mp // CMP_STRIDE, CHUNK_COLS), (rb, CHUNK_COLS), lambda a, s: (s, 0))],
            (1, n_prompt // CMP_STRIDE // rb), pe, w1, rb, n_prompt // CMP_STRIDE, name="cmp_lt_prompt")
        cmp_kv = _compress_finish(lt, B, n_chunk, cmp_b1[l], cmp_w2[l], "cmp_finish_prompt")
        if n_chunk < LANES:
            cmp_kv = jnp.pad(cmp_kv, ((0, 0), (0, 0), (0, 0), (0, LANES - n_chunk), (0, 0)))
        q_blk0 = d_ssm // (Q_PER_KV * HEAD_DIM)
        nsa_out = _nsa_prompt(uq, q_blk0, glg, cmp_kv, kvs, kvw, n_seq=B, seq_len=T, out_rows=Mp)

        tail = lambda a, c0, c1: a[n_prompt:n_real, c0:c1]
        q_s = tail(uq, d_ssm, o_kv).reshape(S, N_KV_HEADS, Q_PER_KV, HEAD_DIM)
        gl_s =tail(glg, 0, N_KV_HEADS * LANES).reshape(S, N_KV_HEADS, LANES)[:, :, :3 * Q_PER_KV]
        gl_s = gl_s.reshape(S, N_KV_HEADS, 3, Q_PER_KV).transpose(0, 1, 3, 2)
        gate_s = jnp.pad(gl_s, ((0, 0), (0, 0), (0, 0), (0, LANES - 3)))
        new_sel = tail(kvs, 0, KV_COLS).reshape(S, 2 * N_KV_HEADS, 1, HEAD_DIM)
        new_win = tail(kvw, 0, KV_COLS).reshape(S, 2 * N_KV_HEADS, 1, HEAD_DIM)
        nsa_s = _nsa_sample(q_s, gate_s, new_sel, new_win, cache_cmp, cache_sel, cache_win, page_table, l,
                            pe, w1, cmp_b1[l], cmp_w2[l])
        nsa_out = lax.dynamic_update_slice(nsa_out, pad_tail(nsa_s).astype(BF16), (n_prompt, 0))

        ng = D // tn
        y = _fused_matmul(
            xs=[(ssm_out, lambda m, n, k, *pf: (m, 0)), (nsa_out, lambda m, n, k, *pf: (m, 0))],
            dots=[(0, w_br_ssm, (None, d_ssm, tn), lambda m, n, k, *pf: (l, 0, n)),
                  (1, w_br_nsa, (None, d_nsa, tn), lambda m, n, k, *pf: (l, 0, n))],
            extras=[(gates, (tm_mid, tn), lambda m, n, k, *pf: (m, n)),
                    (gates, (tm_mid, tn), lambda m, n, k, *pf: (m, ng + n))],
            epilogue=lambda p, e: e[0] * p[0] + e[1] * p[1],
            M=Mp, N=D, K=d_ssm, tm=tm_mid, tn=tn, tk=d_ssm, out_dtype=BF16, name="merge")
        x_all = _dense(y, w_out, (l,), 0, D, tm=tm_big, tn=tn,
                       extras=[(x_all, (tm_big, tn), lambda m, n, k, *pf: (m, n))],
                       epilogue=lambda p, e: e[0] + p[0], name="out_proj")

        kv5 = lambda a, rows, lead: a[rows].reshape(lead + (2, N_KV_HEADS, HEAD_DIM))
        outs["cmp_p"].append(kv5(kvc, slice(0, n_prompt), (B, T)))
        outs["sel_p"].append(kv5(kvs, slice(0, n_prompt), (B, T)))
        wp = min(WINDOW, T)
        outs["win_p"].append(kv5(kvw, slice(0, n_prompt), (B, T))[:, T - wp:])
        outs["sre_p"].append(fin_p[0].reshape(B, G, P))
        outs["sim_p"].append(fin_p[1].reshape(B, G, P))
        outs["cmp_s"].append(kv5(kvc, slice(n_prompt, n_real), (S, 1)))
        outs["sel_s"].append(kv5(kvs, slice(n_prompt, n_real), (S, 1)))
        win = jnp.concatenate([cache_win[l], kv5(kvw, slice(n_prompt, n_real), (S, 1))], axis=1)
        outs["win_s"].append(win[:, 1:])
        outs["sre_s"].append(fin_s[0][:S].reshape(S, G, P))
        outs["sim_s"].append(fin_s[1][:S].reshape(S, G, P))

        if l % 2 == 0:
            i = l // 2
            h2 = _rmsnorm(x_all, norm_ffn[l], BF16, "norm_ffn")
            f = ffn_w_gate.shape[2]
            tnf = _pick_tile(f, 256, LANES)
            wmap = lambda m, n, k, *pf: (i, 0, n)
            hmid = _fused_matmul(
                xs=[(h2, lambda m, n, k, *pf: (m, 0))],
                dots=[(0, ffn_w_gate, (None, D, tnf), wmap), (0, ffn_w_up, (None, D, tnf), wmap)],
                epilogue=lambda p, e: p[0] * _sigmoid(p[0]) * p[1],
                M=Mp, N=f, K=D, tm=tm_big, tn=tnf, tk=D, out_dtype=BF16, name="ffn_up")
            tk = _pick_tile(f, 1024, LANES)
            tnd = _pick_tile(D, 1024, LANES)
            x_all = _fused_matmul(
                xs=[(hmid, lambda m, n, k, *pf: (m, k))],
                dots=[(0, ffn_w_down, (None, tk, tnd), lambda m, n, k, *pf: (i, k, n))],
                extras=[(x_all, (tm_big, tnd), lambda m, n, k, *pf: (m, n))],
                epilogue=lambda p, e: e[0] + p[0],
                M=Mp, N=D, K=f, tm=tm_big, tn=tnd, tk=tk, out_dtype=F32, name="ffn_down")
        else:
            x_all = _moe(x_all, n_real, norm_ffn[l], moe_router, moe_w_gate, moe_w_up, moe_w_down, l // 2)

    y_all = _rmsnorm(x_all, norm_final, F32, "norm_final")
    st = lambda k: jnp.stack(outs[k])
    return (y_all[:n_prompt].reshape(B, T, D), y_all[n_prompt:n_real].reshape(S, 1, D),
            st("cmp_p"), st("sel_p"), st("win_p"), st("sre_p"), st("sim_p"),
            st("cmp_s"), st("sel_s"), st("win_s"), st("sre_s"), st("sim_s"))
```

```python
import functools
import math

import numpy as np
import jax
import jax.numpy as jnp
from jax import lax
from jax.experimental import pallas as pl
from jax.experimental.pallas import tpu as pltpu

F32 = jnp.float32
BF16 = jnp.bfloat16

SSM_GROUP = 16
N_HEADS = 16
HEAD_DIM = 128
N_KV_HEADS = 2
Q_PER_KV = N_HEADS // N_KV_HEADS
CMP_STRIDE = 16
CMP_BLOCK = 2 * CMP_STRIDE
SEL_BLOCK = 64
N_SEL = 16
N_LOCAL_SEL = 2
WINDOW = 512
Q_BLOCK = 128
TOP_K = 2
RMS_EPS = 1e-6
NEG_INF = -1e30
BIG = 1e30

LANES = 128
SUBLANES_F32 = 8
SUBLANES_BF16 = 16
VMEM_BYTES_V7X = 64 * 1024 * 1024
VMEM_LIMIT_CAP = VMEM_BYTES_V7X - 6 * 1024 * 1024
ROW_PAD = 64


def _vmem_limit(estimate_bytes):
    return int(min(max(estimate_bytes * 5 // 4 + (4 << 20), 16 << 20), VMEM_LIMIT_CAP))


def _pick_tile(total, target, mult):
    best = None
    for t in range(mult, min(total, target) + 1, mult):
        if total % t == 0:
            best = t
    assert best is not None, (total, target, mult)
    return best


def _nbytes(shape, dtype):
    n = 1
    for s in shape:
        if s is not None:
            n *= s
    return n * jnp.dtype(dtype).itemsize


def _gelu_tanh(x):
    return 0.5 * x * (1.0 + jnp.tanh(math.sqrt(2.0 / math.pi) * (x + 0.044715 * (x * x * x))))


def _sigmoid(x):
    return 1.0 / (1.0 + jnp.exp(-x))


def _fused_matmul(*, xs, dots, extras=(), epilogue, M, N, K, tm, tn, tk, out_dtype,
                  out_rows=None, out_map=None, order="mn", prefetch=(), x_bias=None,
                  n_used_tiles=False, name):
    assert M % tm == 0 and N % tn == 0 and K % tk == 0, (M, N, K, tm, tn, tk)
    nm, nn, nk = M // tm, N // tn, K // tk
    nd, nx, ne, npf = len(dots), len(xs), len(extras), len(prefetch)
    use_acc = nk > 1
    out_rows = M if out_rows is None else out_rows

    def wrap(f):
        if order == "mn":
            return lambda a, b, k, *pf: f(a, b, k, *pf)
        return lambda a, b, k, *pf: f(b, a, k, *pf)

    def body(*refs):
        pf = refs[:npf]
        x_refs = refs[npf:npf + nx]
        pos = npf + nx
        xb_ref = None
        if x_bias is not None:
            xb_ref = refs[pos]
            pos += 1
        w_refs = refs[pos:pos + nd]
        pos += nd
        e_refs = refs[pos:pos + ne]
        pos += ne
        o_ref = refs[pos]
        acc_refs = refs[pos + 1:]
        m_id = pl.program_id(0 if order == "mn" else 1)

        def compute():
            xv = []
            for xr in x_refs:
                v = xr[...]
                if xb_ref is not None:
                    v = v + xb_ref[...]
                xv.append(v.astype(BF16))
            parts = [jnp.dot(xv[d[0]], w_refs[i][...].astype(BF16), preferred_element_type=F32)
                     for i, d in enumerate(dots)]
            if not use_acc:
                o_ref[...] = epilogue(parts, [e[...] for e in e_refs]).astype(out_dtype)
                return
            k = pl.program_id(2)

            @pl.when(k == 0)
            def _():
                for i in range(nd):
                    acc_refs[i][...] = parts[i]

            @pl.when(k > 0)
            def _():
                for i in range(nd):
                    acc_refs[i][...] += parts[i]

            @pl.when(k == nk - 1)
            def _():
                o_ref[...] = epilogue([a[...] for a in acc_refs],
                                      [e[...] for e in e_refs]).astype(out_dtype)

        if n_used_tiles:
            pl.when(m_id < pf[-1][0])(compute)
        else:
            compute()

    in_specs = [pl.BlockSpec((tm, tk), wrap(f)) for _, f in xs]
    args = [a for a, _ in xs]
    est = sum(2 * _nbytes((tm, tk), a.dtype) for a, _ in xs)
    if x_bias is not None:
        in_specs.append(pl.BlockSpec((1, tk), wrap(lambda m, n, k, *pf: (0, k))))
        args.append(x_bias)
    for _, w, bs, f in dots:
        in_specs.append(pl.BlockSpec(bs, wrap(f)))
        args.append(w)
        est += 2 * _nbytes(bs, w.dtype) + _nbytes(bs, BF16)
    for a, bs, f in extras:
        in_specs.append(pl.BlockSpec(bs, wrap(f)))
        args.append(a)
        est += 2 * _nbytes(bs, a.dtype)
    if out_map is None:
        out_map = lambda m, n, k, *pf: (m, n)
    est += 2 * _nbytes((tm, tn), out_dtype) + (2 + nd) * _nbytes((tm, tn), F32)
    scratch = [pltpu.VMEM((tm, tn), F32) for _ in range(nd)] if use_acc else []
    grid = (nm, nn, nk) if order == "mn" else (nn, nm, nk)
    return pl.pallas_call(
        body,
        grid_spec=pltpu.PrefetchScalarGridSpec(
            num_scalar_prefetch=npf, grid=grid, in_specs=in_specs,
            out_specs=pl.BlockSpec((tm, tn), wrap(out_map)), scratch_shapes=scratch),
        out_shape=jax.ShapeDtypeStruct((out_rows, N), out_dtype),
        compiler_params=pltpu.CompilerParams(
            dimension_semantics=("parallel", "parallel", "arbitrary"),
            vmem_limit_bytes=_vmem_limit(est)),
        name=name,
    )(*prefetch, *args)


def _first(parts, extras):
    return parts[0]


def _dense(x, w, w_lead, col0, N, *, tm, tn, epilogue=_first, extras=(), out_dtype=F32, name):
    M, K = x.shape
    assert col0 % tn == 0
    c0 = col0 // tn
    nl = len(w_lead)
    return _fused_matmul(
        xs=[(x, lambda m, n, k, *pf: (m, 0))],
        dots=[(0, w, (None,) * nl + (K, tn), lambda m, n, k, *pf: tuple(w_lead) + (0, c0 + n))],
        extras=extras, epilogue=epilogue, M=M, N=N, K=K, tm=tm, tn=tn, tk=K,
        out_dtype=out_dtype, name=name)


def _rmsnorm(x, g, out_dtype, name):
    M, D = x.shape
    tr = _pick_tile(M, 256, SUBLANES_BF16)

    def body(x_ref, g_ref, o_ref):
        xf = x_ref[...]
        ms = jnp.mean(xf * xf, axis=-1, keepdims=True)
        o_ref[...] = (xf * lax.rsqrt(ms + RMS_EPS) * g_ref[...]).astype(out_dtype)

    return pl.pallas_call(
        body, grid=(M // tr,),
        in_specs=[pl.BlockSpec((tr, D), lambda i: (i, 0)), pl.BlockSpec((1, D), lambda i: (0, 0))],
        out_specs=pl.BlockSpec((tr, D), lambda i: (i, 0)),
        out_shape=jax.ShapeDtypeStruct((M, D), out_dtype),
        compiler_params=pltpu.CompilerParams(dimension_semantics=("parallel",)),
        name=name,
    )(x, g.reshape(1, D))


def _s5_discretize(a_re, a_im, log_dt_b, b_re_t, b_im_t):
    C, GP = b_re_t.shape

    def body(ar_ref, ai_ref, ld_ref, br_ref, bi_ref, lr_ref, li_ref, obr_ref, obi_ref):
        ar, ai = ar_ref[...], ai_ref[...]
        dt = jnp.exp(ld_ref[...])
        mag = jnp.exp(ar * dt)
        ang = ai * dt
        lr = mag * jnp.cos(ang)
        li = mag * jnp.sin(ang)
        pr, pi = lr, li
        for k in range(SUBLANES_F32):
            lr_ref[k:k + 1, :] = pr
            li_ref[k:k + 1, :] = pi
            pr, pi = pr * lr - pi * li, pr * li + pi * lr
        x, y = lr - 1.0, li
        den = ar * ar + ai * ai
        cr = (x * ar + y * ai) / den
        ci = (y * ar - x * ai) / den
        br, bi = br_ref[...], bi_ref[...]
        obr_ref[...] = cr * br - ci * bi
        obi_ref[...] = cr * bi + ci * br

    vec = jax.ShapeDtypeStruct((SUBLANES_F32, GP), F32)
    mat = jax.ShapeDtypeStruct((C, GP), F32)
    return pl.pallas_call(body, out_shape=(vec, vec, mat, mat), name="s5_discretize")(
        a_re, a_im, log_dt_b, b_re_t, b_im_t)


GROUP_TILE = 16
S5_IN = GROUP_TILE * SSM_GROUP


def _s5_prompt(uq, w_b, w_c, pw_re, pw_im, d2, *, n_seq, seq_len, out_rows):
    ngt, kx, ns = w_b.shape[1:]
    sub = SUBLANES_F32
    tc = _pick_tile(seq_len, 256, sub)
    nch = seq_len // tc
    ntile = tc // sub

    def body(u_ref, wb_ref, wc_ref, pr_ref, pi_ref, d_ref, z_ref, fr_ref, fi_ref, st, sr_ref, si_ref):
        c = pl.program_id(2)

        @pl.when(c == 0)
        def _():
            st[...] = jnp.zeros(st.shape, F32)

        u = u_ref[...]
        ub = u.astype(BF16)
        xr = jnp.dot(ub, wb_ref[0], preferred_element_type=F32).reshape(ntile, sub, ns)
        xi = jnp.dot(ub, wb_ref[1], preferred_element_type=F32).reshape(ntile, sub, ns)
        pr, pi = pr_ref[...], pi_ref[...]
        row = lax.broadcasted_iota(jnp.int32, (sub, ns), 0)
        d = 1
        while d < sub:
            mr = jnp.where(row >= d, pr[d - 1:d, :], 0.0)
            mi = jnp.where(row >= d, pi[d - 1:d, :], 0.0)
            rr = pltpu.roll(xr, d, axis=1)
            ri = pltpu.roll(xi, d, axis=1)
            xr, xi = xr + mr * rr - mi * ri, xi + mr * ri + mi * rr
            d *= 2
        sr_ref[...] = xr
        si_ref[...] = xi

        def tile(j, carry):
            cr, ci = carry
            nr = sr_ref[j] + pr * cr - pi * ci
            ni = si_ref[j] + pr * ci + pi * cr
            sr_ref[j] = nr
            si_ref[j] = ni
            return nr[sub - 1:sub, :], ni[sub - 1:sub, :]

        cr, ci = lax.fori_loop(0, ntile, tile, (st[0:1, :], st[1:2, :]))
        st[0:1, :] = cr
        st[1:2, :] = ci
        fr_ref[...] = cr
        fi_ref[...] = ci
        y = (jnp.dot(sr_ref[...].reshape(tc, ns).astype(BF16), wc_ref[0], preferred_element_type=F32)
             + jnp.dot(si_ref[...].reshape(tc, ns).astype(BF16), wc_ref[1], preferred_element_type=F32))
        z_ref[...] = _gelu_tanh(y + d_ref[...] * u)

    u_cols = ngt * kx
    fin = jax.ShapeDtypeStruct((n_seq, ngt, 1, ns), F32)
    return pl.pallas_call(
        body, grid=(n_seq, ngt, nch),
        in_specs=[pl.BlockSpec((tc, kx), lambda b, g, c: (b * nch + c, g)),
                  pl.BlockSpec((2, None, kx, ns), lambda b, g, c: (0, g, 0, 0)),
                  pl.BlockSpec((2, None, ns, kx), lambda b, g, c: (0, g, 0, 0)),
                  pl.BlockSpec((sub, ns), lambda b, g, c: (0, g)),
                  pl.BlockSpec((sub, ns), lambda b, g, c: (0, g)),
                  pl.BlockSpec((1, kx), lambda b, g, c: (0, g))],
        out_specs=[pl.BlockSpec((tc, kx), lambda b, g, c: (b * nch + c, g)),
                   pl.BlockSpec((None, None, 1, ns), lambda b, g, c: (b, g, 0, 0)),
                   pl.BlockSpec((None, None, 1, ns), lambda b, g, c: (b, g, 0, 0))],
        out_shape=(jax.ShapeDtypeStruct((out_rows, u_cols), F32), fin, fin),
        scratch_shapes=[pltpu.VMEM((sub, ns), F32), pltpu.VMEM((ntile, sub, ns), F32),
                        pltpu.VMEM((ntile, sub, ns), F32)],
        compiler_params=pltpu.CompilerParams(
            dimension_semantics=("parallel", "parallel", "arbitrary")),
        name="s5_prompt",
    )(uq, w_b, w_c, pw_re, pw_im, d2)


def _s5_tail(uq, w_b, w_c, pw_re, pw_im, d2, s0_re, s0_im, z_prev, *, row0):
    ngt, kx, ns = w_b.shape[1:]
    n_tail = s0_re.shape[0]
    assert row0 % n_tail == 0
    rb = row0 // n_tail

    def body(u_ref, wb_ref, wc_ref, pr_ref, pi_ref, d_ref, s0r_ref, s0i_ref, zp_ref, z_ref, fr_ref, fi_ref):
        u = u_ref[...]
        ub = u.astype(BF16)
        lr, li = pr_ref[0:1, :], pi_ref[0:1, :]
        s0r, s0i = s0r_ref[...], s0i_ref[...]
        sr = lr * s0r - li * s0i + jnp.dot(ub, wb_ref[0], preferred_element_type=F32)
        si = lr * s0i + li * s0r + jnp.dot(ub, wb_ref[1], preferred_element_type=F32)
        fr_ref[...] = sr
        fi_ref[...] = si
        y = (jnp.dot(sr.astype(BF16), wc_ref[0], preferred_element_type=F32)
             + jnp.dot(si.astype(BF16), wc_ref[1], preferred_element_type=F32))
        z_ref[...] = _gelu_tanh(y + d_ref[...] * u)

    st = jax.ShapeDtypeStruct(s0_re.shape, F32)
    return pl.pallas_call(
        body, grid=(ngt,),
        in_specs=[pl.BlockSpec((n_tail, kx), lambda g: (rb, g)),
                  pl.BlockSpec((2, None, kx, ns), lambda g: (0, g, 0, 0)),
                  pl.BlockSpec((2, None, ns, kx), lambda g: (0, g, 0, 0)),
                  pl.BlockSpec((SUBLANES_F32, ns), lambda g: (0, g)),
                  pl.BlockSpec((SUBLANES_F32, ns), lambda g: (0, g)),
                  pl.BlockSpec((1, kx), lambda g: (0, g)),
                  pl.BlockSpec((n_tail, ns), lambda g: (0, g)),
                  pl.BlockSpec((n_tail, ns), lambda g: (0, g)),
                  pl.BlockSpec(memory_space=pl.ANY)],
        out_specs=[pl.BlockSpec((n_tail, kx), lambda g: (rb, g)),
                   pl.BlockSpec((n_tail, ns), lambda g: (0, g)),
                   pl.BlockSpec((n_tail, ns), lambda g: (0, g))],
        out_shape=(jax.ShapeDtypeStruct(z_prev.shape, F32), st, st),
        input_output_aliases={8: 0},
        compiler_params=pltpu.CompilerParams(dimension_semantics=("parallel",)),
        name="s5_tail",
    )(uq, w_b, w_c, pw_re, pw_im, d2, s0_re, s0_im, z_prev)


def _block_diag_tiles(w):
    eye = jnp.eye(GROUP_TILE, dtype=w.dtype)
    nt, gt, a, b = w.shape
    return jnp.einsum("tgab,gh->tgahb", w, eye).reshape(nt, gt * a, gt * b)


def _s5_branch(uq, u_cols, s0_re, s0_im, n_prompt, n_seq, seq_len, params, layer, tm):
    (a_re, a_im, b_re, b_im, c_re, c_im, d, log_dt, w_glu, b_glu) = params
    Mp = uq.shape[0]
    G, P = a_re.shape[1], a_re.shape[2]
    C = SSM_GROUP
    GP = G * P
    ngt = G // GROUP_TILE

    lam_re, lam_im, bb_re, bb_im = _s5_discretize(
        a_re[layer].reshape(1, GP), a_im[layer].reshape(1, GP),
        jnp.broadcast_to(log_dt[layer][:, None], (G, P)).reshape(1, GP),
        b_re[layer].transpose(2, 0, 1).reshape(C, GP), b_im[layer].transpose(2, 0, 1).reshape(C, GP))

    def b_tiles(bb):
        t = bb.reshape(C, ngt, GROUP_TILE, P).transpose(1, 2, 0, 3)
        return _block_diag_tiles(t)

    w_b =jnp.stack([b_tiles(bb_re), b_tiles(bb_im)]).astype(BF16)

    def c_tiles(cc):
        t = cc.reshape(ngt, GROUP_TILE, C, P).transpose(0, 1, 3, 2)
        return _block_diag_tiles(t)

    w_c = jnp.stack([c_tiles(c_re[layer]), -c_tiles(c_im[layer])]).astype(BF16)

    d2 = d[layer].reshape(1, u_cols)
    z, fp_re, fp_im = _s5_prompt(uq, w_b, w_c, lam_re, lam_im, d2, n_seq=n_seq, seq_len=seq_len,
                                 out_rows=Mp)
    z, fs_re, fs_im = _s5_tail(uq, w_b, w_c, lam_re, lam_im, d2, s0_re, s0_im, z, row0=n_prompt)

    tn = _pick_tile(u_cols, 512, LANES)
    ssm_out = _dense(z, w_glu, (layer,), 0, u_cols, tm=tm, tn=tn,
                     extras=[(z, (tm, tn), lambda m, n, k, *pf: (m, n)),
                             (b_glu[layer].reshape(1, u_cols), (1, tn), lambda m, n, k, *pf: (0, n))],
                     epilogue=lambda p, e: e[0] * _sigmoid(p[0] + e[1]),
                     out_dtype=BF16, name="s5_glu")
    return ssm_out, (fp_re, fp_im), (fs_re, fs_im)


KV_COLS = 2 * N_KV_HEADS * HEAD_DIM
CHUNK_COLS = CMP_STRIDE * KV_COLS


def _compress_lead_trail(srcs, grid, pe, w1, rows_per_src, n_rows, prefetch=(), name="cmp_lt"):
    ns = len(srcs)
    npf = len(prefetch)
    rows = ns * rows_per_src
    kd = CMP_STRIDE * HEAD_DIM

    def body(*refs):
        x_refs = refs[npf:npf + ns]
        pe_ref, w_ref, o_ref = refs[npf + ns:]
        for kv in range(2):
            per_head = []
            for h in range(N_KV_HEADS):
                c0 = (kv * N_KV_HEADS + h) * HEAD_DIM
                per_src = [jnp.concatenate([xr[:, j * KV_COLS + c0: j * KV_COLS + c0 + HEAD_DIM]
                                            for j in range(CMP_STRIDE)], axis=1) for xr in x_refs]
                per_head.append(per_src[0] if ns == 1 else jnp.concatenate(per_src, axis=0))
            xs = jnp.concatenate(per_head, axis=0)
            for lt in range(2):
                xv = (xs + pe_ref[kv, lt]).astype(BF16)
                r = jnp.dot(xv, w_ref[kv, lt].astype(BF16), preferred_element_type=F32)
                for h in range(N_KV_HEADS):
                    o_ref[kv, lt, h] = r[h * rows:(h + 1) * rows]

    in_specs = [pl.BlockSpec(bs, f) for _, bs, f in srcs]
    in_specs += [pl.BlockSpec((2, 2, 1, kd), lambda *a: (0, 0, 0, 0)),
                 pl.BlockSpec((2, 2, kd, HEAD_DIM), lambda *a: (0, 0, 0, 0))]
    step_of = lambda *a: a[0] * grid[1] + a[1]
    return pl.pallas_call(
        body,
        grid_spec=pltpu.PrefetchScalarGridSpec(
            num_scalar_prefetch=npf, grid=grid, in_specs=in_specs,
            out_specs=pl.BlockSpec((2, 2, N_KV_HEADS, rows, HEAD_DIM),
                                   lambda *a: (0, 0, 0, step_of(*a), 0))),
        out_shape=jax.ShapeDtypeStruct((2, 2, N_KV_HEADS, n_rows, HEAD_DIM), F32),
        compiler_params=pltpu.CompilerParams(
            dimension_semantics=("parallel", "arbitrary"), vmem_limit_bytes=48 << 20),
        name=name,
    )(*prefetch, *[a for a, _, _ in srcs], pe, w1)


def _compress_finish(lt, n_seq, n_chunk, b1, w2, name):
    lead = lt[:, 0]
    trail = lt[:, 1].reshape(2, N_KV_HEADS, n_seq, n_chunk, HEAD_DIM)
    trail = jnp.roll(trail, -1, axis=3).reshape(lead.shape)
    rows = N_KV_HEADS * n_seq * n_chunk
    lead = lead.reshape(2, rows, HEAD_DIM)
    trail = trail.reshape(2, rows, HEAD_DIM)
    rt = _pick_tile(rows, 1024, SUBLANES_F32)

    def body(a_ref, t_ref, b_ref, w_ref, o_ref):
        hid = _gelu_tanh(a_ref[...] + t_ref[...] + b_ref[...])
        o_ref[...] = jnp.dot(hid.astype(BF16), w_ref[...].astype(BF16), preferred_element_type=F32)

    out = pl.pallas_call(
        body, grid=(2, rows // rt),
        in_specs=[pl.BlockSpec((None, rt, HEAD_DIM), lambda kv, r: (kv, r, 0)),
                  pl.BlockSpec((None, rt, HEAD_DIM), lambda kv, r: (kv, r, 0)),
                  pl.BlockSpec((None, 1, HEAD_DIM), lambda kv, r: (kv, 0, 0)),
                  pl.BlockSpec((None, HEAD_DIM, HEAD_DIM), lambda kv, r: (kv, 0, 0))],
        out_specs=pl.BlockSpec((None, rt, HEAD_DIM), lambda kv, r: (kv, r, 0)),
        out_shape=jax.ShapeDtypeStruct((2, rows, HEAD_DIM), F32),
        compiler_params=pltpu.CompilerParams(dimension_semantics=("parallel", "parallel")),
        name=name,
    )(lead, trail, b1.reshape(2, 1, HEAD_DIM), w2)
    return out.reshape(2, N_KV_HEADS, n_seq, n_chunk, HEAD_DIM)


def _overlap_matrix(n_cmp, n_blk, rows, cols):
    start = np.arange(rows)[:, None] * CMP_STRIDE
    js = np.arange(cols)[None, :] * SEL_BLOCK
    ov = (start < js + SEL_BLOCK) & (start + CMP_BLOCK > js)
    ov &= (np.arange(rows)[:, None] < n_cmp) & (np.arange(cols)[None, :] < n_blk)
    return jnp.asarray(ov.astype(np.float32))


def _masked_softmax(s, valid):
    s = jnp.where(valid, s, NEG_INF)
    p = jnp.exp(s - jnp.max(s, axis=-1, keepdims=True)) * jnp.where(valid, 1.0, 0.0)
    return p / jnp.maximum(jnp.sum(p, axis=-1, keepdims=True), 1e-30)


_NT = (((1,), (1,)), ((), ()))


def _nsa_prompt(uq, q_blk0, glg, cmp_kv, kvs, kvw, *, n_seq, seq_len, out_rows):
    T = seq_len
    QB = Q_BLOCK
    HG = Q_PER_KV
    HD = HEAD_DIM
    nqb = T // QB
    ncp = cmp_kv.shape[3]
    n_cmp = T // CMP_STRIDE - 1
    n_blk = -(-T // SEL_BLOCK)
    n_pick = min(N_SEL, n_blk)
    assert T % QB == 0 and T % SEL_BLOCK == 0 and n_blk <= LANES and ncp <= LANES
    CH = min(512, T)
    assert T % CH == 0
    WB = min(WINDOW + QB, T)
    scale = HD ** -0.5
    blk_of_key = np.arange(T) // SEL_BLOCK
    expand = (np.arange(LANES)[:, None] == blk_of_key[None, :]).astype(np.float32)
    expand = jnp.asarray(expand.reshape(LANES, T // CH, CH).transpose(1, 0, 2), dtype=BF16)
    ov = _overlap_matrix(n_cmp, n_blk, ncp, LANES)

    def body(q_ref, gl_ref, kc_ref, vc_ref, ks_ref, vs_ref, kw_ref, vw_ref, e_ref, ov_ref, o_ref,
             m_ref, l_ref, acc_ref):
        i = pl.program_id(2)
        start = i * QB
        q = q_ref[...]
        q_st = jnp.concatenate([q[:, g * HD:(g + 1) * HD] for g in range(HG)], axis=0).astype(BF16)
        qpos = start + lax.broadcasted_iota(jnp.int32, (QB, 1), 0)

        s = lax.dot_general(q_st, kc_ref[...].astype(BF16), _NT, preferred_element_type=F32) * scale
        n_idx = lax.broadcasted_iota(jnp.int32, (QB, ncp), 1)
        cvalid = (n_idx * CMP_STRIDE + (CMP_BLOCK - 1) <= qpos) & (n_idx < n_cmp)
        p_c = _masked_softmax(s.reshape(HG, QB, ncp), cvalid)
        o_c = jnp.dot(p_c.reshape(HG * QB, ncp).astype(BF16), vc_ref[...].astype(BF16),
                      preferred_element_type=F32).reshape(HG, QB, HD)

        imp = jnp.dot(jnp.sum(p_c, axis=0), ov_ref[...], preferred_element_type=F32,
                      precision=lax.Precision.HIGHEST)
        blk = lax.broadcasted_iota(jnp.int32, (QB, LANES), 1)
        cur = qpos // SEL_BLOCK
        forced = (blk == 0) | ((blk <= cur) & (blk > cur - N_LOCAL_SEL))
        score = jnp.where(blk > cur, -BIG, jnp.where(forced, BIG, imp))
        rank = jnp.zeros((QB, LANES), F32)
        for j2 in range(n_blk):
            col = score[:, j2:j2 + 1]
            ahead = (col > score) | ((col == score) & (blk > j2))
            rank = rank + jnp.where(ahead, 1.0, 0.0)
        sel = jnp.where((rank < n_pick) & (blk < n_blk), 1.0, 0.0).astype(BF16)

        m_ref[...] = jnp.full(m_ref.shape, NEG_INF, F32)
        l_ref[...] = jnp.zeros(l_ref.shape, F32)
        acc_ref[...] = jnp.zeros(acc_ref.shape, F32)

        def chunk(c, carry):
            k0 = pl.multiple_of(c * CH, CH)
            kch = ks_ref[pl.ds(k0, CH), :].astype(BF16)
            vch = vs_ref[pl.ds(k0, CH), :].astype(BF16)
            sc = lax.dot_general(q_st, kch, _NT, preferred_element_type=F32) * scale
            picked = jnp.dot(sel, e_ref[c], preferred_element_type=F32)
            kpos = k0 + lax.broadcasted_iota(jnp.int32, (QB, CH), 1)
            valid = (picked > 0.5) & (kpos <= qpos)
            s3 = jnp.where(valid, sc.reshape(HG, QB, CH), NEG_INF)
            m_old = m_ref[...]
            m_new = jnp.maximum(m_old, jnp.max(s3, axis=-1, keepdims=True))
            p = jnp.exp(s3 - m_new) * jnp.where(valid, 1.0, 0.0)
            alpha = jnp.exp(m_old - m_new)
            l_ref[...] = alpha * l_ref[...] + jnp.sum(p, axis=-1, keepdims=True)
            pv = jnp.dot(p.reshape(HG * QB, CH).astype(BF16), vch, preferred_element_type=F32)
            acc_ref[...] = alpha * acc_ref[...] + pv.reshape(HG, QB, HD)
            m_ref[...] = m_new
            return carry

        lax.fori_loop(0, (start + QB + CH - 1) // CH, chunk, 0)
        o_s = acc_ref[...] / jnp.maximum(l_ref[...], 1e-30)

        w0 = pl.multiple_of(jnp.minimum(jnp.maximum(start - WINDOW, 0), T - WB), QB)
        sw = lax.dot_general(q_st, kw_ref[pl.ds(w0, WB), :].astype(BF16), _NT,
                             preferred_element_type=F32) * scale
        dist = qpos - (w0 + lax.broadcasted_iota(jnp.int32, (QB, WB), 1))
        p_w = _masked_softmax(sw.reshape(HG, QB, WB), (dist >= 0) & (dist < WINDOW))
        o_w = jnp.dot(p_w.reshape(HG * QB, WB).astype(BF16), vw_ref[pl.ds(w0, WB), :].astype(BF16),
                      preferred_element_type=F32).reshape(HG, QB, HD)

        gate = _sigmoid(gl_ref[...])
        for g in range(HG):
            og = (gate[:, g:g + 1] * o_c[g] + gate[:, HG + g:HG + g + 1] * o_s[g]
                  + gate[:, 2 * HG + g:2 * HG + g + 1] * o_w[g])
            o_ref[:, g * HD:(g + 1) * HD] = og.astype(o_ref.dtype)

    row = lambda b, h, i: b * nqb + i
    kv_spec = lambda kv: pl.BlockSpec((T, HD), lambda b, h, i: (b, kv * N_KV_HEADS + h))
    return pl.pallas_call(
        body, grid=(n_seq, N_KV_HEADS, nqb),
        in_specs=[pl.BlockSpec((QB, HG * HD), lambda b, h, i: (row(b, h, i), q_blk0 + h)),
                  pl.BlockSpec((QB, LANES), lambda b, h, i: (row(b, h, i), h)),
                  pl.BlockSpec((None, None, None, ncp, HD), lambda b, h, i: (0, h, b, 0, 0)),
                  pl.BlockSpec((None, None, None, ncp, HD), lambda b, h, i: (1, h, b, 0, 0)),
                  kv_spec(0), kv_spec(1), kv_spec(0), kv_spec(1),
                  pl.BlockSpec((T // CH, LANES, CH), lambda b, h, i: (0, 0, 0)),
                  pl.BlockSpec((ncp, LANES), lambda b, h, i: (0, 0))],
        out_specs=pl.BlockSpec((QB, HG * HD), lambda b, h, i: (row(b, h, i), h)),
        out_shape=jax.ShapeDtypeStruct((out_rows, N_HEADS * HD), BF16),
        scratch_shapes=[pltpu.VMEM((HG, QB, 1), F32), pltpu.VMEM((HG, QB, 1), F32),
                        pltpu.VMEM((HG, QB, HD), F32)],
        compiler_params=pltpu.CompilerParams(
            dimension_semantics=("parallel", "parallel", "arbitrary"), vmem_limit_bytes=48 << 20),
        name="nsa_prompt",
    )(uq, glg, cmp_kv, cmp_kv, kvs, kvs, kvw, kvw, expand, ov)


def _nsa_sample_select(q_s, cmp_kv, past_len):
    S = q_s.shape[0]
    HG, HD = Q_PER_KV, HEAD_DIM
    nc = cmp_kv.shape[3]
    n_cmp = nc - 1
    q_pos = past_len
    n_blk = -(-(past_len + 1) // SEL_BLOCK)
    nbp = -(-n_blk // LANES) * LANES
    n_pick = min(N_SEL, n_blk)
    cur = q_pos // SEL_BLOCK
    scale = HD ** -0.5
    ov = _overlap_matrix(n_cmp, n_blk, nc, nbp)

    def body(q_ref, kc_ref, vc_ref, ov_ref, oc_ref, idx_ref):
        q = q_ref[...].astype(BF16)
        s = lax.dot_general(q, kc_ref[...].astype(BF16), _NT, preferred_element_type=F32) * scale
        n_idx = lax.broadcasted_iota(jnp.int32, (HG, nc), 1)
        cvalid = (n_idx * CMP_STRIDE + (CMP_BLOCK - 1) <= q_pos) & (n_idx < n_cmp)
        p_c = _masked_softmax(s, cvalid)
        oc_ref[...] = jnp.dot(p_c.astype(BF16), vc_ref[...].astype(BF16), preferred_element_type=F32)
        imp = jnp.sum(jnp.dot(p_c, ov_ref[...], preferred_element_type=F32,
                              precision=lax.Precision.HIGHEST), axis=0, keepdims=True)
        blk = lax.broadcasted_iota(jnp.int32, (1, nbp), 1)
        forced = (blk == 0) | ((blk <= cur) & (blk > cur - N_LOCAL_SEL))
        score = jnp.where(blk > cur, -BIG, jnp.where(forced, BIG, imp))
        col = jnp.broadcast_to(score, (LANES, nbp)).T[:, 0:1]
        jp = lax.broadcasted_iota(jnp.int32, (nbp, nbp), 0)
        jj = lax.broadcasted_iota(jnp.int32, (nbp, nbp), 1)
        ahead = ((col > score) | ((col == score) & (jp < jj))) & (jp < n_blk)
        rank = jnp.sum(jnp.where(ahead, 1.0, 0.0), axis=0, keepdims=True)
        lane = lax.broadcasted_iota(jnp.int32, (1, LANES), 1)
        out = jnp.zeros((1, LANES), jnp.int32)
        for r in range(n_pick):
            hit = (rank == float(r)) & (blk < n_blk)
            idx_r = jnp.sum(jnp.where(hit, blk, 0), axis=1, keepdims=True)
            out = jnp.where(lane == r, idx_r, out)
        idx_ref[...] = jnp.broadcast_to(out, (SUBLANES_F32, LANES))

    o_c, idx = pl.pallas_call(
        body, grid=(S, N_KV_HEADS),
        in_specs=[pl.BlockSpec((None, None, HG, HD), lambda b, h: (b, h, 0, 0)),
                  pl.BlockSpec((None, None, None, nc, HD), lambda b, h: (0, h, b, 0, 0)),
                  pl.BlockSpec((None, None, None, nc, HD), lambda b, h: (1, h, b, 0, 0)),
                  pl.BlockSpec((nc, nbp), lambda b, h: (0, 0))],
        out_specs=[pl.BlockSpec((None, None, HG, HD), lambda b, h: (b, h, 0, 0)),
                   pl.BlockSpec((None, None, SUBLANES_F32, LANES), lambda b, h: (b, h, 0, 0))],
        out_shape=(jax.ShapeDtypeStruct((S, N_KV_HEADS, HG, HD), F32),
                   jax.ShapeDtypeStruct((S, N_KV_HEADS, SUBLANES_F32, LANES), jnp.int32)),
        compiler_params=pltpu.CompilerParams(dimension_semantics=("parallel", "parallel")),
        name="nsa_sample_select",
    )(q_s, cmp_kv, cmp_kv, ov)
    return o_c, idx[:, :, 0, :n_pick], n_pick


def _nsa_sample_attend(q_s, o_c, gate_s, src, live, cache_sel_v, layer, new_sel, cache_win_v, new_win,
                       n_pick):
    S = q_s.shape[0]
    HG, HD = Q_PER_KV, HEAD_DIM
    wb = cache_win_v.shape[2]
    scale = HD ** -0.5
    nkeys = n_pick * SEL_BLOCK

    def body(src_ref, live_ref, q_ref, oc_ref, g_ref, *rest):
        k_refs = rest[:n_pick]
        v_refs = rest[n_pick:2 * n_pick]
        kn_ref, vn_ref, kw_ref, vw_ref, kwn_ref, vwn_ref, o_ref = rest[2 * n_pick:]
        b, h = pl.program_id(0), pl.program_id(1)
        base = (b * N_KV_HEADS + h) * n_pick
        q = q_ref[...]
        qb = q.astype(BF16)

        def with_new_key(s, valid, s_new, v_old, v_new):
            sm = jnp.where(valid, s, NEG_INF)
            m = jnp.maximum(jnp.max(sm, axis=-1, keepdims=True), s_new)
            p = jnp.exp(sm - m) * jnp.where(valid, 1.0, 0.0)
            p_new = jnp.exp(s_new - m)
            den = jnp.sum(p, axis=-1, keepdims=True) + p_new
            return (jnp.dot(p.astype(BF16), v_old, preferred_element_type=F32) + p_new * v_new) / den

        keys = jnp.concatenate([r[...] for r in k_refs], axis=0).astype(BF16)
        vals = jnp.concatenate([r[...] for r in v_refs], axis=0).astype(BF16)
        s = lax.dot_general(qb, keys, _NT, preferred_element_type=F32) * scale
        lane = lax.broadcasted_iota(jnp.int32, (1, nkeys), 1)
        livef = jnp.zeros((1, nkeys), F32)
        for r in range(n_pick):
            livef = jnp.where((lane >= r * SEL_BLOCK) & (lane < (r + 1) * SEL_BLOCK),
                              live_ref[base + r].astype(F32), livef)
        s_new = jnp.sum(q * kn_ref[...], axis=-1, keepdims=True) * scale
        o_s = with_new_key(s, livef > 0.5, s_new, vals, vn_ref[...])

        sw = lax.dot_general(qb, kw_ref[...].astype(BF16), _NT, preferred_element_type=F32) * scale
        wrow = lax.broadcasted_iota(jnp.int32, (1, wb), 1)
        sw_new = jnp.sum(q * kwn_ref[...], axis=-1, keepdims=True) * scale
        o_w = with_new_key(sw, wrow > wb - WINDOW, sw_new, vw_ref[...].astype(BF16), vwn_ref[...])

        gate = _sigmoid(g_ref[...])
        o_ref[...] = gate[:, 0:1] * oc_ref[...] + gate[:, 1:2] * o_s + gate[:, 2:3] * o_w

    def blk_spec(r, kv):
        return pl.BlockSpec((None, None, SEL_BLOCK, HD),
                            lambda b, h, src_ref, live_ref: (layer, src_ref[(b * N_KV_HEADS + h) * n_pick + r],
                                                             0, kv * N_KV_HEADS + h))

    qspec = pl.BlockSpec((None, None, HG, HD), lambda b, h, *_: (b, h, 0, 0))
    gspec = pl.BlockSpec((None, None, HG, LANES), lambda b, h, *_: (b, h, 0, 0))
    new_spec = lambda kv: pl.BlockSpec((None, None, 1, HD), lambda b, h, *_: (b, kv * N_KV_HEADS + h, 0, 0))
    win_spec = lambda kv: pl.BlockSpec((None, None, wb, HD), lambda b, h, *_: (layer, b, 0, kv * N_KV_HEADS + h))
    in_specs = ([qspec, qspec, gspec] + [blk_spec(r, 0) for r in range(n_pick)]
                + [blk_spec(r, 1) for r in range(n_pick)]
                + [new_spec(0), new_spec(1), win_spec(0), win_spec(1), new_spec(0), new_spec(1)])
    return pl.pallas_call(
        body,
        grid_spec=pltpu.PrefetchScalarGridSpec(
            num_scalar_prefetch=2, grid=(S, N_KV_HEADS), in_specs=in_specs,
            out_specs=pl.BlockSpec((None, None, HG, HD), lambda b, h, *_: (b, h, 0, 0))),
        out_shape=jax.ShapeDtypeStruct((S, N_KV_HEADS, HG, HD), F32),
        compiler_params=pltpu.CompilerParams(dimension_semantics=("parallel", "parallel")),
        name="nsa_sample_attend",
    )(src, live, q_s, o_c, gate_s, *([cache_sel_v] * (2 * n_pick)), new_sel, new_sel,
      cache_win_v, cache_win_v, new_win, new_win)


def _nsa_sample(q_s, gate_s, new_sel, new_win, cache_cmp, cache_sel, cache_win, page_table, layer,
                pe, w1, b1, w2):
    S, n_pages = page_table.shape
    depth, n_pool, page = cache_cmp.shape[:3]
    past_len = n_pages * page
    assert page % CMP_STRIDE == 0 and page % SEL_BLOCK == 0
    cpp = page // CMP_STRIDE
    bpp = page // SEL_BLOCK
    pps = _pick_tile(n_pages, 16, 1)
    cache_v = cache_cmp.reshape(depth, n_pool, cpp, CHUNK_COLS)
    pt_flat = page_table.reshape(-1)
    srcs = [(cache_v, (None, None, cpp, CHUNK_COLS),
             (lambda b, s, pt, p=p: (layer, pt[b * n_pages + s * pps + p], 0, 0))) for p in range(pps)]
    nc = n_pages * cpp
    lt = _compress_lead_trail(srcs, (S, n_pages // pps), pe, w1, cpp, S * nc, prefetch=(pt_flat,),
                              name="cmp_lt_sample")
    cmp_kv = _compress_finish(lt, S, nc, b1, w2, "cmp_finish_sample")
    o_c, idx, n_pick = _nsa_sample_select(q_s, cmp_kv, past_len)

    n_past_blk = past_len // SEL_BLOCK
    idc = jnp.minimum(idx, n_past_blk - 1)
    pages = jnp.take_along_axis(page_table[:, None, :], idc // bpp, axis=2)
    src = (pages * bpp + idc % bpp).astype(jnp.int32).reshape(-1)
    live = (idx < n_past_blk).astype(jnp.int32).reshape(-1)
    wb = cache_win.shape[2]
    out = _nsa_sample_attend(
        q_s, o_c, gate_s, src, live, cache_sel.reshape(depth, n_pool * bpp, SEL_BLOCK, KV_COLS), layer,
        new_sel, cache_win.reshape(depth, S, wb, KV_COLS), new_win, n_pick)
    return out.reshape(S, N_HEADS * HEAD_DIM)


MOE_ROW_TILE = 512


def _moe_route(x, g, w_router):
    M, D = x.shape
    E = w_router.shape[1]
    tr = _pick_tile(M, 256, SUBLANES_BF16)
    wr = jnp.pad(w_router, ((0, 0), (0, LANES - E)))

    def body(x_ref, g_ref, w_ref, h_ref, r_ref):
        xf = x_ref[...]
        ms = jnp.mean(xf * xf, axis=-1, keepdims=True)
        hf = xf * lax.rsqrt(ms + RMS_EPS) * g_ref[...]
        h_ref[...] = hf.astype(BF16)
        logits = jnp.dot(hf, w_ref[...], preferred_element_type=F32, precision=lax.Precision.HIGHEST)
        lane = lax.broadcasted_iota(jnp.int32, (tr, LANES), 1)
        logits = jnp.where(lane < E, logits, -jnp.inf)
        v1 = jnp.max(logits, axis=-1, keepdims=True)
        i1 = jnp.min(jnp.where(logits == v1, lane, LANES), axis=-1, keepdims=True)
        rest = jnp.where(lane == i1, -jnp.inf, logits)
        v2 = jnp.max(rest, axis=-1, keepdims=True)
        i2 = jnp.min(jnp.where(rest == v2, lane, LANES), axis=-1, keepdims=True)
        e2 = jnp.exp(v2 - v1)
        g1 = 1.0 / (1.0 + e2)
        out = jnp.where(lane == 0, i1.astype(F32), jnp.where(lane == 1, i2.astype(F32),
                        jnp.where(lane == 2, g1, jnp.where(lane == 3, e2 * g1, 0.0))))
        r_ref[...] = out

    return pl.pallas_call(
        body, grid=(M // tr,),
        in_specs=[pl.BlockSpec((tr, D), lambda i: (i, 0)), pl.BlockSpec((1, D), lambda i: (0, 0)),
                  pl.BlockSpec((D, LANES), lambda i: (0, 0))],
        out_specs=[pl.BlockSpec((tr, D), lambda i: (i, 0)), pl.BlockSpec((tr, LANES), lambda i: (i, 0))],
        out_shape=(jax.ShapeDtypeStruct((M, D), BF16), jax.ShapeDtypeStruct((M, LANES), F32)),
        compiler_params=pltpu.CompilerParams(dimension_semantics=("parallel",)),
        name="moe_route",
    )(x, g.reshape(1, D), wr)


MOE_GROUP_TILES = 5


def _moe_down(hmid, wd, li, gate_b, tile_expert, n_used, tm):
    rows, F = hmid.shape
    D = wd.shape[-1]
    mt = MOE_GROUP_TILES
    n_tiles = rows // tm
    tk = _pick_tile(F, 2048, LANES)
    tn = _pick_tile(D, 1024, LANES)
    nk = F // tk
    n_slots = n_tiles * nk

    t = jnp.arange(n_tiles, dtype=jnp.int32)
    used = t < n_used
    same = (tile_expert[:, None] == tile_expert[None, :]) & used[None, :]
    first = jnp.min(jnp.where(same, t[None, :], n_tiles), axis=1)
    cnt = jnp.sum(same.astype(jnp.int32), axis=1)
    j = t - first
    m_local = jnp.where(used, j % mt, 0)
    g_first = t - m_local
    g_cnt = jnp.where(used, jnp.minimum(mt, cnt - (j - m_local)), 1)
    kk = jnp.arange(nk, dtype=jnp.int32)
    pos = (nk * g_first[:, None] + kk[None, :] * g_cnt[:, None] + m_local[:, None]).reshape(-1)
    n_items = (nk * n_used).astype(jnp.int32)
    order = jnp.minimum(jnp.arange(n_slots, dtype=jnp.int32), n_items - 1)
    table = lambda v: jnp.zeros((n_slots,), jnp.int32).at[pos].set(v.reshape(-1))[order]
    it_tile = table(jnp.broadcast_to(t[:, None], (n_tiles, nk)))
    it_k = table(jnp.broadcast_to(kk[None, :], (n_tiles, nk)))
    it_slot = table(jnp.broadcast_to(m_local[:, None], (n_tiles, nk)))
    it_exp = tile_expert[it_tile]
    it_out = jnp.where(it_k == nk - 1, it_tile, it_tile - it_slot)

    def body(tile_ref, k_ref, slot_ref, exp_ref, out_ref, n_ref, h_ref, w_ref, g_ref, o_ref, acc_ref):
        i = pl.program_id(1)

        @pl.when(i < n_ref[0])
        def _():
            k = k_ref[i]
            slot = slot_ref[i]
            p = jnp.dot(h_ref[...], w_ref[...].astype(BF16), preferred_element_type=F32)

            @pl.when(k == 0)
            def _():
                acc_ref[slot] = p

            @pl.when(k > 0)
            def _():
                acc_ref[slot] += p

            @pl.when(k == nk - 1)
            def _():
                o_ref[...] = acc_ref[slot] * g_ref[...][:, 0:1]

    est = (2 * _nbytes((tm, tk), BF16) + 2 * _nbytes((tk, tn), F32) + _nbytes((tk, tn), BF16)
           + (mt + 3) * _nbytes((tm, tn), F32))
    return pl.pallas_call(
        body,
        grid_spec=pltpu.PrefetchScalarGridSpec(
            num_scalar_prefetch=6, grid=(D // tn, n_slots),
            in_specs=[pl.BlockSpec((tm, tk), lambda n, i, tr, kr, sr, er, orr, nr: (tr[i], kr[i])),
                      pl.BlockSpec((None, None, tk, tn),
                                   lambda n, i, tr, kr, sr, er, orr, nr: (li, er[i], kr[i], n)),
                      pl.BlockSpec((tm, LANES), lambda n, i, tr, kr, sr, er, orr, nr: (tr[i], 0))],
            out_specs=pl.BlockSpec((tm, tn), lambda n, i, tr, kr, sr, er, orr, nr: (orr[i], n)),
            scratch_shapes=[pltpu.VMEM((mt, tm, tn), F32)]),
        out_shape=jax.ShapeDtypeStruct((rows, D), F32),
        compiler_params=pltpu.CompilerParams(
            dimension_semantics=("parallel", "arbitrary"), vmem_limit_bytes=_vmem_limit(est)),
        name="moe_down",
    )(it_tile, it_k, it_slot, it_exp, it_out, n_items.reshape(1), hmid, wd, gate_b)


def _moe(x_all, n_real, g, w_router, wg, wu, wd, li):
    Mp, D = x_all.shape
    E, _, F = wg.shape[1:]
    tm = MOE_ROW_TILE
    h, route = _moe_route(x_all, g, w_router[li])

    r = route[:n_real]
    e_flat = r[:, :TOP_K].astype(jnp.int32).reshape(-1)
    gates = r[:, TOP_K:2 * TOP_K].reshape(-1)
    n_pairs = n_real * TOP_K
    onehot = (e_flat[:, None] == jnp.arange(E)[None, :]).astype(jnp.int32)
    counts = jnp.sum(onehot, axis=0)
    within = jnp.take_along_axis(jnp.cumsum(onehot, axis=0) - onehot, e_flat[:, None], axis=1)[:, 0]
    padded = (counts + tm - 1) // tm * tm
    ends = jnp.cumsum(padded)
    dest = (ends - padded)[e_flat] + within
    n_tiles = -(-n_pairs // tm) + E
    rows = n_tiles * tm
    src_tok = jnp.zeros((rows,), jnp.int32).at[dest].set(jnp.arange(n_pairs, dtype=jnp.int32) // TOP_K)
    row_gate = jnp.zeros((rows,), F32).at[dest].set(gates)
    n_used = (ends[-1] // tm).astype(jnp.int32)
    tile_ids = jnp.minimum(jnp.arange(n_tiles, dtype=jnp.int32), n_used - 1)
    tile_expert = jnp.minimum(jnp.searchsorted(ends, tile_ids * tm, side="right"), E - 1).astype(jnp.int32)
    pf = (tile_expert, n_used.reshape(1))
    xs = jnp.take(h, src_tok, axis=0)

    tn = _pick_tile(F, 512, LANES)
    wmap = lambda m, n, k, te, nu: (li, te[m], 0, n)
    hmid = _fused_matmul(
        xs=[(xs, lambda m, n, k, *pf: (m, 0))],
        dots=[(0, wg, (None, None, D, tn), wmap), (0, wu, (None, None, D, tn), wmap)],
        epilogue=lambda p, e: p[0] * _sigmoid(p[0]) * p[1],
        M=rows, N=F, K=D, tm=tm, tn=tn, tk=D, out_dtype=BF16, order="nm", prefetch=pf,
        n_used_tiles=True, name="moe_up")

    gate_b = jnp.broadcast_to(row_gate[:, None], (rows, LANES))
    y = _moe_down(hmid, wd, li, gate_b, tile_expert, n_used, tm)

    d2 = dest.reshape(n_real, TOP_K)
    y_tok = jnp.take(y, d2[:, 0], axis=0)
    for k in range(1, TOP_K):
        y_tok = y_tok + jnp.take(y, d2[:, k], axis=0)
    return x_all.at[:n_real].add(y_tok)


def _gate_logit_weights(w_gl):
    d = w_gl.shape[0]
    w = w_gl.reshape(d, N_KV_HEADS, Q_PER_KV, 3).transpose(0, 1, 3, 2).reshape(d, N_KV_HEADS, 3 * Q_PER_KV)
    return jnp.pad(w, ((0, 0), (0, 0), (0, LANES - 3 * Q_PER_KV))).reshape(d, N_KV_HEADS * LANES)


def kernel(x_prompt, x_sample, cache_cmp, cache_sel, cache_win, state_ssm_re, state_ssm_im, page_table,
           norm_mix, w_in, ssm_a_re, ssm_a_im, ssm_b_re, ssm_b_im, ssm_c_re, ssm_c_im, ssm_d, ssm_log_dt,
           ssm_w_glu, ssm_b_glu, cmp_pe, cmp_w1, cmp_b1, cmp_w2, w_br_ssm, w_br_nsa, w_out,
           norm_ffn, ffn_w_gate, ffn_w_up, ffn_w_down, moe_router, moe_w_gate, moe_w_up, moe_w_down,
           norm_final):
    B, T, D = x_prompt.shape
    S = x_sample.shape[0]
    assert x_sample.shape[1] == 1
    depth = w_in.shape[0]
    n_prompt = B * T
    n_real = n_prompt + S
    Mp = -(-n_real // ROW_PAD) * ROW_PAD
    n_tail = Mp - n_prompt
    d_ssm = ssm_d.shape[1]
    d_nsa = N_HEADS * HEAD_DIM
    G, P = ssm_a_re.shape[1], ssm_a_re.shape[2]
    o_kv = d_ssm + d_nsa
    o_gl = o_kv + 3 * KV_COLS
    o_g = o_gl + 3 * N_HEADS
    assert d_ssm % (Q_PER_KV * HEAD_DIM) == 0 and w_in.shape[2] == o_g + 2 * D

    tm_big = _pick_tile(Mp, 1376, SUBLANES_BF16)
    tm_mid = _pick_tile(Mp, 688, SUBLANES_BF16)
    tn = 512

    x_all = jnp.concatenate([x_prompt.reshape(n_prompt, D), x_sample.reshape(S, D),
                             jnp.zeros((Mp - n_real, D), F32)], axis=0)
    w_gates = w_in[:, :, o_g:]
    pad_tail = lambda a: jnp.pad(a.reshape(S, -1), ((0, n_tail - S), (0, 0)))

    outs = {k: [] for k in ("cmp_p", "sel_p", "win_p", "sre_p", "sim_p",
                            "cmp_s", "sel_s", "win_s", "sre_s", "sim_s")}
    for l in range(depth):
        h = _rmsnorm(x_all, norm_mix[l], BF16, "norm_mix")
        uq = _dense(h, w_in, (l,), 0, o_kv, tm=tm_big, tn=tn, name="in_uq")
        kvc, kvs, kvw = [_dense(h, w_in, (l,), o_kv + i * KV_COLS, KV_COLS, tm=tm_big, tn=KV_COLS,
                                name=f"in_kv{i}") for i in range(3)]
        glg = _dense(h, _gate_logit_weights(w_in[l][:, o_gl:o_g]), (), 0, N_KV_HEADS * LANES,
                     tm=tm_big, tn=N_KV_HEADS * LANES, name="in_gate_logits")
        gates = _dense(h, w_gates, (l,), 0, 2 * D, tm=tm_big, tn=tn,
                       epilogue=lambda p, e: _sigmoid(p[0]), name="in_merge_gates")

        s0 = [pad_tail(s[l]) for s in (state_ssm_re, state_ssm_im)]
        ssm_params = (ssm_a_re, ssm_a_im, ssm_b_re, ssm_b_im, ssm_c_re, ssm_c_im, ssm_d, ssm_log_dt,
                      ssm_w_glu, ssm_b_glu)
        ssm_out, fin_p, fin_s = _s5_branch(uq, d_ssm, s0[0], s0[1], n_prompt, B, T, ssm_params, l, tm_mid)

        kd = CMP_STRIDE * HEAD_DIM
        pe = cmp_pe[l].reshape(2, 2, 1, kd)
        w1 = cmp_w1[l].reshape(2, 2, kd, HEAD_DIM)
        n_chunk = T // CMP_STRIDE
        rb = _pick_tile(n_prompt // CMP_STRIDE, 128, SUBLANES_F32)
        lt = _compress_lead_trail(
            [(kvc.reshape(Mp // CMP_STRIDE, CHUNK_COLS), (rb, CHUNK_COLS), lambda a, s: (s, 0))],
            (1, n_prompt // CMP_STRIDE // rb), pe, w1, rb, n_prompt // CMP_STRIDE, name="cmp_lt_prompt")
        cmp_kv = _compress_finish(lt, B, n_chunk, cmp_b1[l], cmp_w2[l], "cmp_finish_prompt")
        if n_chunk < LANES:
            cmp_kv = jnp.pad(cmp_kv, ((0, 0), (0, 0), (0, 0), (0, LANES - n_chunk), (0, 0)))
        q_blk0 = d_ssm // (Q_PER_KV * HEAD_DIM)
        nsa_out = _nsa_prompt(uq, q_blk0, glg, cmp_kv, kvs, kvw, n_seq=B, seq_len=T, out_rows=Mp)

        tail = lambda a, c0, c1: a[n_prompt:n_real, c0:c1]
        q_s = tail(uq, d_ssm, o_kv).reshape(S, N_KV_HEADS, Q_PER_KV, HEAD_DIM)
        gl_s =tail(glg, 0, N_KV_HEADS * LANES).reshape(S, N_KV_HEADS, LANES)[:, :, :3 * Q_PER_KV]
        gl_s = gl_s.reshape(S, N_KV_HEADS, 3, Q_PER_KV).transpose(0, 1, 3, 2)
        gate_s = jnp.pad(gl_s, ((0, 0), (0, 0), (0, 0), (0, LANES - 3)))
        new_sel = tail(kvs, 0, KV_COLS).reshape(S, 2 * N_KV_HEADS, 1, HEAD_DIM)
        new_win = tail(kvw, 0, KV_COLS).reshape(S, 2 * N_KV_HEADS, 1, HEAD_DIM)
        nsa_s = _nsa_sample(q_s, gate_s, new_sel, new_win, cache_cmp, cache_sel, cache_win, page_table, l,
                            pe, w1, cmp_b1[l], cmp_w2[l])
        nsa_out = lax.dynamic_update_slice(nsa_out, pad_tail(nsa_s).astype(BF16), (n_prompt, 0))

        ng = D // tn
        y = _fused_matmul(
            xs=[(ssm_out, lambda m, n, k, *pf: (m, 0)), (nsa_out, lambda m, n, k, *pf: (m, 0))],
            dots=[(0, w_br_ssm, (None, d_ssm, tn), lambda m, n, k, *pf: (l, 0, n)),
                  (1, w_br_nsa, (None, d_nsa, tn), lambda m, n, k, *pf: (l, 0, n))],
            extras=[(gates, (tm_mid, tn), lambda m, n, k, *pf: (m, n)),
                    (gates, (tm_mid, tn), lambda m, n, k, *pf: (m, ng + n))],
            epilogue=lambda p, e: e[0] * p[0] + e[1] * p[1],
            M=Mp, N=D, K=d_ssm, tm=tm_mid, tn=tn, tk=d_ssm, out_dtype=BF16, name="merge")
        x_all = _dense(y, w_out, (l,), 0, D, tm=tm_big, tn=tn,
                       extras=[(x_all, (tm_big, tn), lambda m, n, k, *pf: (m, n))],
                       epilogue=lambda p, e: e[0] + p[0], name="out_proj")

        kv5 = lambda a, rows, lead: a[rows].reshape(lead + (2, N_KV_HEADS, HEAD_DIM))
        outs["cmp_p"].append(kv5(kvc, slice(0, n_prompt), (B, T)))
        outs["sel_p"].append(kv5(kvs, slice(0, n_prompt), (B, T)))
        wp = min(WINDOW, T)
        outs["win_p"].append(kv5(kvw, slice(0, n_prompt), (B, T))[:, T - wp:])
        outs["sre_p"].append(fin_p[0].reshape(B, G, P))
        outs["sim_p"].append(fin_p[1].reshape(B, G, P))
        outs["cmp_s"].append(kv5(kvc, slice(n_prompt, n_real), (S, 1)))
        outs["sel_s"].append(kv5(kvs, slice(n_prompt, n_real), (S, 1)))
        win = jnp.concatenate([cache_win[l], kv5(kvw, slice(n_prompt, n_real), (S, 1))], axis=1)
        outs["win_s"].append(win[:, 1:])
        outs["sre_s"].append(fin_s[0][:S].reshape(S, G, P))
        outs["sim_s"].append(fin_s[1][:S].reshape(S, G, P))

        if l % 2 == 0:
            i = l // 2
            h2 = _rmsnorm(x_all, norm_ffn[l], BF16, "norm_ffn")
            f = ffn_w_gate.shape[2]
            tnf = _pick_tile(f, 256, LANES)
            wmap = lambda m, n, k, *pf: (i, 0, n)
            hmid = _fused_matmul(
                xs=[(h2, lambda m, n, k, *pf: (m, 0))],
                dots=[(0, ffn_w_gate, (None, D, tnf), wmap), (0, ffn_w_up, (None, D, tnf), wmap)],
                epilogue=lambda p, e: p[0] * _sigmoid(p[0]) * p[1],
                M=Mp, N=f, K=D, tm=tm_big, tn=tnf, tk=D, out_dtype=BF16, name="ffn_up")
            tk = _pick_tile(f, 1024, LANES)
            tnd = _pick_tile(D, 1024, LANES)
            x_all = _fused_matmul(
                xs=[(hmid, lambda m, n, k, *pf: (m, k))],
                dots=[(0, ffn_w_down, (None, tk, tnd), lambda m, n, k, *pf: (i, k, n))],
                extras=[(x_all, (tm_big, tnd), lambda m, n, k, *pf: (m, n))],
                epilogue=lambda p, e: e[0] + p[0],
                M=Mp, N=D, K=f, tm=tm_big, tn=tnd, tk=tk, out_dtype=F32, name="ffn_down")
        else:
            x_all = _moe(x_all, n_real, norm_ffn[l], moe_router, moe_w_gate, moe_w_up, moe_w_down, l // 2)

    y_all = _rmsnorm(x_all, norm_final, F32, "norm_final")
    st = lambda k: jnp.stack(outs[k])
    return (y_all[:n_prompt].reshape(B, T, D), y_all[n_prompt:n_real].reshape(S, 1, D),
            st("cmp_p"), st("sel_p"), st("win_p"), st("sre_p"), st("sim_p"),
            st("cmp_s"), st("sel_s"), st("win_s"), st("sre_s"), st("sim_s"))
```

```python
import functools
import math

import numpy as np
import jax
import jax.numpy as jnp
from jax import lax
from jax.experimental import pallas as pl
from jax.experimental.pallas import tpu as pltpu

F32 = jnp.float32
BF16 = jnp.bfloat16

SSM_GROUP = 16
N_HEADS = 16
HEAD_DIM = 128
N_KV_HEADS = 2
Q_PER_KV = N_HEADS // N_KV_HEADS
CMP_STRIDE = 16
CMP_BLOCK = 2 * CMP_STRIDE
SEL_BLOCK = 64
N_SEL = 16
N_LOCAL_SEL = 2
WINDOW = 512
Q_BLOCK = 128
TOP_K = 2
RMS_EPS = 1e-6
NEG_INF = -1e30
BIG = 1e30

LANES = 128
SUBLANES_F32 = 8
SUBLANES_BF16 = 16
VMEM_BYTES_V7X = 64 * 1024 * 1024
VMEM_LIMIT_CAP = VMEM_BYTES_V7X - 6 * 1024 * 1024
ROW_PAD = 64


def _vmem_limit(estimate_bytes):
    return int(min(max(estimate_bytes * 5 // 4 + (4 << 20), 16 << 20), VMEM_LIMIT_CAP))


def _pick_tile(total, target, mult):
    best = None
    for t in range(mult, min(total, target) + 1, mult):
        if total % t == 0:
            best = t
    assert best is not None, (total, target, mult)
    return best


def _nbytes(shape, dtype):
    n = 1
    for s in shape:
        if s is not None:
            n *= s
    return n * jnp.dtype(dtype).itemsize


def _gelu_tanh(x):
    return 0.5 * x * (1.0 + jnp.tanh(math.sqrt(2.0 / math.pi) * (x + 0.044715 * (x * x * x))))


def _sigmoid(x):
    return 1.0 / (1.0 + jnp.exp(-x))


def _fused_matmul(*, xs, dots, extras=(), epilogue, M, N, K, tm, tn, tk, out_dtype,
                  out_rows=None, out_map=None, order="mn", prefetch=(), x_bias=None,
                  n_used_tiles=False, w_changed=None, name):
    assert w_changed is None or (order == "nm" and K == tk)
    assert M % tm == 0 and N % tn == 0 and K % tk == 0, (M, N, K, tm, tn, tk)
    nm, nn, nk = M // tm, N // tn, K // tk
    nd, nx, ne, npf = len(dots), len(xs), len(extras), len(prefetch)
    use_acc = nk > 1
    out_rows = M if out_rows is None else out_rows

    def wrap(f):
        if order == "mn":
            return lambda a, b, k, *pf: f(a, b, k, *pf)
        return lambda a, b, k, *pf: f(b, a, k, *pf)

    def body(*refs):
        pf = refs[:npf]
        x_refs = refs[npf:npf + nx]
        pos = npf + nx
        xb_ref = None
        if x_bias is not None:
            xb_ref = refs[pos]
            pos += 1
        w_refs = refs[pos:pos + nd]
        pos += nd
        e_refs = refs[pos:pos + ne]
        pos += ne
        o_ref = refs[pos]
        acc_refs = refs[pos + 1:]
        m_id = pl.program_id(0 if order == "mn" else 1)

        def compute():
            xv = []
            for xr in x_refs:
                v = xr[...]
                if xb_ref is not None:
                    v = v + xb_ref[...]
                xv.append(v.astype(BF16))
            if w_changed is None:
                wv = [w_refs[i][...].astype(BF16) for i in range(nd)]
            else:
                @pl.when(w_changed(m_id, *pf))
                def _():
                    for i in range(nd):
                        acc_refs[i][...] = w_refs[i][...].astype(BF16)

                wv = [acc_refs[i][...] for i in range(nd)]
            parts = [jnp.dot(xv[d[0]], wv[i], preferred_element_type=F32) for i, d in enumerate(dots)]
            if not use_acc:
                o_ref[...] = epilogue(parts, [e[...] for e in e_refs]).astype(out_dtype)
                return
            k = pl.program_id(2)

            @pl.when(k == 0)
            def _():
                for i in range(nd):
                    acc_refs[i][...] = parts[i]

            @pl.when(k > 0)
            def _():
                for i in range(nd):
                    acc_refs[i][...] += parts[i]

            @pl.when(k == nk - 1)
            def _():
                o_ref[...] = epilogue([a[...] for a in acc_refs],
                                      [e[...] for e in e_refs]).astype(out_dtype)

        if n_used_tiles:
            pl.when(m_id < pf[-1][0])(compute)
        else:
            compute()

    in_specs = [pl.BlockSpec((tm, tk), wrap(f)) for _, f in xs]
    args = [a for a, _ in xs]
    est = sum(2 * _nbytes((tm, tk), a.dtype) for a, _ in xs)
    if x_bias is not None:
        in_specs.append(pl.BlockSpec((1, tk), wrap(lambda m, n, k, *pf: (0, k))))
        args.append(x_bias)
    for _, w, bs, f in dots:
        in_specs.append(pl.BlockSpec(bs, wrap(f)))
        args.append(w)
        est += 2 * _nbytes(bs, w.dtype) + _nbytes(bs, BF16)
    for a, bs, f in extras:
        in_specs.append(pl.BlockSpec(bs, wrap(f)))
        args.append(a)
        est += 2 * _nbytes(bs, a.dtype)
    if out_map is None:
        out_map = lambda m, n, k, *pf: (m, n)
    est += 2 * _nbytes((tm, tn), out_dtype) + (2 + nd) * _nbytes((tm, tn), F32)
    scratch = [pltpu.VMEM((tm, tn), F32) for _ in range(nd)] if use_acc else []
    if w_changed is not None:
        scratch = [pltpu.VMEM((tk, tn), BF16) for _ in range(nd)]
    grid = (nm, nn, nk) if order == "mn" else (nn, nm, nk)
    return pl.pallas_call(
        body,
        grid_spec=pltpu.PrefetchScalarGridSpec(
            num_scalar_prefetch=npf, grid=grid, in_specs=in_specs,
            out_specs=pl.BlockSpec((tm, tn), wrap(out_map)), scratch_shapes=scratch),
        out_shape=jax.ShapeDtypeStruct((out_rows, N), out_dtype),
        compiler_params=pltpu.CompilerParams(
            dimension_semantics=("parallel", "parallel" if w_changed is None else "arbitrary", "arbitrary"),
            vmem_limit_bytes=_vmem_limit(est)),
        name=name,
    )(*prefetch, *args)


def _first(parts, extras):
    return parts[0]


def _dense(x, w, w_lead, col0, N, *, tm, tn, epilogue=_first, extras=(), out_dtype=F32, name):
    M, K = x.shape
    assert col0 % tn == 0
    c0 = col0 // tn
    nl = len(w_lead)
    return _fused_matmul(
        xs=[(x, lambda m, n, k, *pf: (m, 0))],
        dots=[(0, w, (None,) * nl + (K, tn), lambda m, n, k, *pf: tuple(w_lead) + (0, c0 + n))],
        extras=extras, epilogue=epilogue, M=M, N=N, K=K, tm=tm, tn=tn, tk=K,
        out_dtype=out_dtype, name=name)


def _rmsnorm(x, g, out_dtype, name, rows=None):
    M, D = x.shape
    M = M if rows is None else rows
    tr = _pick_tile(M, 256, SUBLANES_BF16)

    def body(x_ref, g_ref, o_ref):
        xf = x_ref[...]
        ms = jnp.mean(xf * xf, axis=-1, keepdims=True)
        o_ref[...] = (xf * lax.rsqrt(ms + RMS_EPS) * g_ref[...]).astype(out_dtype)

    return pl.pallas_call(
        body, grid=(M // tr,),
        in_specs=[pl.BlockSpec((tr, D), lambda i: (i, 0)), pl.BlockSpec((1, D), lambda i: (0, 0))],
        out_specs=pl.BlockSpec((tr, D), lambda i: (i, 0)),
        out_shape=jax.ShapeDtypeStruct((M, D), out_dtype),
        compiler_params=pltpu.CompilerParams(dimension_semantics=("parallel",)),
        name=name,
    )(x, g.reshape(1, D))


def _s5_discretize(a_re, a_im, log_dt_b, b_re_t, b_im_t):
    C, GP = b_re_t.shape

    def body(ar_ref, ai_ref, ld_ref, br_ref, bi_ref, lr_ref, li_ref, obr_ref, obi_ref):
        ar, ai = ar_ref[...], ai_ref[...]
        dt = jnp.exp(ld_ref[...])
        mag = jnp.exp(ar * dt)
        ang = ai * dt
        lr = mag * jnp.cos(ang)
        li = mag * jnp.sin(ang)
        pr, pi = lr, li
        for k in range(SUBLANES_F32):
            lr_ref[k:k + 1, :] = pr
            li_ref[k:k + 1, :] = pi
            pr, pi = pr * lr - pi * li, pr * li + pi * lr
        x, y = lr - 1.0, li
        den = ar * ar + ai * ai
        cr = (x * ar + y * ai) / den
        ci = (y * ar - x * ai) / den
        br, bi = br_ref[...], bi_ref[...]
        obr_ref[...] = cr * br - ci * bi
        obi_ref[...] = cr * bi + ci * br

    vec = jax.ShapeDtypeStruct((SUBLANES_F32, GP), F32)
    mat = jax.ShapeDtypeStruct((C, GP), F32)
    return pl.pallas_call(body, out_shape=(vec, vec, mat, mat), name="s5_discretize")(
        a_re, a_im, log_dt_b, b_re_t, b_im_t)


GROUP_TILE = 16
S5_IN = GROUP_TILE * SSM_GROUP


def _s5_prompt(uq, w_b, w_c, pw_re, pw_im, d2, *, n_seq, seq_len, out_rows):
    ngt, kx, ns = w_b.shape[1:]
    sub = SUBLANES_F32
    tc = _pick_tile(seq_len, 256, sub)
    nch = seq_len // tc
    ntile = tc // sub

    def body(u_ref, wb_ref, wc_ref, pr_ref, pi_ref, d_ref, z_ref, fr_ref, fi_ref, st, sr_ref, si_ref):
        c = pl.program_id(2)

        @pl.when(c == 0)
        def _():
            st[...] = jnp.zeros(st.shape, F32)

        u = u_ref[...]
        ub = u.astype(BF16)
        xr = jnp.dot(ub, wb_ref[0], preferred_element_type=F32).reshape(ntile, sub, ns)
        xi = jnp.dot(ub, wb_ref[1], preferred_element_type=F32).reshape(ntile, sub, ns)
        pr, pi = pr_ref[...], pi_ref[...]
        row = lax.broadcasted_iota(jnp.int32, (sub, ns), 0)
        d = 1
        while d < sub:
            mr = jnp.where(row >= d, pr[d - 1:d, :], 0.0)
            mi = jnp.where(row >= d, pi[d - 1:d, :], 0.0)
            rr = pltpu.roll(xr, d, axis=1)
            ri = pltpu.roll(xi, d, axis=1)
            xr, xi = xr + mr * rr - mi * ri, xi + mr * ri + mi * rr
            d *= 2
        sr_ref[...] = xr
        si_ref[...] = xi

        def tile(j, carry):
            cr, ci = carry
            nr = sr_ref[j] + pr * cr - pi * ci
            ni = si_ref[j] + pr * ci + pi * cr
            sr_ref[j] = nr
            si_ref[j] = ni
            return nr[sub - 1:sub, :], ni[sub - 1:sub, :]

        cr, ci = lax.fori_loop(0, ntile, tile, (st[0:1, :], st[1:2, :]))
        st[0:1, :] = cr
        st[1:2, :] = ci
        fr_ref[...] = cr
        fi_ref[...] = ci
        y = (jnp.dot(sr_ref[...].reshape(tc, ns).astype(BF16), wc_ref[0], preferred_element_type=F32)
             + jnp.dot(si_ref[...].reshape(tc, ns).astype(BF16), wc_ref[1], preferred_element_type=F32))
        z_ref[...] = _gelu_tanh(y + d_ref[...] * u)

    u_cols = ngt * kx
    fin = jax.ShapeDtypeStruct((n_seq, ngt, 1, ns), F32)
    return pl.pallas_call(
        body, grid=(n_seq, ngt, nch),
        in_specs=[pl.BlockSpec((tc, kx), lambda b, g, c: (b * nch + c, g)),
                  pl.BlockSpec((2, None, kx, ns), lambda b, g, c: (0, g, 0, 0)),
                  pl.BlockSpec((2, None, ns, kx), lambda b, g, c: (0, g, 0, 0)),
                  pl.BlockSpec((sub, ns), lambda b, g, c: (0, g)),
                  pl.BlockSpec((sub, ns), lambda b, g, c: (0, g)),
                  pl.BlockSpec((1, kx), lambda b, g, c: (0, g))],
        out_specs=[pl.BlockSpec((tc, kx), lambda b, g, c: (b * nch + c, g)),
                   pl.BlockSpec((None, None, 1, ns), lambda b, g, c: (b, g, 0, 0)),
                   pl.BlockSpec((None, None, 1, ns), lambda b, g, c: (b, g, 0, 0))],
        out_shape=(jax.ShapeDtypeStruct((out_rows, u_cols), F32), fin, fin),
        scratch_shapes=[pltpu.VMEM((sub, ns), F32), pltpu.VMEM((ntile, sub, ns), F32),
                        pltpu.VMEM((ntile, sub, ns), F32)],
        compiler_params=pltpu.CompilerParams(
            dimension_semantics=("parallel", "parallel", "arbitrary")),
        name="s5_prompt",
    )(uq, w_b, w_c, pw_re, pw_im, d2)


def _s5_tail(uq, w_b, w_c, pw_re, pw_im, d2, s0_re, s0_im, z_prev, *, row0):
    ngt, kx, ns = w_b.shape[1:]
    n_tail = s0_re.shape[0]
    assert row0 % n_tail == 0
    rb = row0 // n_tail

    def body(u_ref, wb_ref, wc_ref, pr_ref, pi_ref, d_ref, s0r_ref, s0i_ref, zp_ref, z_ref, fr_ref, fi_ref):
        u = u_ref[...]
        ub = u.astype(BF16)
        lr, li = pr_ref[0:1, :], pi_ref[0:1, :]
        s0r, s0i = s0r_ref[...], s0i_ref[...]
        sr = lr * s0r - li * s0i + jnp.dot(ub, wb_ref[0], preferred_element_type=F32)
        si = lr * s0i + li * s0r + jnp.dot(ub, wb_ref[1], preferred_element_type=F32)
        fr_ref[...] = sr
        fi_ref[...] = si
        y = (jnp.dot(sr.astype(BF16), wc_ref[0], preferred_element_type=F32)
             + jnp.dot(si.astype(BF16), wc_ref[1], preferred_element_type=F32))
        z_ref[...] = _gelu_tanh(y + d_ref[...] * u)

    st = jax.ShapeDtypeStruct(s0_re.shape, F32)
    return pl.pallas_call(
        body, grid=(ngt,),
        in_specs=[pl.BlockSpec((n_tail, kx), lambda g: (rb, g)),
                  pl.BlockSpec((2, None, kx, ns), lambda g: (0, g, 0, 0)),
                  pl.BlockSpec((2, None, ns, kx), lambda g: (0, g, 0, 0)),
                  pl.BlockSpec((SUBLANES_F32, ns), lambda g: (0, g)),
                  pl.BlockSpec((SUBLANES_F32, ns), lambda g: (0, g)),
                  pl.BlockSpec((1, kx), lambda g: (0, g)),
                  pl.BlockSpec((n_tail, ns), lambda g: (0, g)),
                  pl.BlockSpec((n_tail, ns), lambda g: (0, g)),
                  pl.BlockSpec(memory_space=pl.ANY)],
        out_specs=[pl.BlockSpec((n_tail, kx), lambda g: (rb, g)),
                   pl.BlockSpec((n_tail, ns), lambda g: (0, g)),
                   pl.BlockSpec((n_tail, ns), lambda g: (0, g))],
        out_shape=(jax.ShapeDtypeStruct(z_prev.shape, F32), st, st),
        input_output_aliases={8: 0},
        compiler_params=pltpu.CompilerParams(dimension_semantics=("parallel",)),
        name="s5_tail",
    )(uq, w_b, w_c, pw_re, pw_im, d2, s0_re, s0_im, z_prev)


def _block_diag_tiles(w):
    eye = jnp.eye(GROUP_TILE, dtype=w.dtype)
    nt, gt, a, b = w.shape
    return jnp.einsum("tgab,gh->tgahb", w, eye).reshape(nt, gt * a, gt * b)


def _s5_branch(uq, u_cols, s0_re, s0_im, n_prompt, n_seq, seq_len, params, layer, tm):
    (a_re, a_im, b_re, b_im, c_re, c_im, d, log_dt, w_glu, b_glu) = params
    Mp = uq.shape[0]
    G, P = a_re.shape[1], a_re.shape[2]
    C = SSM_GROUP
    GP = G * P
    ngt = G // GROUP_TILE

    lam_re, lam_im, bb_re, bb_im = _s5_discretize(
        a_re[layer].reshape(1, GP), a_im[layer].reshape(1, GP),
        jnp.broadcast_to(log_dt[layer][:, None], (G, P)).reshape(1, GP),
        b_re[layer].transpose(2, 0, 1).reshape(C, GP), b_im[layer].transpose(2, 0, 1).reshape(C, GP))

    def b_tiles(bb):
        t = bb.reshape(C, ngt, GROUP_TILE, P).transpose(1, 2, 0, 3)
        return _block_diag_tiles(t)

    w_b =jnp.stack([b_tiles(bb_re), b_tiles(bb_im)]).astype(BF16)

    def c_tiles(cc):
        t = cc.reshape(ngt, GROUP_TILE, C, P).transpose(0, 1, 3, 2)
        return _block_diag_tiles(t)

    w_c = jnp.stack([c_tiles(c_re[layer]), -c_tiles(c_im[layer])]).astype(BF16)

    d2 = d[layer].reshape(1, u_cols)
    z, fp_re, fp_im = _s5_prompt(uq, w_b, w_c, lam_re, lam_im, d2, n_seq=n_seq, seq_len=seq_len,
                                 out_rows=Mp)
    z, fs_re, fs_im = _s5_tail(uq, w_b, w_c, lam_re, lam_im, d2, s0_re, s0_im, z, row0=n_prompt)

    tn = _pick_tile(u_cols, 512, LANES)
    ssm_out = _dense(z, w_glu, (layer,), 0, u_cols, tm=tm, tn=tn,
                     extras=[(z, (tm, tn), lambda m, n, k, *pf: (m, n)),
                             (b_glu[layer].reshape(1, u_cols), (1, tn), lambda m, n, k, *pf: (0, n))],
                     epilogue=lambda p, e: e[0] * _sigmoid(p[0] + e[1]),
                     out_dtype=BF16, name="s5_glu")
    return ssm_out, (fp_re, fp_im), (fs_re, fs_im)


KV_COLS = 2 * N_KV_HEADS * HEAD_DIM
CHUNK_COLS = CMP_STRIDE * KV_COLS


KV_ROWS = 2 * N_KV_HEADS


def _compress_lead_trail(srcs, grid, pe, w1, rows_per_src, n_rows, prefetch=(), token_major=False,
                         name="cmp_lt"):
    ns = len(srcs)
    npf = len(prefetch)
    rows = ns * rows_per_src
    kd = CMP_STRIDE * HEAD_DIM

    def piece(xr, j, c):
        if token_major:
            return xr[pl.ds(j * KV_ROWS + c, rows_per_src, stride=CMP_STRIDE * KV_ROWS), :]
        return xr[:, j * KV_COLS + c * HEAD_DIM: j * KV_COLS + (c + 1) * HEAD_DIM]

    def body(*refs):
        x_refs = refs[npf:npf + ns]
        pe_ref, w_ref, o_ref = refs[npf + ns:]
        for kv in range(2):
            per_head = []
            for h in range(N_KV_HEADS):
                c = kv * N_KV_HEADS + h
                per_src = [jnp.concatenate([piece(xr, j, c) for j in range(CMP_STRIDE)], axis=1)
                           for xr in x_refs]
                per_head.append(per_src[0] if ns == 1 else jnp.concatenate(per_src, axis=0))
            xs = jnp.concatenate(per_head, axis=0)
            for lt in range(2):
                xv = (xs + pe_ref[kv, lt]).astype(BF16)
                r = jnp.dot(xv, w_ref[kv, lt].astype(BF16), preferred_element_type=F32)
                for h in range(N_KV_HEADS):
                    o_ref[kv, lt, h] = r[h * rows:(h + 1) * rows]

    in_specs = [pl.BlockSpec(bs, f) for _, bs, f in srcs]
    in_specs += [pl.BlockSpec((2, 2, 1, kd), lambda *a: (0, 0, 0, 0)),
                 pl.BlockSpec((2, 2, kd, HEAD_DIM), lambda *a: (0, 0, 0, 0))]
    step_of = lambda *a: a[0] * grid[1] + a[1]
    return pl.pallas_call(
        body,
        grid_spec=pltpu.PrefetchScalarGridSpec(
            num_scalar_prefetch=npf, grid=grid, in_specs=in_specs,
            out_specs=pl.BlockSpec((2, 2, N_KV_HEADS, rows, HEAD_DIM),
                                   lambda *a: (0, 0, 0, step_of(*a), 0))),
        out_shape=jax.ShapeDtypeStruct((2, 2, N_KV_HEADS, n_rows, HEAD_DIM), F32),
        compiler_params=pltpu.CompilerParams(
            dimension_semantics=("parallel", "arbitrary"), vmem_limit_bytes=48 << 20),
        name=name,
    )(*prefetch, *[a for a, _, _ in srcs], pe, w1)


def _compress_finish(lt, n_seq, n_chunk, b1, w2, name):
    lead = lt[:, 0]
    trail = lt[:, 1].reshape(2, N_KV_HEADS, n_seq, n_chunk, HEAD_DIM)
    trail = jnp.roll(trail, -1, axis=3).reshape(lead.shape)
    rows = N_KV_HEADS * n_seq * n_chunk
    lead = lead.reshape(2, rows, HEAD_DIM)
    trail = trail.reshape(2, rows, HEAD_DIM)
    rt = _pick_tile(rows, 1024, SUBLANES_F32)

    def body(a_ref, t_ref, b_ref, w_ref, o_ref):
        hid = _gelu_tanh(a_ref[...] + t_ref[...] + b_ref[...])
        o_ref[...] = jnp.dot(hid.astype(BF16), w_ref[...].astype(BF16), preferred_element_type=F32)

    out = pl.pallas_call(
        body, grid=(2, rows // rt),
        in_specs=[pl.BlockSpec((None, rt, HEAD_DIM), lambda kv, r: (kv, r, 0)),
                  pl.BlockSpec((None, rt, HEAD_DIM), lambda kv, r: (kv, r, 0)),
                  pl.BlockSpec((None, 1, HEAD_DIM), lambda kv, r: (kv, 0, 0)),
                  pl.BlockSpec((None, HEAD_DIM, HEAD_DIM), lambda kv, r: (kv, 0, 0))],
        out_specs=pl.BlockSpec((None, rt, HEAD_DIM), lambda kv, r: (kv, r, 0)),
        out_shape=jax.ShapeDtypeStruct((2, rows, HEAD_DIM), F32),
        compiler_params=pltpu.CompilerParams(dimension_semantics=("parallel", "parallel")),
        name=name,
    )(lead, trail, b1.reshape(2, 1, HEAD_DIM), w2)
    return out.reshape(2, N_KV_HEADS, n_seq, n_chunk, HEAD_DIM)


def _overlap_matrix(n_cmp, n_blk, rows, cols):
    start = np.arange(rows)[:, None] * CMP_STRIDE
    js = np.arange(cols)[None, :] * SEL_BLOCK
    ov = (start < js + SEL_BLOCK) & (start + CMP_BLOCK > js)
    ov &= (np.arange(rows)[:, None] < n_cmp) & (np.arange(cols)[None, :] < n_blk)
    return jnp.asarray(ov.astype(np.float32))


def _masked_softmax(s, valid):
    s = jnp.where(valid, s, NEG_INF)
    p = jnp.exp(s - jnp.max(s, axis=-1, keepdims=True)) * jnp.where(valid, 1.0, 0.0)
    return p * (1.0 / jnp.maximum(jnp.sum(p, axis=-1, keepdims=True), 1e-30))


def _softmax_dot(s, valid, v):
    hg, r, n = s.shape
    s = jnp.where(valid, s, NEG_INF)
    p = jnp.exp(s - jnp.max(s, axis=-1, keepdims=True))
    den = jnp.sum(p, axis=-1, keepdims=True)
    o = jnp.dot(p.reshape(hg * r, n).astype(BF16), v, preferred_element_type=F32)
    return o.reshape(hg, r, v.shape[1]) * (1.0 / den)


_NT = (((1,), (1,)), ((), ()))


def _nsa_prompt(uq, q_blk0, glg, cmp_kv, kvs, kvw, *, n_seq, seq_len, out_rows):
    T = seq_len
    QB = Q_BLOCK
    HG = Q_PER_KV
    HD = HEAD_DIM
    nqb = T // QB
    ncp = cmp_kv.shape[3]
    n_cmp = T // CMP_STRIDE - 1
    n_blk = -(-T // SEL_BLOCK)
    n_pick = min(N_SEL, n_blk)
    assert T % QB == 0 and T % SEL_BLOCK == 0 and n_blk <= LANES and ncp <= LANES
    assert n_pick > N_LOCAL_SEL
    CH = min(512, T)
    assert T % CH == 0
    WB = min(WINDOW + QB, T)
    scale = HD ** -0.5
    blk_of_key = np.arange(T) // SEL_BLOCK
    expand = (np.arange(LANES)[:, None] == blk_of_key[None, :]).astype(np.float32)
    expand = jnp.asarray(expand.reshape(LANES, T // CH, CH).transpose(1, 0, 2), dtype=BF16)
    ov = _overlap_matrix(n_cmp, n_blk, ncp, LANES)

    perm = np.zeros((N_KV_HEADS, LANES, LANES), np.float32)
    for hh in range(N_KV_HEADS):
        for g in range(HG):
            for j in range(3):
                perm[hh, (hh * HG + g) * 3 + j, j * HG + g] = 1.0
    perm = jnp.asarray(perm)

    def body(q_ref, gl_ref, perm_ref, kc_ref, vc_ref, ks_ref, vs_ref, kw_ref, vw_ref, e_ref, ov_ref, o_ref,
             m_ref, l_ref, acc_ref):
        i = pl.program_id(2)
        start = i * QB
        q = q_ref[...]
        q_st = jnp.concatenate([q[:, g * HD:(g + 1) * HD] for g in range(HG)], axis=0).astype(BF16)
        qpos = start + lax.broadcasted_iota(jnp.int32, (QB, 1), 0)

        s = lax.dot_general(q_st, kc_ref[...].astype(BF16), _NT, preferred_element_type=F32) * scale
        n_idx = lax.broadcasted_iota(jnp.int32, (QB, ncp), 1)
        cvalid = (n_idx * CMP_STRIDE + (CMP_BLOCK - 1) <= qpos) & (n_idx < n_cmp)
        p_c = _masked_softmax(s.reshape(HG, QB, ncp), cvalid)
        o_c = jnp.dot(p_c.reshape(HG * QB, ncp).astype(BF16), vc_ref[...].astype(BF16),
                      preferred_element_type=F32).reshape(HG, QB, HD)

        imp = jnp.dot(jnp.sum(p_c, axis=0), ov_ref[...], preferred_element_type=F32,
                      precision=lax.Precision.HIGHEST)
        blk = lax.broadcasted_iota(jnp.int32, (QB, LANES), 1)
        cur = qpos // SEL_BLOCK
        forced = (blk == 0) | ((blk <= cur) & (blk > cur - N_LOCAL_SEL))
        score = jnp.where(blk > cur, -BIG, jnp.where(forced, BIG, imp))
        rank = jnp.zeros((QB, LANES), F32)
        for j2 in range(n_blk):
            col = score[:, j2:j2 + 1]
            ahead = (col > score) | ((col == score) & (blk > j2))
            rank = rank + jnp.where(ahead, 1.0, 0.0)
        sel = jnp.where((rank < n_pick) & (blk < n_blk), 1.0, 0.0).astype(BF16)

        m_ref[...] = jnp.full(m_ref.shape, NEG_INF, F32)
        l_ref[...] = jnp.zeros(l_ref.shape, F32)
        acc_ref[...] = jnp.zeros(acc_ref.shape, F32)

        def chunk(c, carry):
            k0 = pl.multiple_of(c * CH, CH)
            kch = ks_ref[pl.ds(k0, CH), :].astype(BF16)
            vch = vs_ref[pl.ds(k0, CH), :].astype(BF16)
            sc = lax.dot_general(q_st, kch, _NT, preferred_element_type=F32) * scale
            picked = jnp.dot(sel, e_ref[c], preferred_element_type=F32)
            kpos = k0 + lax.broadcasted_iota(jnp.int32, (QB, CH), 1)
            valid = (picked > 0.5) & (kpos <= qpos)
            s3 = jnp.where(valid, sc.reshape(HG, QB, CH), NEG_INF)
            m_old = m_ref[...]
            m_new = jnp.maximum(m_old, jnp.max(s3, axis=-1, keepdims=True))
            p = jnp.exp(s3 - m_new)
            alpha = jnp.exp(m_old - m_new)
            l_ref[...] = alpha * l_ref[...] + jnp.sum(p, axis=-1, keepdims=True)
            pv = jnp.dot(p.reshape(HG * QB, CH).astype(BF16), vch, preferred_element_type=F32)
            acc_ref[...] = alpha * acc_ref[...] + pv.reshape(HG, QB, HD)
            m_ref[...] = m_new
            return carry

        lax.fori_loop(0, (start + QB + CH - 1) // CH, chunk, 0)
        o_s = acc_ref[...] * (1.0 / l_ref[...])

        w0 = pl.multiple_of(jnp.minimum(jnp.maximum(start - WINDOW, 0), T - WB), QB)
        sw = lax.dot_general(q_st, kw_ref[pl.ds(w0, WB), :].astype(BF16), _NT,
                             preferred_element_type=F32) * scale
        dist = qpos - (w0 + lax.broadcasted_iota(jnp.int32, (QB, WB), 1))
        o_w = _softmax_dot(sw.reshape(HG, QB, WB), (dist >= 0) & (dist < WINDOW),
                           vw_ref[pl.ds(w0, WB), :].astype(BF16))

        gate = _sigmoid(jnp.dot(gl_ref[...], perm_ref[...], preferred_element_type=F32,
                                precision=lax.Precision.HIGHEST))
        for g in range(HG):
            og = (gate[:, g:g + 1] * o_c[g] + gate[:, HG + g:HG + g + 1] * o_s[g]
                  + gate[:, 2 * HG + g:2 * HG + g + 1] * o_w[g])
            o_ref[:, g * HD:(g + 1) * HD] = og.astype(o_ref.dtype)

    row = lambda b, h, i: b * nqb + i
    kv_spec = lambda kv: pl.BlockSpec((T, HD), lambda b, h, i: (b, kv * N_KV_HEADS + h))
    return pl.pallas_call(
        body, grid=(n_seq, N_KV_HEADS, nqb),
        in_specs=[pl.BlockSpec((QB, HG * HD), lambda b, h, i: (row(b, h, i), q_blk0 + h)),
                  pl.BlockSpec((QB, LANES), lambda b, h, i: (row(b, h, i), 0)),
                  pl.BlockSpec((None, LANES, LANES), lambda b, h, i: (h, 0, 0)),
                  pl.BlockSpec((None, None, None, ncp, HD), lambda b, h, i: (0, h, b, 0, 0)),
                  pl.BlockSpec((None, None, None, ncp, HD), lambda b, h, i: (1, h, b, 0, 0)),
                  kv_spec(0), kv_spec(1), kv_spec(0), kv_spec(1),
                  pl.BlockSpec((T // CH, LANES, CH), lambda b, h, i: (0, 0, 0)),
                  pl.BlockSpec((ncp, LANES), lambda b, h, i: (0, 0))],
        out_specs=pl.BlockSpec((QB, HG * HD), lambda b, h, i: (row(b, h, i), h)),
        out_shape=jax.ShapeDtypeStruct((out_rows, N_HEADS * HD), BF16),
        scratch_shapes=[pltpu.VMEM((HG, QB, 1), F32), pltpu.VMEM((HG, QB, 1), F32),
                        pltpu.VMEM((HG, QB, HD), F32)],
        compiler_params=pltpu.CompilerParams(
            dimension_semantics=("parallel", "parallel", "arbitrary"), vmem_limit_bytes=48 << 20),
        name="nsa_prompt",
    )(uq, glg, perm, cmp_kv, cmp_kv, kvs, kvs, kvw, kvw, expand, ov)


def _nsa_sample_select(q_s, cmp_kv, past_len):
    S = q_s.shape[0]
    HG, HD = Q_PER_KV, HEAD_DIM
    nc = cmp_kv.shape[3]
    n_cmp = nc - 1
    q_pos = past_len
    n_blk = -(-(past_len + 1) // SEL_BLOCK)
    nbp = -(-n_blk // LANES) * LANES
    n_pick = min(N_SEL, n_blk)
    cur = q_pos // SEL_BLOCK
    scale = HD ** -0.5
    ov = _overlap_matrix(n_cmp, n_blk, nc, nbp)

    def body(q_ref, kc_ref, vc_ref, ov_ref, oc_ref, idx_ref):
        q = q_ref[...].astype(BF16)
        s = lax.dot_general(q, kc_ref[...].astype(BF16), _NT, preferred_element_type=F32) * scale
        n_idx = lax.broadcasted_iota(jnp.int32, (HG, nc), 1)
        cvalid = (n_idx * CMP_STRIDE + (CMP_BLOCK - 1) <= q_pos) & (n_idx < n_cmp)
        p_c = _masked_softmax(s, cvalid)
        oc_ref[...] = jnp.dot(p_c.astype(BF16), vc_ref[...].astype(BF16), preferred_element_type=F32)
        imp = jnp.sum(jnp.dot(p_c, ov_ref[...], preferred_element_type=F32,
                              precision=lax.Precision.HIGHEST), axis=0, keepdims=True)
        blk = lax.broadcasted_iota(jnp.int32, (1, nbp), 1)
        forced = (blk == 0) | ((blk <= cur) & (blk > cur - N_LOCAL_SEL))
        score = jnp.where(blk > cur, -BIG, jnp.where(forced, BIG, imp))
        col = jnp.broadcast_to(score, (LANES, nbp)).T[:, 0:1]
        jp = lax.broadcasted_iota(jnp.int32, (nbp, nbp), 0)
        jj = lax.broadcasted_iota(jnp.int32, (nbp, nbp), 1)
        ahead = ((col > score) | ((col == score) & (jp < jj))) & (jp < n_blk)
        rank = jnp.sum(jnp.where(ahead, 1.0, 0.0), axis=0, keepdims=True)
        lane = lax.broadcasted_iota(jnp.int32, (1, LANES), 1)
        out = jnp.zeros((1, LANES), jnp.int32)
        for r in range(n_pick):
            hit = (rank == float(r)) & (blk < n_blk)
            idx_r = jnp.sum(jnp.where(hit, blk, 0), axis=1, keepdims=True)
            out = jnp.where(lane == r, idx_r, out)
        idx_ref[...] = jnp.broadcast_to(out, (SUBLANES_F32, LANES))

    o_c, idx = pl.pallas_call(
        body, grid=(S, N_KV_HEADS),
        in_specs=[pl.BlockSpec((None, None, HG, HD), lambda b, h: (b, h, 0, 0)),
                  pl.BlockSpec((None, None, None, nc, HD), lambda b, h: (0, h, b, 0, 0)),
                  pl.BlockSpec((None, None, None, nc, HD), lambda b, h: (1, h, b, 0, 0)),
                  pl.BlockSpec((nc, nbp), lambda b, h: (0, 0))],
        out_specs=[pl.BlockSpec((None, None, HG, HD), lambda b, h: (b, h, 0, 0)),
                   pl.BlockSpec((None, None, SUBLANES_F32, LANES), lambda b, h: (b, h, 0, 0))],
        out_shape=(jax.ShapeDtypeStruct((S, N_KV_HEADS, HG, HD), F32),
                   jax.ShapeDtypeStruct((S, N_KV_HEADS, SUBLANES_F32, LANES), jnp.int32)),
        compiler_params=pltpu.CompilerParams(dimension_semantics=("parallel", "parallel")),
        name="nsa_sample_select",
    )(q_s, cmp_kv, cmp_kv, ov)
    return o_c, idx[:, :, 0, :n_pick], n_pick


def _nsa_sample_attend(q_s, o_c, gate_s, src, live, cache_sel_v, layer, new_sel, cache_win_v, new_win,
                       n_pick):
    S = q_s.shape[0]
    HG, HD = Q_PER_KV, HEAD_DIM
    wb = cache_win_v.shape[2]
    scale = HD ** -0.5
    nkeys = n_pick * SEL_BLOCK

    def body(src_ref, live_ref, q_ref, oc_ref, g_ref, *rest):
        blk_refs = rest[:n_pick]
        kn_ref, vn_ref, kw_ref, vw_ref, kwn_ref, vwn_ref, o_ref = rest[n_pick:]
        b, h = pl.program_id(0), pl.program_id(1)

        def rows_of(kv):
            per_head = [jnp.concatenate(
                [r[pl.ds(kv * N_KV_HEADS + hh, SEL_BLOCK, stride=KV_ROWS), :] for r in blk_refs], axis=0)
                for hh in range(N_KV_HEADS)]
            out = per_head[0]
            for hh in range(1, N_KV_HEADS):
                out = jnp.where(h == hh, per_head[hh], out)
            return out.astype(BF16)

        base = (b * N_KV_HEADS + h) * n_pick
        q = q_ref[...]
        qb = q.astype(BF16)

        def with_new_key(s, valid, s_new, v_old, v_new):
            sm = jnp.where(valid, s, NEG_INF)
            m = jnp.maximum(jnp.max(sm, axis=-1, keepdims=True), s_new)
            p = jnp.exp(sm - m) * jnp.where(valid, 1.0, 0.0)
            p_new = jnp.exp(s_new - m)
            den = jnp.sum(p, axis=-1, keepdims=True) + p_new
            return (jnp.dot(p.astype(BF16), v_old, preferred_element_type=F32) + p_new * v_new) / den

        keys = rows_of(0)
        vals = rows_of(1)
        s = lax.dot_general(qb, keys, _NT, preferred_element_type=F32) * scale
        lane = lax.broadcasted_iota(jnp.int32, (1, nkeys), 1)
        livef = jnp.zeros((1, nkeys), F32)
        for r in range(n_pick):
            livef = jnp.where((lane >= r * SEL_BLOCK) & (lane < (r + 1) * SEL_BLOCK),
                              live_ref[base + r].astype(F32), livef)
        s_new = jnp.sum(q * kn_ref[...], axis=-1, keepdims=True) * scale
        o_s = with_new_key(s, livef > 0.5, s_new, vals, vn_ref[...])

        sw = lax.dot_general(qb, kw_ref[...].astype(BF16), _NT, preferred_element_type=F32) * scale
        wrow = lax.broadcasted_iota(jnp.int32, (1, wb), 1)
        sw_new = jnp.sum(q * kwn_ref[...], axis=-1, keepdims=True) * scale
        o_w = with_new_key(sw, wrow > wb - WINDOW, sw_new, vw_ref[...].astype(BF16), vwn_ref[...])

        gate = _sigmoid(g_ref[...])
        o_ref[...] = gate[:, 0:1] * oc_ref[...] + gate[:, 1:2] * o_s + gate[:, 2:3] * o_w

    bpp = cache_sel_v.shape[2] // (SEL_BLOCK * KV_ROWS)

    def blk_spec(r):
        def index(b, h, src_ref, live_ref):
            s = src_ref[(b * N_KV_HEADS + h) * n_pick + r]
            return (layer, s // bpp, s % bpp, 0)
        return pl.BlockSpec((None, None, SEL_BLOCK * KV_ROWS, HD), index)

    qspec = pl.BlockSpec((None, None, HG, HD), lambda b, h, *_: (b, h, 0, 0))
    gspec = pl.BlockSpec((None, None, HG, LANES), lambda b, h, *_: (b, h, 0, 0))
    new_spec = lambda kv: pl.BlockSpec((None, None, 1, HD), lambda b, h, *_: (b, kv * N_KV_HEADS + h, 0, 0))
    win_spec = lambda kv: pl.BlockSpec((None, None, wb, HD), lambda b, h, *_: (layer, b, 0, kv * N_KV_HEADS + h))
    in_specs = ([qspec, qspec, gspec] + [blk_spec(r) for r in range(n_pick)]
                + [new_spec(0), new_spec(1), win_spec(0), win_spec(1), new_spec(0), new_spec(1)])
    return pl.pallas_call(
        body,
        grid_spec=pltpu.PrefetchScalarGridSpec(
            num_scalar_prefetch=2, grid=(S, N_KV_HEADS), in_specs=in_specs,
            out_specs=pl.BlockSpec((None, None, HG, HD), lambda b, h, *_: (b, h, 0, 0))),
        out_shape=jax.ShapeDtypeStruct((S, N_KV_HEADS, HG, HD), F32),
        compiler_params=pltpu.CompilerParams(dimension_semantics=("parallel", "parallel")),
        name="nsa_sample_attend",
    )(src, live, q_s, o_c, gate_s, *([cache_sel_v] * n_pick), new_sel, new_sel,
      cache_win_v, cache_win_v, new_win, new_win)


def _nsa_sample(q_s, gate_s, new_sel, new_win, cache_cmp, cache_sel, cache_win, page_table, layer,
                pe, w1, b1, w2):
    S, n_pages = page_table.shape
    depth, n_pool, page = cache_cmp.shape[:3]
    past_len = n_pages * page
    assert page % CMP_STRIDE == 0 and page % SEL_BLOCK == 0
    cpp = page // CMP_STRIDE
    bpp = page // SEL_BLOCK
    pps = _pick_tile(n_pages, 16, 1)
    cache_v = cache_cmp.reshape(depth, n_pool, page * KV_ROWS, HEAD_DIM)
    pt_flat = page_table.reshape(-1)
    srcs = [(cache_v, (None, None, page * KV_ROWS, HEAD_DIM),
             (lambda b, s, pt, p=p: (layer, pt[b * n_pages + s * pps + p], 0, 0))) for p in range(pps)]
    nc = n_pages * cpp
    lt = _compress_lead_trail(srcs, (S, n_pages // pps), pe, w1, cpp, S * nc, prefetch=(pt_flat,),
                              token_major=True, name="cmp_lt_sample")
    cmp_kv = _compress_finish(lt, S, nc, b1, w2, "cmp_finish_sample")
    o_c, idx, n_pick = _nsa_sample_select(q_s, cmp_kv, past_len)

    n_past_blk = past_len // SEL_BLOCK
    idc = jnp.minimum(idx, n_past_blk - 1)
    pages = jnp.take_along_axis(page_table[:, None, :], idc // bpp, axis=2)
    src = (pages * bpp + idc % bpp).astype(jnp.int32).reshape(-1)
    live = (idx < n_past_blk).astype(jnp.int32).reshape(-1)
    wb = cache_win.shape[2]
    out = _nsa_sample_attend(
        q_s, o_c, gate_s, src, live, cache_sel.reshape(depth, n_pool, page * KV_ROWS, HEAD_DIM), layer,
        new_sel, cache_win.reshape(depth, S, wb, KV_COLS), new_win, n_pick)
    return out.reshape(S, N_HEADS * HEAD_DIM)


MOE_ROW_TILE = 512


def _moe_route(x, g, w_router):
    M, D = x.shape
    E = w_router.shape[1]
    tr = _pick_tile(M, 256, SUBLANES_BF16)
    wr = jnp.pad(w_router, ((0, 0), (0, LANES - E)))

    def body(x_ref, g_ref, w_ref, h_ref, r_ref):
        xf = x_ref[...]
        ms = jnp.mean(xf * xf, axis=-1, keepdims=True)
        hf = xf * lax.rsqrt(ms + RMS_EPS) * g_ref[...]
        h_ref[...] = hf.astype(BF16)
        logits = jnp.dot(hf, w_ref[...], preferred_element_type=F32, precision=lax.Precision.HIGHEST)
        lane = lax.broadcasted_iota(jnp.int32, (tr, LANES), 1)
        logits = jnp.where(lane < E, logits, -jnp.inf)
        v1 = jnp.max(logits, axis=-1, keepdims=True)
        i1 = jnp.min(jnp.where(logits == v1, lane, LANES), axis=-1, keepdims=True)
        rest = jnp.where(lane == i1, -jnp.inf, logits)
        v2 = jnp.max(rest, axis=-1, keepdims=True)
        i2 = jnp.min(jnp.where(rest == v2, lane, LANES), axis=-1, keepdims=True)
        e2 = jnp.exp(v2 - v1)
        g1 = 1.0 / (1.0 + e2)
        out = jnp.where(lane == 0, i1.astype(F32), jnp.where(lane == 1, i2.astype(F32),
                        jnp.where(lane == 2, g1, jnp.where(lane == 3, e2 * g1, 0.0))))
        r_ref[...] = out

    return pl.pallas_call(
        body, grid=(M // tr,),
        in_specs=[pl.BlockSpec((tr, D), lambda i: (i, 0)), pl.BlockSpec((1, D), lambda i: (0, 0)),
                  pl.BlockSpec((D, LANES), lambda i: (0, 0))],
        out_specs=[pl.BlockSpec((tr, D), lambda i: (i, 0)), pl.BlockSpec((tr, LANES), lambda i: (i, 0))],
        out_shape=(jax.ShapeDtypeStruct((M, D), BF16), jax.ShapeDtypeStruct((M, LANES), F32)),
        compiler_params=pltpu.CompilerParams(dimension_semantics=("parallel",)),
        name="moe_route",
    )(x, g.reshape(1, D), wr)


MOE_GROUP_TILES = 5


def _moe_down(hmid, wd, li, gate_b, tile_expert, n_used, tm):
    rows, F = hmid.shape
    D = wd.shape[-1]
    mt = MOE_GROUP_TILES
    n_tiles = rows // tm
    tk = _pick_tile(F, 2048, LANES)
    tn = _pick_tile(D, 1024, LANES)
    nk = F // tk
    n_slots = n_tiles * nk

    t = jnp.arange(n_tiles, dtype=jnp.int32)
    used = t < n_used
    same = (tile_expert[:, None] == tile_expert[None, :]) & used[None, :]
    first = jnp.min(jnp.where(same, t[None, :], n_tiles), axis=1)
    cnt = jnp.sum(same.astype(jnp.int32), axis=1)
    j = t - first
    m_local = jnp.where(used, j % mt, 0)
    g_first = t - m_local
    g_cnt = jnp.where(used, jnp.minimum(mt, cnt - (j - m_local)), 1)
    kk = jnp.arange(nk, dtype=jnp.int32)
    pos = (nk * g_first[:, None] + kk[None, :] * g_cnt[:, None] + m_local[:, None]).reshape(-1)
    n_items = (nk * n_used).astype(jnp.int32)
    order = jnp.minimum(jnp.arange(n_slots, dtype=jnp.int32), n_items - 1)
    table = lambda v: jnp.zeros((n_slots,), jnp.int32).at[pos].set(v.reshape(-1))[order]
    it_tile = table(jnp.broadcast_to(t[:, None], (n_tiles, nk)))
    it_k = table(jnp.broadcast_to(kk[None, :], (n_tiles, nk)))
    it_slot = table(jnp.broadcast_to(m_local[:, None], (n_tiles, nk)))
    it_exp = tile_expert[it_tile]
    it_out = jnp.where(it_k == nk - 1, it_tile, it_tile - it_slot)

    def body(tile_ref, k_ref, slot_ref, exp_ref, out_ref, n_ref, h_ref, w_ref, g_ref, o_ref, acc_ref,
             wb_ref):
        i = pl.program_id(1)

        @pl.when(i < n_ref[0])
        def _():
            k = k_ref[i]
            slot = slot_ref[i]

            @pl.when(slot == 0)
            def _():
                wb_ref[...] = w_ref[...].astype(BF16)

            p = jnp.dot(h_ref[...], wb_ref[...], preferred_element_type=F32)

            @pl.when(k == 0)
            def _():
                acc_ref[slot] = p

            @pl.when(k > 0)
            def _():
                acc_ref[slot] += p

            @pl.when(k == nk - 1)
            def _():
                o_ref[...] = acc_ref[slot] * g_ref[...][:, 0:1]

    est = (2 * _nbytes((tm, tk), BF16) + 2 * _nbytes((tk, tn), F32) + _nbytes((tk, tn), BF16)
           + (mt + 3) * _nbytes((tm, tn), F32))
    return pl.pallas_call(
        body,
        grid_spec=pltpu.PrefetchScalarGridSpec(
            num_scalar_prefetch=6, grid=(D // tn, n_slots),
            in_specs=[pl.BlockSpec((tm, tk), lambda n, i, tr, kr, sr, er, orr, nr: (tr[i], kr[i])),
                      pl.BlockSpec((None, None, tk, tn),
                                   lambda n, i, tr, kr, sr, er, orr, nr: (li, er[i], kr[i], n)),
                      pl.BlockSpec((tm, LANES), lambda n, i, tr, kr, sr, er, orr, nr: (tr[i], 0))],
            out_specs=pl.BlockSpec((tm, tn), lambda n, i, tr, kr, sr, er, orr, nr: (orr[i], n)),
            scratch_shapes=[pltpu.VMEM((mt, tm, tn), F32), pltpu.VMEM((tk, tn), BF16)]),
        out_shape=jax.ShapeDtypeStruct((rows, D), F32),
        compiler_params=pltpu.CompilerParams(
            dimension_semantics=("arbitrary", "arbitrary"), vmem_limit_bytes=_vmem_limit(est)),
        name="moe_down",
    )(it_tile, it_k, it_slot, it_exp, it_out, n_items.reshape(1), hmid, wd, gate_b)


def _moe(x_all, n_real, g, w_router, wg, wu, wd, li):
    Mp, D = x_all.shape
    E, _, F = wg.shape[1:]
    tm = MOE_ROW_TILE
    h, route = _moe_route(x_all, g, w_router[li])

    r = route[:n_real]
    e_flat = r[:, :TOP_K].astype(jnp.int32).reshape(-1)
    gates = r[:, TOP_K:2 * TOP_K].reshape(-1)
    n_pairs = n_real * TOP_K
    onehot = (e_flat[:, None] == jnp.arange(E)[None, :]).astype(jnp.int32)
    counts = jnp.sum(onehot, axis=0)
    within = jnp.take_along_axis(jnp.cumsum(onehot, axis=0) - onehot, e_flat[:, None], axis=1)[:, 0]
    padded = (counts + tm - 1) // tm * tm
    ends = jnp.cumsum(padded)
    dest = (ends - padded)[e_flat] + within
    n_tiles = -(-n_pairs // tm) + E
    rows = n_tiles * tm
    src_tok = jnp.zeros((rows,), jnp.int32).at[dest].set(jnp.arange(n_pairs, dtype=jnp.int32) // TOP_K)
    row_gate = jnp.zeros((rows,), F32).at[dest].set(gates)
    n_used = (ends[-1] // tm).astype(jnp.int32)
    tile_ids = jnp.minimum(jnp.arange(n_tiles, dtype=jnp.int32), n_used - 1)
    tile_expert = jnp.minimum(jnp.searchsorted(ends, tile_ids * tm, side="right"), E - 1).astype(jnp.int32)
    pf = (tile_expert, n_used.reshape(1))
    xs = jnp.take(h, src_tok, axis=0, mode="clip")

    tn = _pick_tile(F, 512, LANES)
    wmap = lambda m, n, k, te, nu: (li, te[m], 0, n)
    hmid = _fused_matmul(
        xs=[(xs, lambda m, n, k, *pf: (m, 0))],
        dots=[(0, wg, (None, None, D, tn), wmap), (0, wu, (None, None, D, tn), wmap)],
        epilogue=lambda p, e: p[0] * _sigmoid(p[0]) * p[1],
        M=rows, N=F, K=D, tm=tm, tn=tn, tk=D, out_dtype=BF16, order="nm", prefetch=pf,
        n_used_tiles=True, name="moe_up",
        w_changed=lambda m, te, nu: (m == 0) | (te[m] != te[jnp.maximum(m - 1, 0)]))

    gate_b = jnp.broadcast_to(row_gate[:, None], (rows, LANES))
    y = _moe_down(hmid, wd, li, gate_b, tile_expert, n_used, tm)

    d2 = dest.reshape(n_real, TOP_K)
    y_tok = jnp.take(y, d2[:, 0], axis=0, mode="clip")
    for k in range(1, TOP_K):
        y_tok = y_tok + jnp.take(y, d2[:, k], axis=0, mode="clip")
    return x_all.at[:n_real].add(y_tok)


def kernel(x_prompt, x_sample, cache_cmp, cache_sel, cache_win, state_ssm_re, state_ssm_im, page_table,
           norm_mix, w_in, ssm_a_re, ssm_a_im, ssm_b_re, ssm_b_im, ssm_c_re, ssm_c_im, ssm_d, ssm_log_dt,
           ssm_w_glu, ssm_b_glu, cmp_pe, cmp_w1, cmp_b1, cmp_w2, w_br_ssm, w_br_nsa, w_out,
           norm_ffn, ffn_w_gate, ffn_w_up, ffn_w_down, moe_router, moe_w_gate, moe_w_up, moe_w_down,
           norm_final):
    B, T, D = x_prompt.shape
    S = x_sample.shape[0]
    assert x_sample.shape[1] == 1
    depth = w_in.shape[0]
    n_prompt = B * T
    n_real = n_prompt + S
    Mp = -(-n_real // ROW_PAD) * ROW_PAD
    n_tail = Mp - n_prompt
    d_ssm = ssm_d.shape[1]
    d_nsa = N_HEADS * HEAD_DIM
    G, P = ssm_a_re.shape[1], ssm_a_re.shape[2]
    o_kv = d_ssm + d_nsa
    o_gl = o_kv + 3 * KV_COLS
    o_g = o_gl + 3 * N_HEADS
    assert d_ssm % (Q_PER_KV * HEAD_DIM) == 0 and w_in.shape[2] == o_g + 2 * D

    tm_big = _pick_tile(Mp, 1376, SUBLANES_BF16)
    tm_mid = _pick_tile(Mp, 688, SUBLANES_BF16)
    tn = 512

    x_all = jnp.concatenate([x_prompt.reshape(n_prompt, D), x_sample.reshape(S, D),
                             jnp.zeros((Mp - n_real, D), F32)], axis=0)
    w_gates = w_in[:, :, o_g:]
    pad_tail = lambda a: jnp.pad(a.reshape(S, -1), ((0, n_tail - S), (0, 0)))

    outs = {k: [] for k in ("cmp_p", "sel_p", "win_p", "sre_p", "sim_p",
                            "cmp_s", "sel_s", "win_s", "sre_s", "sim_s")}
    for l in range(depth):
        h = _rmsnorm(x_all, norm_mix[l], BF16, "norm_mix")
        uq = _dense(h, w_in, (l,), 0, o_kv, tm=tm_big, tn=tn, name="in_uq")
        kvc, kvs, kvw = [_dense(h, w_in, (l,), o_kv + i * KV_COLS, KV_COLS, tm=tm_big, tn=KV_COLS,
                                name=f"in_kv{i}") for i in range(3)]
        glg = _dense(h, w_in, (l,), o_gl, LANES, tm=tm_big, tn=LANES, name="in_gate_logits")
        gates = _dense(h, w_gates, (l,), 0, 2 * D, tm=tm_big, tn=tn,
                       epilogue=lambda p, e: _sigmoid(p[0]), name="in_merge_gates")

        s0 = [pad_tail(s[l]) for s in (state_ssm_re, state_ssm_im)]
        ssm_params = (ssm_a_re, ssm_a_im, ssm_b_re, ssm_b_im, ssm_c_re, ssm_c_im, ssm_d, ssm_log_dt,
                      ssm_w_glu, ssm_b_glu)
        ssm_out, fin_p, fin_s = _s5_branch(uq, d_ssm, s0[0], s0[1], n_prompt, B, T, ssm_params, l, tm_mid)

        kd = CMP_STRIDE * HEAD_DIM
        pe = cmp_pe[l].reshape(2, 2, 1, kd)
        w1 = cmp_w1[l].reshape(2, 2, kd, HEAD_DIM)
        n_chunk = T // CMP_STRIDE
        rb = _pick_tile(n_prompt // CMP_STRIDE, 128, SUBLANES_F32)
        lt = _compress_lead_trail(
            [(kvc.reshape(Mp // CMP_STRIDE, CHUNK_COLS), (rb, CHUNK_COLS), lambda a, s: (s, 0))],
            (1, n_prompt // CMP_STRIDE // rb), pe, w1, rb, n_prompt // CMP_STRIDE, name="cmp_lt_prompt")
        cmp_kv = _compress_finish(lt, B, n_chunk, cmp_b1[l], cmp_w2[l], "cmp_finish_prompt")
        if n_chunk < LANES:
            cmp_kv = jnp.pad(cmp_kv, ((0, 0), (0, 0), (0, 0), (0, LANES - n_chunk), (0, 0)))
        q_blk0 = d_ssm // (Q_PER_KV * HEAD_DIM)
        nsa_out = _nsa_prompt(uq, q_blk0, glg, cmp_kv, kvs, kvw, n_seq=B, seq_len=T, out_rows=Mp)

        tail = lambda a, c0, c1: a[n_prompt:n_real, c0:c1]
        q_s = tail(uq, d_ssm, o_kv).reshape(S, N_KV_HEADS, Q_PER_KV, HEAD_DIM)
        gl_s = tail(glg, 0, 3 * N_HEADS).reshape(S, N_KV_HEADS, Q_PER_KV, 3)
        gate_s = jnp.pad(gl_s, ((0, 0), (0, 0), (0, 0), (0, LANES - 3)))
        new_sel = tail(kvs, 0, KV_COLS).reshape(S, 2 * N_KV_HEADS, 1, HEAD_DIM)
        new_win = tail(kvw, 0, KV_COLS).reshape(S, 2 * N_KV_HEADS, 1, HEAD_DIM)
        nsa_s = _nsa_sample(q_s, gate_s, new_sel, new_win, cache_cmp, cache_sel, cache_win, page_table, l,
                            pe, w1, cmp_b1[l], cmp_w2[l])
        nsa_out = lax.dynamic_update_slice(nsa_out, pad_tail(nsa_s).astype(BF16), (n_prompt, 0))

        ng = D // tn
        y = _fused_matmul(
            xs=[(ssm_out, lambda m, n, k, *pf: (m, 0)), (nsa_out, lambda m, n, k, *pf: (m, 0))],
            dots=[(0, w_br_ssm, (None, d_ssm, tn), lambda m, n, k, *pf: (l, 0, n)),
                  (1, w_br_nsa, (None, d_nsa, tn), lambda m, n, k, *pf: (l, 0, n))],
            extras=[(gates, (tm_mid, tn), lambda m, n, k, *pf: (m, n)),
                    (gates, (tm_mid, tn), lambda m, n, k, *pf: (m, ng + n))],
            epilogue=lambda p, e: e[0] * p[0] + e[1] * p[1],
            M=Mp, N=D, K=d_ssm, tm=tm_mid, tn=tn, tk=d_ssm, out_dtype=BF16, name="merge")
        x_all = _dense(y, w_out, (l,), 0, D, tm=tm_big, tn=tn,
                       extras=[(x_all, (tm_big, tn), lambda m, n, k, *pf: (m, n))],
                       epilogue=lambda p, e: e[0] + p[0], name="out_proj")

        kv5 = lambda a, rows, lead: a[rows].reshape(lead + (2, N_KV_HEADS, HEAD_DIM))
        outs["cmp_p"].append(kv5(kvc, slice(0, n_prompt), (B, T)))
        outs["sel_p"].append(kv5(kvs, slice(0, n_prompt), (B, T)))
        wp = min(WINDOW, T)
        outs["win_p"].append(kv5(kvw, slice(0, n_prompt), (B, T))[:, T - wp:])
        outs["sre_p"].append(fin_p[0].reshape(B, G, P))
        outs["sim_p"].append(fin_p[1].reshape(B, G, P))
        outs["cmp_s"].append(kv5(kvc, slice(n_prompt, n_real), (S, 1)))
        outs["sel_s"].append(kv5(kvs, slice(n_prompt, n_real), (S, 1)))
        win = jnp.concatenate([cache_win[l], kv5(kvw, slice(n_prompt, n_real), (S, 1))], axis=1)
        outs["win_s"].append(win[:, 1:])
        outs["sre_s"].append(fin_s[0][:S].reshape(S, G, P))
        outs["sim_s"].append(fin_s[1][:S].reshape(S, G, P))

        if l % 2 == 0:
            i = l // 2
            h2 = _rmsnorm(x_all, norm_ffn[l], BF16, "norm_ffn")
            f = ffn_w_gate.shape[2]
            tnf = _pick_tile(f, 256, LANES)
            wmap = lambda m, n, k, *pf: (i, 0, n)
            hmid = _fused_matmul(
                xs=[(h2, lambda m, n, k, *pf: (m, 0))],
                dots=[(0, ffn_w_gate, (None, D, tnf), wmap), (0, ffn_w_up, (None, D, tnf), wmap)],
                epilogue=lambda p, e: p[0] * _sigmoid(p[0]) * p[1],
                M=Mp, N=f, K=D, tm=tm_big, tn=tnf, tk=D, out_dtype=BF16, name="ffn_up")
            tk = _pick_tile(f, 1024, LANES)
            tnd = _pick_tile(D, 1024, LANES)
            x_all = _fused_matmul(
                xs=[(hmid, lambda m, n, k, *pf: (m, k))],
                dots=[(0, ffn_w_down, (None, tk, tnd), lambda m, n, k, *pf: (i, k, n))],
                extras=[(x_all, (tm_big, tnd), lambda m, n, k, *pf: (m, n))],
                epilogue=lambda p, e: e[0] + p[0],
                M=Mp, N=D, K=f, tm=tm_big, tn=tnd, tk=tk, out_dtype=F32, name="ffn_down")
        else:
            x_all = _moe(x_all, n_real, norm_ffn[l], moe_router, moe_w_gate, moe_w_up, moe_w_down, l // 2)

    y_p = _rmsnorm(x_all, norm_final, F32, "norm_final", rows=n_prompt)
    y_s = _rmsnorm(x_all[n_prompt:], norm_final, F32, "norm_final_tail")
    st = lambda k: jnp.stack(outs[k])
    return (y_p.reshape(B, T, D), y_s[:S].reshape(S, 1, D),
            st("cmp_p"), st("sel_p"), st("win_p"), st("sre_p"), st("sim_p"),
            st("cmp_s"), st("sel_s"), st("win_s"), st("sre_s"), st("sim_s"))
```

```python
import functools
import math

import numpy as np
import jax
import jax.numpy as jnp
from jax import lax
from jax.experimental import pallas as pl
from jax.experimental.pallas import tpu as pltpu

F32 = jnp.float32
BF16 = jnp.bfloat16

SSM_GROUP = 16
N_HEADS = 16
HEAD_DIM = 128
N_KV_HEADS = 2
Q_PER_KV = N_HEADS // N_KV_HEADS
CMP_STRIDE = 16
CMP_BLOCK = 2 * CMP_STRIDE
SEL_BLOCK = 64
N_SEL = 16
N_LOCAL_SEL = 2
WINDOW = 512
Q_BLOCK = 128
TOP_K = 2
RMS_EPS = 1e-6
NEG_INF = -1e30
BIG = 1e30

LANES = 128
SUBLANES_F32 = 8
SUBLANES_BF16 = 16
VMEM_BYTES_V7X = 64 * 1024 * 1024
VMEM_LIMIT_CAP = VMEM_BYTES_V7X - 6 * 1024 * 1024
ROW_PAD = 64


def _vmem_limit(estimate_bytes):
    return int(min(max(estimate_bytes * 5 // 4 + (4 << 20), 16 << 20), VMEM_LIMIT_CAP))


def _pick_tile(total, target, mult):
    best = None
    for t in range(mult, min(total, target) + 1, mult):
        if total % t == 0:
            best = t
    assert best is not None, (total, target, mult)
    return best


def _nbytes(shape, dtype):
    n = 1
    for s in shape:
        if s is not None:
            n *= s
    return n * jnp.dtype(dtype).itemsize


def _gelu_tanh(x):
    return 0.5 * x * (1.0 + jnp.tanh(math.sqrt(2.0 / math.pi) * (x + 0.044715 * (x * x * x))))


def _sigmoid(x):
    return 1.0 / (1.0 + jnp.exp(-x))


def _fused_matmul(*, xs, dots, extras=(), epilogue, M, N, K, tm, tn, tk, out_dtype,
                  out_rows=None, out_map=None, order="mn", prefetch=(), x_bias=None,
                  n_used_tiles=False, x_packed=False, name):
    assert not x_packed or K == tk
    assert M % tm == 0 and N % tn == 0 and K % tk == 0, (M, N, K, tm, tn, tk)
    nm, nn, nk = M // tm, N // tn, K // tk
    nd, nx, ne, npf = len(dots), len(xs), len(extras), len(prefetch)
    use_acc = nk > 1
    out_rows = M if out_rows is None else out_rows

    def wrap(f):
        if order == "mn":
            return lambda a, b, k, *pf: f(a, b, k, *pf)
        return lambda a, b, k, *pf: f(b, a, k, *pf)

    def body(*refs):
        pf = refs[:npf]
        x_refs = refs[npf:npf + nx]
        pos = npf + nx
        xb_ref = None
        if x_bias is not None:
            xb_ref = refs[pos]
            pos += 1
        w_refs = refs[pos:pos + nd]
        pos += nd
        e_refs = refs[pos:pos + ne]
        pos += ne
        o_ref = refs[pos]
        acc_refs = refs[pos + 1:]
        m_id = pl.program_id(0 if order == "mn" else 1)

        def compute():
            xv = []
            for xr in x_refs:
                v = xr[...]
                if x_packed:
                    lo = lax.bitcast_convert_type(v << 16, F32)
                    hi = lax.bitcast_convert_type(v & jnp.uint32(0xFFFF0000), F32)
                    v = jnp.concatenate([lo, hi], axis=1)
                if xb_ref is not None:
                    v = v + xb_ref[...]
                xv.append(v.astype(BF16))
            parts = []
            for i, d in enumerate(dots):
                wv = w_refs[i][...].astype(BF16)
                if len(d) > 4 and d[4]:
                    parts.append(lax.dot_general(xv[d[0]], wv, _NT, preferred_element_type=F32))
                else:
                    parts.append(jnp.dot(xv[d[0]], wv, preferred_element_type=F32))
            if not use_acc:
                o_ref[...] = epilogue(parts, [e[...] for e in e_refs]).astype(out_dtype)
                return
            k = pl.program_id(2)

            @pl.when(k == 0)
            def _():
                for i in range(nd):
                    acc_refs[i][...] = parts[i]

            @pl.when(k > 0)
            def _():
                for i in range(nd):
                    acc_refs[i][...] += parts[i]

            @pl.when(k == nk - 1)
            def _():
                o_ref[...] = epilogue([a[...] for a in acc_refs],
                                      [e[...] for e in e_refs]).astype(out_dtype)

        if n_used_tiles:
            pl.when(m_id < pf[-1][0])(compute)
        else:
            compute()

    xw = tk // 2 if x_packed else tk
    in_specs = [pl.BlockSpec((tm, xw), wrap(f)) for _, f in xs]
    args = [a for a, _ in xs]
    est = sum(2 * _nbytes((tm, xw), a.dtype) for a, _ in xs)
    if x_packed:
        est += nx * 3 * _nbytes((tm, tk), BF16)
    if x_bias is not None:
        in_specs.append(pl.BlockSpec((1, tk), wrap(lambda m, n, k, *pf: (0, k))))
        args.append(x_bias)
    for d in dots:
        w, bs, f = d[1:4]
        in_specs.append(pl.BlockSpec(bs, wrap(f)))
        args.append(w)
        est += 2 * _nbytes(bs, w.dtype) + _nbytes(bs, BF16)
    for a, bs, f in extras:
        in_specs.append(pl.BlockSpec(bs, wrap(f)))
        args.append(a)
        est += 2 * _nbytes(bs, a.dtype)
    if out_map is None:
        out_map = lambda m, n, k, *pf: (m, n)
    est += 2 * _nbytes((tm, tn), out_dtype) + (2 + nd) * _nbytes((tm, tn), F32)
    scratch = [pltpu.VMEM((tm, tn), F32) for _ in range(nd)] if use_acc else []
    grid = (nm, nn, nk) if order == "mn" else (nn, nm, nk)
    return pl.pallas_call(
        body,
        grid_spec=pltpu.PrefetchScalarGridSpec(
            num_scalar_prefetch=npf, grid=grid, in_specs=in_specs,
            out_specs=pl.BlockSpec((tm, tn), wrap(out_map)), scratch_shapes=scratch),
        out_shape=jax.ShapeDtypeStruct((out_rows, N), out_dtype),
        compiler_params=pltpu.CompilerParams(
            dimension_semantics=("parallel", "parallel", "arbitrary"),
            vmem_limit_bytes=_vmem_limit(est)),
        name=name,
    )(*prefetch, *args)


def _first(parts, extras):
    return parts[0]


def _dense(x, w, w_lead, col0, N, *, tm, tn, epilogue=_first, extras=(), out_dtype=F32, w_t=False, name):
    M, K = x.shape
    assert col0 % tn == 0
    c0 = col0 // tn
    nl = len(w_lead)
    if w_t:
        dot = (0, w, (None,) * nl + (tn, K), lambda m, n, k, *pf: tuple(w_lead) + (c0 + n, 0), True)
    else:
        dot = (0, w, (None,) * nl + (K, tn), lambda m, n, k, *pf: tuple(w_lead) + (0, c0 + n))
    return _fused_matmul(
        xs=[(x, lambda m, n, k, *pf: (m, 0))],
        dots=[dot],
        extras=extras, epilogue=epilogue, M=M, N=N, K=K, tm=tm, tn=tn, tk=K,
        out_dtype=out_dtype, name=name)


def _dense_t_unaligned(x, w, w_lead, row0, N, *, tm, tn, epilogue, out_dtype, name):
    M, K = x.shape
    sh = row0 % tn
    assert sh % SUBLANES_F32 == 0 and N % tn == 0 and M % tm == 0
    b0 = row0 // tn
    nl = len(w_lead)

    def body(x_ref, wa_ref, wb_ref, o_ref):
        wv = jnp.concatenate([wa_ref[sh:, :], wb_ref[:sh, :]], axis=0).astype(BF16)
        acc = lax.dot_general(x_ref[...], wv, _NT, preferred_element_type=F32)
        o_ref[...] = epilogue(acc).astype(out_dtype)

    wspec = lambda off: pl.BlockSpec((None,) * nl + (tn, K), lambda m, n: tuple(w_lead) + (b0 + off + n, 0))
    est = (2 * _nbytes((tm, K), x.dtype) + 4 * _nbytes((tn, K), F32) + 2 * _nbytes((tn, K), BF16)
           + 4 * _nbytes((tm, tn), F32))
    return pl.pallas_call(
        body, grid=(M // tm, N // tn),
        in_specs=[pl.BlockSpec((tm, K), lambda m, n: (m, 0)), wspec(0), wspec(1)],
        out_specs=pl.BlockSpec((tm, tn), lambda m, n: (m, n)),
        out_shape=jax.ShapeDtypeStruct((M, N), out_dtype),
        compiler_params=pltpu.CompilerParams(
            dimension_semantics=("parallel", "parallel"), vmem_limit_bytes=_vmem_limit(est)),
        name=name,
    )(x, w, w)


def _rmsnorm(x, g, out_dtype, name, rows=None):
    M, D = x.shape
    M = M if rows is None else rows
    tr = _pick_tile(M, 256, SUBLANES_BF16)

    def body(x_ref, g_ref, o_ref):
        xf = x_ref[...]
        ms = jnp.mean(xf * xf, axis=-1, keepdims=True)
        o_ref[...] = (xf * lax.rsqrt(ms + RMS_EPS) * g_ref[...]).astype(out_dtype)

    return pl.pallas_call(
        body, grid=(M // tr,),
        in_specs=[pl.BlockSpec((tr, D), lambda i: (i, 0)), pl.BlockSpec((1, D), lambda i: (0, 0))],
        out_specs=pl.BlockSpec((tr, D), lambda i: (i, 0)),
        out_shape=jax.ShapeDtypeStruct((M, D), out_dtype),
        compiler_params=pltpu.CompilerParams(dimension_semantics=("parallel",)),
        name=name,
    )(x, g.reshape(1, D))


def _s5_discretize(a_re, a_im, log_dt_b, b_re_t, b_im_t):
    C, GP = b_re_t.shape

    def body(ar_ref, ai_ref, ld_ref, br_ref, bi_ref, lr_ref, li_ref, obr_ref, obi_ref):
        ar, ai = ar_ref[...], ai_ref[...]
        dt = jnp.exp(ld_ref[...])
        mag = jnp.exp(ar * dt)
        ang = ai * dt
        lr = mag * jnp.cos(ang)
        li = mag * jnp.sin(ang)
        pr, pi = lr, li
        for k in range(SUBLANES_F32):
            lr_ref[k:k + 1, :] = pr
            li_ref[k:k + 1, :] = pi
            pr, pi = pr * lr - pi * li, pr * li + pi * lr
        x, y = lr - 1.0, li
        den = ar * ar + ai * ai
        cr = (x * ar + y * ai) / den
        ci = (y * ar - x * ai) / den
        br, bi = br_ref[...], bi_ref[...]
        obr_ref[...] = cr * br - ci * bi
        obi_ref[...] = cr * bi + ci * br

    vec = jax.ShapeDtypeStruct((SUBLANES_F32, GP), F32)
    mat = jax.ShapeDtypeStruct((C, GP), F32)
    return pl.pallas_call(body, out_shape=(vec, vec, mat, mat), name="s5_discretize")(
        a_re, a_im, log_dt_b, b_re_t, b_im_t)


GROUP_TILE = 16
S5_IN = GROUP_TILE * SSM_GROUP


def _s5_prompt(uq, w_b, w_c, pw_re, pw_im, d2, *, n_seq, seq_len, out_rows):
    ngt, kx, ns = w_b.shape[1:]
    sub = SUBLANES_F32
    tc = _pick_tile(seq_len, 256, sub)
    nch = seq_len // tc
    ntile = tc // sub

    def body(u_ref, wb_ref, wc_ref, pr_ref, pi_ref, d_ref, z_ref, fr_ref, fi_ref, st, sr_ref, si_ref):
        c = pl.program_id(2)

        @pl.when(c == 0)
        def _():
            st[...] = jnp.zeros(st.shape, F32)

        u = u_ref[...]
        ub = u.astype(BF16)
        xr = jnp.dot(ub, wb_ref[0], preferred_element_type=F32).reshape(ntile, sub, ns)
        xi = jnp.dot(ub, wb_ref[1], preferred_element_type=F32).reshape(ntile, sub, ns)
        pr, pi = pr_ref[...], pi_ref[...]
        row = lax.broadcasted_iota(jnp.int32, (sub, ns), 0)
        d = 1
        while d < sub:
            mr = jnp.where(row >= d, pr[d - 1:d, :], 0.0)
            mi = jnp.where(row >= d, pi[d - 1:d, :], 0.0)
            rr = pltpu.roll(xr, d, axis=1)
            ri = pltpu.roll(xi, d, axis=1)
            xr, xi = xr + mr * rr - mi * ri, xi + mr * ri + mi * rr
            d *= 2
        sr_ref[...] = xr
        si_ref[...] = xi

        def tile(j, carry):
            cr, ci = carry
            nr = sr_ref[j] + pr * cr - pi * ci
            ni = si_ref[j] + pr * ci + pi * cr
            sr_ref[j] = nr
            si_ref[j] = ni
            return nr[sub - 1:sub, :], ni[sub - 1:sub, :]

        cr, ci = lax.fori_loop(0, ntile, tile, (st[0:1, :], st[1:2, :]))
        st[0:1, :] = cr
        st[1:2, :] = ci
        fr_ref[...] = cr
        fi_ref[...] = ci
        y = (jnp.dot(sr_ref[...].reshape(tc, ns).astype(BF16), wc_ref[0], preferred_element_type=F32)
             + jnp.dot(si_ref[...].reshape(tc, ns).astype(BF16), wc_ref[1], preferred_element_type=F32))
        z_ref[...] = _gelu_tanh(y + d_ref[...] * u)

    u_cols = ngt * kx
    fin = jax.ShapeDtypeStruct((n_seq, ngt, 1, ns), F32)
    return pl.pallas_call(
        body, grid=(n_seq, ngt, nch),
        in_specs=[pl.BlockSpec((tc, kx), lambda b, g, c: (b * nch + c, g)),
                  pl.BlockSpec((2, None, kx, ns), lambda b, g, c: (0, g, 0, 0)),
                  pl.BlockSpec((2, None, ns, kx), lambda b, g, c: (0, g, 0, 0)),
                  pl.BlockSpec((sub, ns), lambda b, g, c: (0, g)),
                  pl.BlockSpec((sub, ns), lambda b, g, c: (0, g)),
                  pl.BlockSpec((1, kx), lambda b, g, c: (0, g))],
        out_specs=[pl.BlockSpec((tc, kx), lambda b, g, c: (b * nch + c, g)),
                   pl.BlockSpec((None, None, 1, ns), lambda b, g, c: (b, g, 0, 0)),
                   pl.BlockSpec((None, None, 1, ns), lambda b, g, c: (b, g, 0, 0))],
        out_shape=(jax.ShapeDtypeStruct((out_rows, u_cols), F32), fin, fin),
        scratch_shapes=[pltpu.VMEM((sub, ns), F32), pltpu.VMEM((ntile, sub, ns), F32),
                        pltpu.VMEM((ntile, sub, ns), F32)],
        compiler_params=pltpu.CompilerParams(
            dimension_semantics=("parallel", "parallel", "arbitrary")),
        name="s5_prompt",
    )(uq, w_b, w_c, pw_re, pw_im, d2)


def _s5_tail(uq, w_b, w_c, pw_re, pw_im, d2, s0_re, s0_im, z_prev, *, row0):
    ngt, kx, ns = w_b.shape[1:]
    n_tail = s0_re.shape[0]
    assert row0 % n_tail == 0
    rb = row0 // n_tail

    def body(u_ref, wb_ref, wc_ref, pr_ref, pi_ref, d_ref, s0r_ref, s0i_ref, zp_ref, z_ref, fr_ref, fi_ref):
        u = u_ref[...]
        ub = u.astype(BF16)
        lr, li = pr_ref[0:1, :], pi_ref[0:1, :]
        s0r, s0i = s0r_ref[...], s0i_ref[...]
        sr = lr * s0r - li * s0i + jnp.dot(ub, wb_ref[0], preferred_element_type=F32)
        si = lr * s0i + li * s0r + jnp.dot(ub, wb_ref[1], preferred_element_type=F32)
        fr_ref[...] = sr
        fi_ref[...] = si
        y = (jnp.dot(sr.astype(BF16), wc_ref[0], preferred_element_type=F32)
             + jnp.dot(si.astype(BF16), wc_ref[1], preferred_element_type=F32))
        z_ref[...] = _gelu_tanh(y + d_ref[...] * u)

    st = jax.ShapeDtypeStruct(s0_re.shape, F32)
    return pl.pallas_call(
        body, grid=(ngt,),
        in_specs=[pl.BlockSpec((n_tail, kx), lambda g: (rb, g)),
                  pl.BlockSpec((2, None, kx, ns), lambda g: (0, g, 0, 0)),
                  pl.BlockSpec((2, None, ns, kx), lambda g: (0, g, 0, 0)),
                  pl.BlockSpec((SUBLANES_F32, ns), lambda g: (0, g)),
                  pl.BlockSpec((SUBLANES_F32, ns), lambda g: (0, g)),
                  pl.BlockSpec((1, kx), lambda g: (0, g)),
                  pl.BlockSpec((n_tail, ns), lambda g: (0, g)),
                  pl.BlockSpec((n_tail, ns), lambda g: (0, g)),
                  pl.BlockSpec(memory_space=pl.ANY)],
        out_specs=[pl.BlockSpec((n_tail, kx), lambda g: (rb, g)),
                   pl.BlockSpec((n_tail, ns), lambda g: (0, g)),
                   pl.BlockSpec((n_tail, ns), lambda g: (0, g))],
        out_shape=(jax.ShapeDtypeStruct(z_prev.shape, F32), st, st),
        input_output_aliases={8: 0},
        compiler_params=pltpu.CompilerParams(dimension_semantics=("parallel",)),
        name="s5_tail",
    )(uq, w_b, w_c, pw_re, pw_im, d2, s0_re, s0_im, z_prev)


def _block_diag_tiles(w):
    eye = jnp.eye(GROUP_TILE, dtype=w.dtype)
    nt, gt, a, b = w.shape
    return jnp.einsum("tgab,gh->tgahb", w, eye).reshape(nt, gt * a, gt * b)


def _s5_branch(uq, u_cols, s0_re, s0_im, n_prompt, n_seq, seq_len, params, layer, tm):
    (a_re, a_im, b_re, b_im, c_re, c_im, d, log_dt, w_glu, b_glu) = params
    Mp = uq.shape[0]
    G, P = a_re.shape[1], a_re.shape[2]
    C = SSM_GROUP
    GP = G * P
    ngt = G // GROUP_TILE

    lam_re, lam_im, bb_re, bb_im = _s5_discretize(
        a_re[layer].reshape(1, GP), a_im[layer].reshape(1, GP),
        jnp.broadcast_to(log_dt[layer][:, None], (G, P)).reshape(1, GP),
        b_re[layer].transpose(2, 0, 1).reshape(C, GP), b_im[layer].transpose(2, 0, 1).reshape(C, GP))

    def b_tiles(bb):
        t = bb.reshape(C, ngt, GROUP_TILE, P).transpose(1, 2, 0, 3)
        return _block_diag_tiles(t)

    w_b =jnp.stack([b_tiles(bb_re), b_tiles(bb_im)]).astype(BF16)

    def c_tiles(cc):
        t = cc.reshape(ngt, GROUP_TILE, C, P).transpose(0, 1, 3, 2)
        return _block_diag_tiles(t)

    w_c = jnp.stack([c_tiles(c_re[layer]), -c_tiles(c_im[layer])]).astype(BF16)

    d2 = d[layer].reshape(1, u_cols)
    z, fp_re, fp_im = _s5_prompt(uq, w_b, w_c, lam_re, lam_im, d2, n_seq=n_seq, seq_len=seq_len,
                                 out_rows=Mp)
    z, fs_re, fs_im = _s5_tail(uq, w_b, w_c, lam_re, lam_im, d2, s0_re, s0_im, z, row0=n_prompt)

    tn = _pick_tile(u_cols, 512, LANES)
    ssm_out = _dense(z, w_glu, (layer,), 0, u_cols, tm=tm, tn=tn,
                     extras=[(z, (tm, tn), lambda m, n, k, *pf: (m, n)),
                             (b_glu[layer].reshape(1, u_cols), (1, tn), lambda m, n, k, *pf: (0, n))],
                     epilogue=lambda p, e: e[0] * _sigmoid(p[0] + e[1]),
                     out_dtype=BF16, name="s5_glu")
    return ssm_out, (fp_re, fp_im), (fs_re, fs_im)


KV_COLS = 2 * N_KV_HEADS * HEAD_DIM
CHUNK_COLS = CMP_STRIDE * KV_COLS


KV_ROWS = 2 * N_KV_HEADS


def _compress_lead_trail(srcs, grid, pe, w1, rows_per_src, n_rows, prefetch=(), token_major=False,
                         name="cmp_lt"):
    ns = len(srcs)
    npf = len(prefetch)
    rows = ns * rows_per_src
    kd = CMP_STRIDE * HEAD_DIM

    def piece(xr, j, c):
        if token_major:
            return xr[pl.ds(j * KV_ROWS + c, rows_per_src, stride=CMP_STRIDE * KV_ROWS), :]
        return xr[:, j * KV_COLS + c * HEAD_DIM: j * KV_COLS + (c + 1) * HEAD_DIM]

    def body(*refs):
        x_refs = refs[npf:npf + ns]
        pe_ref, w_ref, o_ref = refs[npf + ns:]
        for kv in range(2):
            per_head = []
            for h in range(N_KV_HEADS):
                c = kv * N_KV_HEADS + h
                per_src = [jnp.concatenate([piece(xr, j, c) for j in range(CMP_STRIDE)], axis=1)
                           for xr in x_refs]
                per_head.append(per_src[0] if ns == 1 else jnp.concatenate(per_src, axis=0))
            xs = jnp.concatenate(per_head, axis=0)
            for lt in range(2):
                xv = (xs + pe_ref[kv, lt]).astype(BF16)
                r = jnp.dot(xv, w_ref[kv, lt].astype(BF16), preferred_element_type=F32)
                for h in range(N_KV_HEADS):
                    o_ref[kv, lt, h] = r[h * rows:(h + 1) * rows]

    in_specs = [pl.BlockSpec(bs, f) for _, bs, f in srcs]
    in_specs += [pl.BlockSpec((2, 2, 1, kd), lambda *a: (0, 0, 0, 0)),
                 pl.BlockSpec((2, 2, kd, HEAD_DIM), lambda *a: (0, 0, 0, 0))]
    step_of = lambda *a: a[0] * grid[1] + a[1]
    return pl.pallas_call(
        body,
        grid_spec=pltpu.PrefetchScalarGridSpec(
            num_scalar_prefetch=npf, grid=grid, in_specs=in_specs,
            out_specs=pl.BlockSpec((2, 2, N_KV_HEADS, rows, HEAD_DIM),
                                   lambda *a: (0, 0, 0, step_of(*a), 0))),
        out_shape=jax.ShapeDtypeStruct((2, 2, N_KV_HEADS, n_rows, HEAD_DIM), F32),
        compiler_params=pltpu.CompilerParams(
            dimension_semantics=("parallel", "arbitrary"), vmem_limit_bytes=48 << 20),
        name=name,
    )(*prefetch, *[a for a, _, _ in srcs], pe, w1)


def _compress_finish(lt, n_seq, n_chunk, b1, w2, name):
    lead = lt[:, 0]
    trail = lt[:, 1].reshape(2, N_KV_HEADS, n_seq, n_chunk, HEAD_DIM)
    trail = jnp.roll(trail, -1, axis=3).reshape(lead.shape)
    rows = N_KV_HEADS * n_seq * n_chunk
    lead = lead.reshape(2, rows, HEAD_DIM)
    trail = trail.reshape(2, rows, HEAD_DIM)
    rt = _pick_tile(rows, 1024, SUBLANES_F32)

    def body(a_ref, t_ref, b_ref, w_ref, o_ref):
        hid = _gelu_tanh(a_ref[...] + t_ref[...] + b_ref[...])
        o_ref[...] = jnp.dot(hid.astype(BF16), w_ref[...].astype(BF16), preferred_element_type=F32)

    out = pl.pallas_call(
        body, grid=(2, rows // rt),
        in_specs=[pl.BlockSpec((None, rt, HEAD_DIM), lambda kv, r: (kv, r, 0)),
                  pl.BlockSpec((None, rt, HEAD_DIM), lambda kv, r: (kv, r, 0)),
                  pl.BlockSpec((None, 1, HEAD_DIM), lambda kv, r: (kv, 0, 0)),
                  pl.BlockSpec((None, HEAD_DIM, HEAD_DIM), lambda kv, r: (kv, 0, 0))],
        out_specs=pl.BlockSpec((None, rt, HEAD_DIM), lambda kv, r: (kv, r, 0)),
        out_shape=jax.ShapeDtypeStruct((2, rows, HEAD_DIM), F32),
        compiler_params=pltpu.CompilerParams(dimension_semantics=("parallel", "parallel")),
        name=name,
    )(lead, trail, b1.reshape(2, 1, HEAD_DIM), w2)
    return out.reshape(2, N_KV_HEADS, n_seq, n_chunk, HEAD_DIM)


def _overlap_matrix(n_cmp, n_blk, rows, cols):
    start = np.arange(rows)[:, None] * CMP_STRIDE
    js = np.arange(cols)[None, :] * SEL_BLOCK
    ov = (start < js + SEL_BLOCK) & (start + CMP_BLOCK > js)
    ov &= (np.arange(rows)[:, None] < n_cmp) & (np.arange(cols)[None, :] < n_blk)
    return jnp.asarray(ov.astype(np.float32))


def _masked_softmax(s, valid):
    s = jnp.where(valid, s, NEG_INF)
    p = jnp.exp(s - jnp.max(s, axis=-1, keepdims=True)) * jnp.where(valid, 1.0, 0.0)
    return p * (1.0 / jnp.maximum(jnp.sum(p, axis=-1, keepdims=True), 1e-30))


def _softmax_dot(s, valid, v):
    hg, r, n = s.shape
    s = jnp.where(valid, s, NEG_INF)
    p = jnp.exp(s - jnp.max(s, axis=-1, keepdims=True))
    den = jnp.sum(p, axis=-1, keepdims=True)
    o = jnp.dot(p.reshape(hg * r, n).astype(BF16), v, preferred_element_type=F32)
    return o.reshape(hg, r, v.shape[1]) * (1.0 / den)


_NT = (((1,), (1,)), ((), ()))


def _nsa_prompt(uq, q_blk0, glg, cmp_kv, kvs, kvw, *, n_seq, seq_len, out_rows):
    T = seq_len
    QB = Q_BLOCK
    HG = Q_PER_KV
    HD = HEAD_DIM
    nqb = T // QB
    ncp = cmp_kv.shape[3]
    n_cmp = T // CMP_STRIDE - 1
    n_blk = -(-T // SEL_BLOCK)
    n_pick = min(N_SEL, n_blk)
    assert T % QB == 0 and T % SEL_BLOCK == 0 and n_blk <= LANES and ncp <= LANES
    assert n_pick > N_LOCAL_SEL
    CH = min(512, T)
    assert T % CH == 0
    WB = min(WINDOW + QB, T)
    scale = HD ** -0.5
    blk_of_key = np.arange(T) // SEL_BLOCK
    expand = (np.arange(LANES)[:, None] == blk_of_key[None, :]).astype(np.float32)
    expand = jnp.asarray(expand.reshape(LANES, T // CH, CH).transpose(1, 0, 2), dtype=BF16)
    ov = _overlap_matrix(n_cmp, n_blk, ncp, LANES)

    perm = np.zeros((N_KV_HEADS, LANES, LANES), np.float32)
    for hh in range(N_KV_HEADS):
        for g in range(HG):
            for j in range(3):
                perm[hh, (hh * HG + g) * 3 + j, j * HG + g] = 1.0
    perm = jnp.asarray(perm)

    def body(q_ref, gl_ref, perm_ref, kc_ref, vc_ref, ks_ref, vs_ref, kw_ref, vw_ref, e_ref, ov_ref, o_ref,
             m_ref, l_ref, acc_ref):
        i = pl.program_id(2)
        start = i * QB
        q = q_ref[...]
        q_st = jnp.concatenate([q[:, g * HD:(g + 1) * HD] for g in range(HG)], axis=0).astype(BF16)
        qpos = start + lax.broadcasted_iota(jnp.int32, (QB, 1), 0)

        s = lax.dot_general(q_st, kc_ref[...].astype(BF16), _NT, preferred_element_type=F32) * scale
        n_idx = lax.broadcasted_iota(jnp.int32, (QB, ncp), 1)
        cvalid = (n_idx * CMP_STRIDE + (CMP_BLOCK - 1) <= qpos) & (n_idx < n_cmp)
        p_c = _masked_softmax(s.reshape(HG, QB, ncp), cvalid)
        o_c = jnp.dot(p_c.reshape(HG * QB, ncp).astype(BF16), vc_ref[...].astype(BF16),
                      preferred_element_type=F32).reshape(HG, QB, HD)

        imp = jnp.dot(jnp.sum(p_c, axis=0), ov_ref[...], preferred_element_type=F32,
                      precision=lax.Precision.HIGHEST)
        blk = lax.broadcasted_iota(jnp.int32, (QB, LANES), 1)
        cur = qpos // SEL_BLOCK
        forced = (blk == 0) | ((blk <= cur) & (blk > cur - N_LOCAL_SEL))
        score = jnp.where(blk > cur, -BIG, jnp.where(forced, BIG, imp))
        rank = jnp.zeros((QB, LANES), F32)
        for j2 in range(n_blk):
            col = score[:, j2:j2 + 1]
            ahead = (col > score) | ((col == score) & (blk > j2))
            rank = rank + jnp.where(ahead, 1.0, 0.0)
        sel = jnp.where((rank < n_pick) & (blk < n_blk), 1.0, 0.0).astype(BF16)

        m_ref[...] = jnp.full(m_ref.shape, NEG_INF, F32)
        l_ref[...] = jnp.zeros(l_ref.shape, F32)
        acc_ref[...] = jnp.zeros(acc_ref.shape, F32)

        def chunk(c, carry):
            k0 = pl.multiple_of(c * CH, CH)
            kch = ks_ref[pl.ds(k0, CH), :].astype(BF16)
            vch = vs_ref[pl.ds(k0, CH), :].astype(BF16)
            sc = lax.dot_general(q_st, kch, _NT, preferred_element_type=F32) * scale
            picked = jnp.dot(sel, e_ref[c], preferred_element_type=F32)
            kpos = k0 + lax.broadcasted_iota(jnp.int32, (QB, CH), 1)
            valid = (picked > 0.5) & (kpos <= qpos)
            s3 = jnp.where(valid, sc.reshape(HG, QB, CH), NEG_INF)
            m_old = m_ref[...]
            m_new = jnp.maximum(m_old, jnp.max(s3, axis=-1, keepdims=True))
            p = jnp.exp(s3 - m_new)
            alpha = jnp.exp(m_old - m_new)
            l_ref[...] = alpha * l_ref[...] + jnp.sum(p, axis=-1, keepdims=True)
            pv = jnp.dot(p.reshape(HG * QB, CH).astype(BF16), vch, preferred_element_type=F32)
            acc_ref[...] = alpha * acc_ref[...] + pv.reshape(HG, QB, HD)
            m_ref[...] = m_new
            return carry

        lax.fori_loop(0, (start + QB + CH - 1) // CH, chunk, 0)
        o_s = acc_ref[...] * (1.0 / l_ref[...])

        w0 = pl.multiple_of(jnp.minimum(jnp.maximum(start - WINDOW, 0), T - WB), QB)
        sw = lax.dot_general(q_st, kw_ref[pl.ds(w0, WB), :].astype(BF16), _NT,
                             preferred_element_type=F32) * scale
        dist = qpos - (w0 + lax.broadcasted_iota(jnp.int32, (QB, WB), 1))
        o_w = _softmax_dot(sw.reshape(HG, QB, WB), (dist >= 0) & (dist < WINDOW),
                           vw_ref[pl.ds(w0, WB), :].astype(BF16))

        gate = _sigmoid(jnp.dot(gl_ref[...], perm_ref[...], preferred_element_type=F32,
                                precision=lax.Precision.HIGHEST))
        for g in range(HG):
            og = (gate[:, g:g + 1] * o_c[g] + gate[:, HG + g:HG + g + 1] * o_s[g]
                  + gate[:, 2 * HG + g:2 * HG + g + 1] * o_w[g])
            o_ref[:, g * HD:(g + 1) * HD] = og.astype(o_ref.dtype)

    row = lambda b, h, i: b * nqb + i
    kv_spec = lambda kv: pl.BlockSpec((T, HD), lambda b, h, i: (b, kv * N_KV_HEADS + h))
    return pl.pallas_call(
        body, grid=(n_seq, N_KV_HEADS, nqb),
        in_specs=[pl.BlockSpec((QB, HG * HD), lambda b, h, i: (row(b, h, i), q_blk0 + h)),
                  pl.BlockSpec((QB, LANES), lambda b, h, i: (row(b, h, i), 0)),
                  pl.BlockSpec((None, LANES, LANES), lambda b, h, i: (h, 0, 0)),
                  pl.BlockSpec((None, None, None, ncp, HD), lambda b, h, i: (0, h, b, 0, 0)),
                  pl.BlockSpec((None, None, None, ncp, HD), lambda b, h, i: (1, h, b, 0, 0)),
                  kv_spec(0), kv_spec(1), kv_spec(0), kv_spec(1),
                  pl.BlockSpec((T // CH, LANES, CH), lambda b, h, i: (0, 0, 0)),
                  pl.BlockSpec((ncp, LANES), lambda b, h, i: (0, 0))],
        out_specs=pl.BlockSpec((QB, HG * HD), lambda b, h, i: (row(b, h, i), h)),
        out_shape=jax.ShapeDtypeStruct((out_rows, N_HEADS * HD), BF16),
        scratch_shapes=[pltpu.VMEM((HG, QB, 1), F32), pltpu.VMEM((HG, QB, 1), F32),
                        pltpu.VMEM((HG, QB, HD), F32)],
        compiler_params=pltpu.CompilerParams(
            dimension_semantics=("parallel", "parallel", "arbitrary"), vmem_limit_bytes=48 << 20),
        name="nsa_prompt",
    )(uq, glg, perm, cmp_kv, cmp_kv, kvs, kvs, kvw, kvw, expand, ov)


def _nsa_sample_select(q_s, cmp_kv, past_len):
    S = q_s.shape[0]
    HG, HD = Q_PER_KV, HEAD_DIM
    nc = cmp_kv.shape[3]
    n_cmp = nc - 1
    q_pos = past_len
    n_blk = -(-(past_len + 1) // SEL_BLOCK)
    nbp = -(-n_blk // LANES) * LANES
    n_pick = min(N_SEL, n_blk)
    cur = q_pos // SEL_BLOCK
    scale = HD ** -0.5
    ov = _overlap_matrix(n_cmp, n_blk, nc, nbp)

    def body(q_ref, kc_ref, vc_ref, ov_ref, oc_ref, idx_ref):
        q = q_ref[...].astype(BF16)
        s = lax.dot_general(q, kc_ref[...].astype(BF16), _NT, preferred_element_type=F32) * scale
        n_idx = lax.broadcasted_iota(jnp.int32, (HG, nc), 1)
        cvalid = (n_idx * CMP_STRIDE + (CMP_BLOCK - 1) <= q_pos) & (n_idx < n_cmp)
        p_c = _masked_softmax(s, cvalid)
        oc_ref[...] = jnp.dot(p_c.astype(BF16), vc_ref[...].astype(BF16), preferred_element_type=F32)
        imp = jnp.sum(jnp.dot(p_c, ov_ref[...], preferred_element_type=F32,
                              precision=lax.Precision.HIGHEST), axis=0, keepdims=True)
        blk = lax.broadcasted_iota(jnp.int32, (1, nbp), 1)
        forced = (blk == 0) | ((blk <= cur) & (blk > cur - N_LOCAL_SEL))
        score = jnp.where(blk > cur, -BIG, jnp.where(forced, BIG, imp))
        col = jnp.broadcast_to(score, (LANES, nbp)).T[:, 0:1]
        jp = lax.broadcasted_iota(jnp.int32, (nbp, nbp), 0)
        jj = lax.broadcasted_iota(jnp.int32, (nbp, nbp), 1)
        ahead = ((col > score) | ((col == score) & (jp < jj))) & (jp < n_blk)
        rank = jnp.sum(jnp.where(ahead, 1.0, 0.0), axis=0, keepdims=True)
        lane = lax.broadcasted_iota(jnp.int32, (1, LANES), 1)
        out = jnp.zeros((1, LANES), jnp.int32)
        for r in range(n_pick):
            hit = (rank == float(r)) & (blk < n_blk)
            idx_r = jnp.sum(jnp.where(hit, blk, 0), axis=1, keepdims=True)
            out = jnp.where(lane == r, idx_r, out)
        idx_ref[...] = jnp.broadcast_to(out, (SUBLANES_F32, LANES))

    o_c, idx = pl.pallas_call(
        body, grid=(S, N_KV_HEADS),
        in_specs=[pl.BlockSpec((None, None, HG, HD), lambda b, h: (b, h, 0, 0)),
                  pl.BlockSpec((None, None, None, nc, HD), lambda b, h: (0, h, b, 0, 0)),
                  pl.BlockSpec((None, None, None, nc, HD), lambda b, h: (1, h, b, 0, 0)),
                  pl.BlockSpec((nc, nbp), lambda b, h: (0, 0))],
        out_specs=[pl.BlockSpec((None, None, HG, HD), lambda b, h: (b, h, 0, 0)),
                   pl.BlockSpec((None, None, SUBLANES_F32, LANES), lambda b, h: (b, h, 0, 0))],
        out_shape=(jax.ShapeDtypeStruct((S, N_KV_HEADS, HG, HD), F32),
                   jax.ShapeDtypeStruct((S, N_KV_HEADS, SUBLANES_F32, LANES), jnp.int32)),
        compiler_params=pltpu.CompilerParams(dimension_semantics=("parallel", "parallel")),
        name="nsa_sample_select",
    )(q_s, cmp_kv, cmp_kv, ov)
    return o_c, idx[:, :, 0, :n_pick], n_pick


def _nsa_sample_attend(q_s, o_c, gate_s, src, live, cache_sel_v, layer, new_sel, cache_win_v, new_win,
                       n_pick):
    S = q_s.shape[0]
    HG, HD = Q_PER_KV, HEAD_DIM
    wb = cache_win_v.shape[2]
    scale = HD ** -0.5
    nkeys = n_pick * SEL_BLOCK

    def body(src_ref, live_ref, q_ref, oc_ref, g_ref, *rest):
        blk_refs = rest[:n_pick]
        kn_ref, vn_ref, kw_ref, vw_ref, kwn_ref, vwn_ref, o_ref = rest[n_pick:]
        b, h = pl.program_id(0), pl.program_id(1)

        def rows_of(kv):
            per_head = [jnp.concatenate(
                [r[pl.ds(kv * N_KV_HEADS + hh, SEL_BLOCK, stride=KV_ROWS), :] for r in blk_refs], axis=0)
                for hh in range(N_KV_HEADS)]
            out = per_head[0]
            for hh in range(1, N_KV_HEADS):
                out = jnp.where(h == hh, per_head[hh], out)
            return out.astype(BF16)

        base = (b * N_KV_HEADS + h) * n_pick
        q = q_ref[...]
        qb = q.astype(BF16)

        def with_new_key(s, valid, s_new, v_old, v_new):
            sm = jnp.where(valid, s, NEG_INF)
            m = jnp.maximum(jnp.max(sm, axis=-1, keepdims=True), s_new)
            p = jnp.exp(sm - m) * jnp.where(valid, 1.0, 0.0)
            p_new = jnp.exp(s_new - m)
            den = jnp.sum(p, axis=-1, keepdims=True) + p_new
            return (jnp.dot(p.astype(BF16), v_old, preferred_element_type=F32) + p_new * v_new) / den

        keys = rows_of(0)
        vals = rows_of(1)
        s = lax.dot_general(qb, keys, _NT, preferred_element_type=F32) * scale
        lane = lax.broadcasted_iota(jnp.int32, (1, nkeys), 1)
        livef = jnp.zeros((1, nkeys), F32)
        for r in range(n_pick):
            livef = jnp.where((lane >= r * SEL_BLOCK) & (lane < (r + 1) * SEL_BLOCK),
                              live_ref[base + r].astype(F32), livef)
        s_new = jnp.sum(q * kn_ref[...], axis=-1, keepdims=True) * scale
        o_s = with_new_key(s, livef > 0.5, s_new, vals, vn_ref[...])

        sw = lax.dot_general(qb, kw_ref[...].astype(BF16), _NT, preferred_element_type=F32) * scale
        wrow = lax.broadcasted_iota(jnp.int32, (1, wb), 1)
        sw_new = jnp.sum(q * kwn_ref[...], axis=-1, keepdims=True) * scale
        o_w = with_new_key(sw, wrow > wb - WINDOW, sw_new, vw_ref[...].astype(BF16), vwn_ref[...])

        gate = _sigmoid(g_ref[...])
        o_ref[...] = gate[:, 0:1] * oc_ref[...] + gate[:, 1:2] * o_s + gate[:, 2:3] * o_w

    bpp = cache_sel_v.shape[2] // (SEL_BLOCK * KV_ROWS)

    def blk_spec(r):
        def index(b, h, src_ref, live_ref):
            s = src_ref[(b * N_KV_HEADS + h) * n_pick + r]
            return (layer, s // bpp, s % bpp, 0)
        return pl.BlockSpec((None, None, SEL_BLOCK * KV_ROWS, HD), index)

    qspec = pl.BlockSpec((None, None, HG, HD), lambda b, h, *_: (b, h, 0, 0))
    gspec = pl.BlockSpec((None, None, HG, LANES), lambda b, h, *_: (b, h, 0, 0))
    new_spec = lambda kv: pl.BlockSpec((None, None, 1, HD), lambda b, h, *_: (b, kv * N_KV_HEADS + h, 0, 0))
    win_spec = lambda kv: pl.BlockSpec((None, None, wb, HD), lambda b, h, *_: (layer, b, 0, kv * N_KV_HEADS + h))
    in_specs = ([qspec, qspec, gspec] + [blk_spec(r) for r in range(n_pick)]
                + [new_spec(0), new_spec(1), win_spec(0), win_spec(1), new_spec(0), new_spec(1)])
    return pl.pallas_call(
        body,
        grid_spec=pltpu.PrefetchScalarGridSpec(
            num_scalar_prefetch=2, grid=(S, N_KV_HEADS), in_specs=in_specs,
            out_specs=pl.BlockSpec((None, None, HG, HD), lambda b, h, *_: (b, h, 0, 0))),
        out_shape=jax.ShapeDtypeStruct((S, N_KV_HEADS, HG, HD), F32),
        compiler_params=pltpu.CompilerParams(dimension_semantics=("parallel", "parallel")),
        name="nsa_sample_attend",
    )(src, live, q_s, o_c, gate_s, *([cache_sel_v] * n_pick), new_sel, new_sel,
      cache_win_v, cache_win_v, new_win, new_win)


def _nsa_sample(q_s, gate_s, new_sel, new_win, cache_cmp, cache_sel, cache_win, page_table, layer,
                pe, w1, b1, w2):
    S, n_pages = page_table.shape
    depth, n_pool, page = cache_cmp.shape[:3]
    past_len = n_pages * page
    assert page % CMP_STRIDE == 0 and page % SEL_BLOCK == 0
    cpp = page // CMP_STRIDE
    bpp = page // SEL_BLOCK
    pps = _pick_tile(n_pages, 16, 1)
    cache_v = cache_cmp.reshape(depth, n_pool, page * KV_ROWS, HEAD_DIM)
    pt_flat = page_table.reshape(-1)
    srcs = [(cache_v, (None, None, page * KV_ROWS, HEAD_DIM),
             (lambda b, s, pt, p=p: (layer, pt[b * n_pages + s * pps + p], 0, 0))) for p in range(pps)]
    nc = n_pages * cpp
    lt = _compress_lead_trail(srcs, (S, n_pages // pps), pe, w1, cpp, S * nc, prefetch=(pt_flat,),
                              token_major=True, name="cmp_lt_sample")
    cmp_kv = _compress_finish(lt, S, nc, b1, w2, "cmp_finish_sample")
    o_c, idx, n_pick = _nsa_sample_select(q_s, cmp_kv, past_len)

    n_past_blk = past_len // SEL_BLOCK
    idc = jnp.minimum(idx, n_past_blk - 1)
    pages = jnp.take_along_axis(page_table[:, None, :], idc // bpp, axis=2)
    src = (pages * bpp + idc % bpp).astype(jnp.int32).reshape(-1)
    live = (idx < n_past_blk).astype(jnp.int32).reshape(-1)
    wb = cache_win.shape[2]
    out = _nsa_sample_attend(
        q_s, o_c, gate_s, src, live, cache_sel.reshape(depth, n_pool, page * KV_ROWS, HEAD_DIM), layer,
        new_sel, cache_win.reshape(depth, S, wb, KV_COLS), new_win, n_pick)
    return out.reshape(S, N_HEADS * HEAD_DIM)


MOE_ROW_TILE = 512


def _moe_route(x, g, w_router):
    M, D = x.shape
    E = w_router.shape[1]
    tr = _pick_tile(M, 256, SUBLANES_BF16)
    wr = jnp.pad(w_router, ((0, 0), (0, LANES - E)))

    def body(x_ref, g_ref, w_ref, h_ref, r_ref):
        xf = x_ref[...]
        ms = jnp.mean(xf * xf, axis=-1, keepdims=True)
        hf = xf * lax.rsqrt(ms + RMS_EPS) * g_ref[...]
        bits = lax.bitcast_convert_type(hf.astype(BF16).astype(F32), jnp.uint32)
        h_ref[...] = (bits[:, :D // 2] >> 16) | (bits[:, D // 2:] & jnp.uint32(0xFFFF0000))
        logits = jnp.dot(hf, w_ref[...], preferred_element_type=F32, precision=lax.Precision.HIGHEST)
        lane = lax.broadcasted_iota(jnp.int32, (tr, LANES), 1)
        logits = jnp.where(lane < E, logits, -jnp.inf)
        v1 = jnp.max(logits, axis=-1, keepdims=True)
        i1 = jnp.min(jnp.where(logits == v1, lane, LANES), axis=-1, keepdims=True)
        rest = jnp.where(lane == i1, -jnp.inf, logits)
        v2 = jnp.max(rest, axis=-1, keepdims=True)
        i2 = jnp.min(jnp.where(rest == v2, lane, LANES), axis=-1, keepdims=True)
        e2 = jnp.exp(v2 - v1)
        g1 = 1.0 / (1.0 + e2)
        out = jnp.where(lane == 0, i1.astype(F32), jnp.where(lane == 1, i2.astype(F32),
                        jnp.where(lane == 2, g1, jnp.where(lane == 3, e2 * g1, 0.0))))
        r_ref[...] = out

    return pl.pallas_call(
        body, grid=(M // tr,),
        in_specs=[pl.BlockSpec((tr, D), lambda i: (i, 0)), pl.BlockSpec((1, D), lambda i: (0, 0)),
                  pl.BlockSpec((D, LANES), lambda i: (0, 0))],
        out_specs=[pl.BlockSpec((tr, D // 2), lambda i: (i, 0)), pl.BlockSpec((tr, LANES), lambda i: (i, 0))],
        out_shape=(jax.ShapeDtypeStruct((M, D // 2), jnp.uint32), jax.ShapeDtypeStruct((M, LANES), F32)),
        compiler_params=pltpu.CompilerParams(dimension_semantics=("parallel",)),
        name="moe_route",
    )(x, g.reshape(1, D), wr)


MOE_GROUP_TILES = 5


def _moe_down(hmid, wd, li, gate_b, tile_expert, n_used, tm):
    rows, F = hmid.shape
    D = wd.shape[-1]
    mt = MOE_GROUP_TILES
    n_tiles = rows // tm
    tk = _pick_tile(F, 2048, LANES)
    tn = _pick_tile(D, 1024, LANES)
    nk = F // tk
    n_slots = n_tiles * nk

    t = jnp.arange(n_tiles, dtype=jnp.int32)
    used = t < n_used
    same = (tile_expert[:, None] == tile_expert[None, :]) & used[None, :]
    first = jnp.min(jnp.where(same, t[None, :], n_tiles), axis=1)
    cnt = jnp.sum(same.astype(jnp.int32), axis=1)
    j = t - first
    m_local = jnp.where(used, j % mt, 0)
    g_first = t - m_local
    g_cnt = jnp.where(used, jnp.minimum(mt, cnt - (j - m_local)), 1)
    kk = jnp.arange(nk, dtype=jnp.int32)
    pos = (nk * g_first[:, None] + kk[None, :] * g_cnt[:, None] + m_local[:, None]).reshape(-1)
    n_items = (nk * n_used).astype(jnp.int32)
    order = jnp.minimum(jnp.arange(n_slots, dtype=jnp.int32), n_items - 1)
    table = lambda v: jnp.zeros((n_slots,), jnp.int32).at[pos].set(v.reshape(-1))[order]
    it_tile = table(jnp.broadcast_to(t[:, None], (n_tiles, nk)))
    it_k = table(jnp.broadcast_to(kk[None, :], (n_tiles, nk)))
    it_slot = table(jnp.broadcast_to(m_local[:, None], (n_tiles, nk)))
    it_exp = tile_expert[it_tile]
    it_out = jnp.where(it_k == nk - 1, it_tile, it_tile - it_slot)

    def body(tile_ref, k_ref, slot_ref, exp_ref, out_ref, n_ref, h_ref, w_ref, g_ref, o_ref, acc_ref):
        i = pl.program_id(1)

        @pl.when(i < n_ref[0])
        def _():
            k = k_ref[i]
            slot = slot_ref[i]
            p = jnp.dot(h_ref[...], w_ref[...].astype(BF16), preferred_element_type=F32)

            @pl.when(k == 0)
            def _():
                acc_ref[slot] = p

            @pl.when(k > 0)
            def _():
                acc_ref[slot] += p

            @pl.when(k == nk - 1)
            def _():
                o_ref[...] = acc_ref[slot] * g_ref[...][:, 0:1]

    est = (2 * _nbytes((tm, tk), BF16) + 2 * _nbytes((tk, tn), F32) + _nbytes((tk, tn), BF16)
           + (mt + 3) * _nbytes((tm, tn), F32))
    return pl.pallas_call(
        body,
        grid_spec=pltpu.PrefetchScalarGridSpec(
            num_scalar_prefetch=6, grid=(D // tn, n_slots),
            in_specs=[pl.BlockSpec((tm, tk), lambda n, i, tr, kr, sr, er, orr, nr: (tr[i], kr[i])),
                      pl.BlockSpec((None, None, tk, tn),
                                   lambda n, i, tr, kr, sr, er, orr, nr: (li, er[i], kr[i], n)),
                      pl.BlockSpec((tm, LANES), lambda n, i, tr, kr, sr, er, orr, nr: (tr[i], 0))],
            out_specs=pl.BlockSpec((tm, tn), lambda n, i, tr, kr, sr, er, orr, nr: (orr[i], n)),
            scratch_shapes=[pltpu.VMEM((mt, tm, tn), F32)]),
        out_shape=jax.ShapeDtypeStruct((rows, D), F32),
        compiler_params=pltpu.CompilerParams(
            dimension_semantics=("arbitrary", "arbitrary"), vmem_limit_bytes=_vmem_limit(est)),
        name="moe_down",
    )(it_tile, it_k, it_slot, it_exp, it_out, n_items.reshape(1), hmid, wd, gate_b)


def _moe(x_all, n_real, g, w_router, wg, wu, wd, li):
    Mp, D = x_all.shape
    E, _, F = wg.shape[1:]
    tm = MOE_ROW_TILE
    h, route = _moe_route(x_all, g, w_router[li])

    r = route[:n_real]
    e_flat = r[:, :TOP_K].astype(jnp.int32).reshape(-1)
    gates = r[:, TOP_K:2 * TOP_K].reshape(-1)
    n_pairs = n_real * TOP_K
    onehot = (e_flat[:, None] == jnp.arange(E)[None, :]).astype(jnp.int32)
    counts = jnp.sum(onehot, axis=0)
    within = jnp.take_along_axis(jnp.cumsum(onehot, axis=0) - onehot, e_flat[:, None], axis=1)[:, 0]
    padded = (counts + tm - 1) // tm * tm
    ends = jnp.cumsum(padded)
    dest = (ends - padded)[e_flat] + within
    n_tiles = -(-n_pairs // tm) + E
    rows = n_tiles * tm
    src_tok = jnp.zeros((rows,), jnp.int32).at[dest].set(jnp.arange(n_pairs, dtype=jnp.int32) // TOP_K)
    row_gate = jnp.zeros((rows,), F32).at[dest].set(gates)
    n_used = (ends[-1] // tm).astype(jnp.int32)
    tile_ids = jnp.minimum(jnp.arange(n_tiles, dtype=jnp.int32), n_used - 1)
    tile_expert = jnp.minimum(jnp.searchsorted(ends, tile_ids * tm, side="right"), E - 1).astype(jnp.int32)
    pf = (tile_expert, n_used.reshape(1))
    xs = jnp.take(h, src_tok, axis=0, mode="clip")

    tn = _pick_tile(F, 512, LANES)
    wmap = lambda m, n, k, te, nu: (li, te[m], 0, n)
    hmid = _fused_matmul(
        xs=[(xs, lambda m, n, k, *pf: (m, 0))],
        dots=[(0, wg, (None, None, D, tn), wmap), (0, wu, (None, None, D, tn), wmap)],
        epilogue=lambda p, e: p[0] * _sigmoid(p[0]) * p[1],
        M=rows, N=F, K=D, tm=tm, tn=tn, tk=D, out_dtype=BF16, order="nm", prefetch=pf,
        n_used_tiles=True, x_packed=True, name="moe_up")

    gate_b = jnp.broadcast_to(row_gate[:, None], (rows, LANES))
    y = _moe_down(hmid, wd, li, gate_b, tile_expert, n_used, tm)

    d2 = dest.reshape(n_real, TOP_K)
    y_tok = jnp.take(y, d2[:, 0], axis=0, mode="clip")
    for k in range(1, TOP_K):
        y_tok = y_tok + jnp.take(y, d2[:, k], axis=0, mode="clip")
    return x_all.at[:n_real].add(y_tok)


def kernel(x_prompt, x_sample, cache_cmp, cache_sel, cache_win, state_ssm_re, state_ssm_im, page_table,
           norm_mix, w_in, ssm_a_re, ssm_a_im, ssm_b_re, ssm_b_im, ssm_c_re, ssm_c_im, ssm_d, ssm_log_dt,
           ssm_w_glu, ssm_b_glu, cmp_pe, cmp_w1, cmp_b1, cmp_w2, w_br_ssm, w_br_nsa, w_out,
           norm_ffn, ffn_w_gate, ffn_w_up, ffn_w_down, moe_router, moe_w_gate, moe_w_up, moe_w_down,
           norm_final):
    B, T, D = x_prompt.shape
    S = x_sample.shape[0]
    assert x_sample.shape[1] == 1
    depth = w_in.shape[0]
    n_prompt = B * T
    n_real = n_prompt + S
    Mp = -(-n_real // ROW_PAD) * ROW_PAD
    n_tail = Mp - n_prompt
    d_ssm = ssm_d.shape[1]
    d_nsa = N_HEADS * HEAD_DIM
    G, P = ssm_a_re.shape[1], ssm_a_re.shape[2]
    o_kv = d_ssm + d_nsa
    o_gl = o_kv + 3 * KV_COLS
    o_g = o_gl + 3 * N_HEADS
    assert d_ssm % (Q_PER_KV * HEAD_DIM) == 0 and w_in.shape[2] == o_g + 2 * D

    tm_big = _pick_tile(Mp, 1376, SUBLANES_BF16)
    tm_mid = _pick_tile(Mp, 688, SUBLANES_BF16)
    tn = 512

    x_all = jnp.concatenate([x_prompt.reshape(n_prompt, D), x_sample.reshape(S, D),
                             jnp.zeros((Mp - n_real, D), F32)], axis=0)
    w_in_t = jnp.swapaxes(w_in, 1, 2)
    pad_tail = lambda a: jnp.pad(a.reshape(S, -1), ((0, n_tail - S), (0, 0)))

    outs = {k: [] for k in ("cmp_p", "sel_p", "win_p", "sre_p", "sim_p",
                            "cmp_s", "sel_s", "win_s", "sre_s", "sim_s")}
    for l in range(depth):
        h = _rmsnorm(x_all, norm_mix[l], BF16, "norm_mix")
        uq = _dense(h, w_in_t, (l,), 0, o_kv, tm=tm_big, tn=tn, w_t=True, name="in_uq")
        kvc, kvs, kvw = [_dense(h, w_in_t, (l,), o_kv + i * KV_COLS, KV_COLS, tm=tm_big, tn=KV_COLS,
                                w_t=True, name=f"in_kv{i}") for i in range(3)]
        glg = _dense(h, w_in_t, (l,), o_gl, LANES, tm=tm_big, tn=LANES, w_t=True, name="in_gate_logits")
        gates = _dense_t_unaligned(h, w_in_t, (l,), o_g, 2 * D, tm=tm_big, tn=256, epilogue=_sigmoid,
                                   out_dtype=F32, name="in_merge_gates")

        s0 = [pad_tail(s[l]) for s in (state_ssm_re, state_ssm_im)]
        ssm_params = (ssm_a_re, ssm_a_im, ssm_b_re, ssm_b_im, ssm_c_re, ssm_c_im, ssm_d, ssm_log_dt,
                      ssm_w_glu, ssm_b_glu)
        ssm_out, fin_p, fin_s = _s5_branch(uq, d_ssm, s0[0], s0[1], n_prompt, B, T, ssm_params, l, tm_mid)

        kd = CMP_STRIDE * HEAD_DIM
        pe = cmp_pe[l].reshape(2, 2, 1, kd)
        w1 = cmp_w1[l].reshape(2, 2, kd, HEAD_DIM)
        n_chunk = T // CMP_STRIDE
        rb = _pick_tile(n_prompt // CMP_STRIDE, 128, SUBLANES_F32)
        lt = _compress_lead_trail(
            [(kvc.reshape(Mp // CMP_STRIDE, CHUNK_COLS), (rb, CHUNK_COLS), lambda a, s: (s, 0))],
            (1, n_prompt // CMP_STRIDE // rb), pe, w1, rb, n_prompt // CMP_STRIDE, name="cmp_lt_prompt")
        cmp_kv = _compress_finish(lt, B, n_chunk, cmp_b1[l], cmp_w2[l], "cmp_finish_prompt")
        if n_chunk < LANES:
            cmp_kv = jnp.pad(cmp_kv, ((0, 0), (0, 0), (0, 0), (0, LANES - n_chunk), (0, 0)))
        q_blk0 = d_ssm // (Q_PER_KV * HEAD_DIM)
        nsa_out = _nsa_prompt(uq, q_blk0, glg, cmp_kv, kvs, kvw, n_seq=B, seq_len=T, out_rows=Mp)

        tail = lambda a, c0, c1: a[n_prompt:n_real, c0:c1]
        q_s = tail(uq, d_ssm, o_kv).reshape(S, N_KV_HEADS, Q_PER_KV, HEAD_DIM)
        gl_s = tail(glg, 0, 3 * N_HEADS).reshape(S, N_KV_HEADS, Q_PER_KV, 3)
        gate_s = jnp.pad(gl_s, ((0, 0), (0, 0), (0, 0), (0, LANES - 3)))
        new_sel = tail(kvs, 0, KV_COLS).reshape(S, 2 * N_KV_HEADS, 1, HEAD_DIM)
        new_win = tail(kvw, 0, KV_COLS).reshape(S, 2 * N_KV_HEADS, 1, HEAD_DIM)
        nsa_s = _nsa_sample(q_s, gate_s, new_sel, new_win, cache_cmp, cache_sel, cache_win, page_table, l,
                            pe, w1, cmp_b1[l], cmp_w2[l])
        nsa_out = lax.dynamic_update_slice(nsa_out, pad_tail(nsa_s).astype(BF16), (n_prompt, 0))

        ng = D // tn
        y = _fused_matmul(
            xs=[(ssm_out, lambda m, n, k, *pf: (m, 0)), (nsa_out, lambda m, n, k, *pf: (m, 0))],
            dots=[(0, w_br_ssm, (None, d_ssm, tn), lambda m, n, k, *pf: (l, 0, n)),
                  (1, w_br_nsa, (None, d_nsa, tn), lambda m, n, k, *pf: (l, 0, n))],
            extras=[(gates, (tm_mid, tn), lambda m, n, k, *pf: (m, n)),
                    (gates, (tm_mid, tn), lambda m, n, k, *pf: (m, ng + n))],
            epilogue=lambda p, e: e[0] * p[0] + e[1] * p[1],
            M=Mp, N=D, K=d_ssm, tm=tm_mid, tn=tn, tk=d_ssm, out_dtype=BF16, name="merge")
        x_all = _dense(y, w_out, (l,), 0, D, tm=tm_big, tn=tn,
                       extras=[(x_all, (tm_big, tn), lambda m, n, k, *pf: (m, n))],
                       epilogue=lambda p, e: e[0] + p[0], name="out_proj")

        kv5 = lambda a, rows, lead: a[rows].reshape(lead + (2, N_KV_HEADS, HEAD_DIM))
        outs["cmp_p"].append(kv5(kvc, slice(0, n_prompt), (B, T)))
        outs["sel_p"].append(kv5(kvs, slice(0, n_prompt), (B, T)))
        wp = min(WINDOW, T)
        outs["win_p"].append(kv5(kvw, slice(0, n_prompt), (B, T))[:, T - wp:])
        outs["sre_p"].append(fin_p[0].reshape(B, G, P))
        outs["sim_p"].append(fin_p[1].reshape(B, G, P))
        outs["cmp_s"].append(kv5(kvc, slice(n_prompt, n_real), (S, 1)))
        outs["sel_s"].append(kv5(kvs, slice(n_prompt, n_real), (S, 1)))
        win = jnp.concatenate([cache_win[l], kv5(kvw, slice(n_prompt, n_real), (S, 1))], axis=1)
        outs["win_s"].append(win[:, 1:])
        outs["sre_s"].append(fin_s[0][:S].reshape(S, G, P))
        outs["sim_s"].append(fin_s[1][:S].reshape(S, G, P))

        if l % 2 == 0:
            i = l // 2
            h2 = _rmsnorm(x_all, norm_ffn[l], BF16, "norm_ffn")
            f = ffn_w_gate.shape[2]
            tnf = _pick_tile(f, 256, LANES)
            wmap = lambda m, n, k, *pf: (i, 0, n)
            hmid = _fused_matmul(
                xs=[(h2, lambda m, n, k, *pf: (m, 0))],
                dots=[(0, ffn_w_gate, (None, D, tnf), wmap), (0, ffn_w_up, (None, D, tnf), wmap)],
                epilogue=lambda p, e: p[0] * _sigmoid(p[0]) * p[1],
                M=Mp, N=f, K=D, tm=tm_big, tn=tnf, tk=D, out_dtype=BF16, name="ffn_up")
            tk = _pick_tile(f, 1024, LANES)
            tnd = _pick_tile(D, 1024, LANES)
            x_all = _fused_matmul(
                xs=[(hmid, lambda m, n, k, *pf: (m, k))],
                dots=[(0, ffn_w_down, (None, tk, tnd), lambda m, n, k, *pf: (i, k, n))],
                extras=[(x_all, (tm_big, tnd), lambda m, n, k, *pf: (m, n))],
                epilogue=lambda p, e: e[0] + p[0],
                M=Mp, N=D, K=f, tm=tm_big, tn=tnd, tk=tk, out_dtype=F32, name="ffn_down")
        else:
            x_all = _moe(x_all, n_real, norm_ffn[l], moe_router, moe_w_gate, moe_w_up, moe_w_down, l // 2)

    y_p = _rmsnorm(x_all, norm_final, F32, "norm_final", rows=n_prompt)
    y_s = _rmsnorm(x_all[n_prompt:], norm_final, F32, "norm_final_tail")
    st = lambda k: jnp.stack(outs[k])
    return (y_p.reshape(B, T, D), y_s[:S].reshape(S, 1, D),
            st("cmp_p"), st("sel_p"), st("win_p"), st("sre_p"), st("sim_p"),
            st("cmp_s"), st("sel_s"), st("win_s"), st("sre_s"), st("sim_s"))
```

```python
import functools
import math

import numpy as np
import jax
import jax.numpy as jnp
from jax import lax
from jax.experimental import pallas as pl
from jax.experimental.pallas import tpu as pltpu

F32 = jnp.float32
BF16 = jnp.bfloat16

SSM_GROUP = 16
N_HEADS = 16
HEAD_DIM = 128
N_KV_HEADS = 2
Q_PER_KV = N_HEADS // N_KV_HEADS
CMP_STRIDE = 16
CMP_BLOCK = 2 * CMP_STRIDE
SEL_BLOCK = 64
N_SEL = 16
N_LOCAL_SEL = 2
WINDOW = 512
Q_BLOCK = 128
TOP_K = 2
RMS_EPS = 1e-6
NEG_INF = -1e30
BIG = 1e30

LANES = 128
SUBLANES_F32 = 8
SUBLANES_BF16 = 16
VMEM_BYTES_V7X = 64 * 1024 * 1024
VMEM_LIMIT_CAP = VMEM_BYTES_V7X - 6 * 1024 * 1024
ROW_PAD = 64


def _vmem_limit(estimate_bytes):
    return int(min(max(estimate_bytes * 5 // 4 + (4 << 20), 16 << 20), VMEM_LIMIT_CAP))


def _pick_tile(total, target, mult):
    best = None
    for t in range(mult, min(total, target) + 1, mult):
        if total % t == 0:
            best = t
    assert best is not None, (total, target, mult)
    return best


def _nbytes(shape, dtype):
    n = 1
    for s in shape:
        if s is not None:
            n *= s
    return n * jnp.dtype(dtype).itemsize


def _gelu_tanh(x):
    return 0.5 * x * (1.0 + jnp.tanh(math.sqrt(2.0 / math.pi) * (x + 0.044715 * (x * x * x))))


def _sigmoid(x):
    return 1.0 / (1.0 + jnp.exp(-x))


def _fused_matmul(*, xs, dots, extras=(), epilogue, M, N, K, tm, tn, tk, out_dtype,
                  out_rows=None, out_map=None, order="mn", prefetch=(), x_bias=None,
                  n_used_tiles=False, x_packed=False, name):
    assert not x_packed or K == tk
    assert M % tm == 0 and N % tn == 0 and K % tk == 0, (M, N, K, tm, tn, tk)
    nm, nn, nk = M // tm, N // tn, K // tk
    nd, nx, ne, npf = len(dots), len(xs), len(extras), len(prefetch)
    use_acc = nk > 1
    out_rows = M if out_rows is None else out_rows

    def wrap(f):
        if order == "mn":
            return lambda a, b, k, *pf: f(a, b, k, *pf)
        return lambda a, b, k, *pf: f(b, a, k, *pf)

    def body(*refs):
        pf = refs[:npf]
        x_refs = refs[npf:npf + nx]
        pos = npf + nx
        xb_ref = None
        if x_bias is not None:
            xb_ref = refs[pos]
            pos += 1
        w_refs = refs[pos:pos + nd]
        pos += nd
        e_refs = refs[pos:pos + ne]
        pos += ne
        o_ref = refs[pos]
        acc_refs = refs[pos + 1:]
        m_id = pl.program_id(0 if order == "mn" else 1)

        def compute():
            xv = []
            for xr in x_refs:
                v = xr[...]
                if x_packed:
                    lo = lax.bitcast_convert_type(v << 16, F32)
                    hi = lax.bitcast_convert_type(v & jnp.uint32(0xFFFF0000), F32)
                    v = jnp.concatenate([lo, hi], axis=1)
                if xb_ref is not None:
                    v = v + xb_ref[...]
                xv.append(v.astype(BF16))
            parts = []
            for i, d in enumerate(dots):
                wv = w_refs[i][...].astype(BF16)
                if len(d) > 4 and d[4]:
                    parts.append(lax.dot_general(xv[d[0]], wv, _NT, preferred_element_type=F32))
                else:
                    parts.append(jnp.dot(xv[d[0]], wv, preferred_element_type=F32))
            if not use_acc:
                o_ref[...] = epilogue(parts, [e[...] for e in e_refs]).astype(out_dtype)
                return
            k = pl.program_id(2)

            @pl.when(k == 0)
            def _():
                for i in range(nd):
                    acc_refs[i][...] = parts[i]

            @pl.when(k > 0)
            def _():
                for i in range(nd):
                    acc_refs[i][...] += parts[i]

            @pl.when(k == nk - 1)
            def _():
                o_ref[...] = epilogue([a[...] for a in acc_refs],
                                      [e[...] for e in e_refs]).astype(out_dtype)

        if n_used_tiles:
            pl.when(m_id < pf[-1][0])(compute)
        else:
            compute()

    xw = tk // 2 if x_packed else tk
    in_specs = [pl.BlockSpec((tm, xw), wrap(f)) for _, f in xs]
    args = [a for a, _ in xs]
    est = sum(2 * _nbytes((tm, xw), a.dtype) for a, _ in xs)
    if x_packed:
        est += nx * 3 * _nbytes((tm, tk), BF16)
    if x_bias is not None:
        in_specs.append(pl.BlockSpec((1, tk), wrap(lambda m, n, k, *pf: (0, k))))
        args.append(x_bias)
    for d in dots:
        w, bs, f = d[1:4]
        in_specs.append(pl.BlockSpec(bs, wrap(f)))
        args.append(w)
        est += 2 * _nbytes(bs, w.dtype) + _nbytes(bs, BF16)
    for a, bs, f in extras:
        in_specs.append(pl.BlockSpec(bs, wrap(f)))
        args.append(a)
        est += 2 * _nbytes(bs, a.dtype)
    if out_map is None:
        out_map = lambda m, n, k, *pf: (m, n)
    est += 2 * _nbytes((tm, tn), out_dtype) + (2 + nd) * _nbytes((tm, tn), F32)
    scratch = [pltpu.VMEM((tm, tn), F32) for _ in range(nd)] if use_acc else []
    grid = (nm, nn, nk) if order == "mn" else (nn, nm, nk)
    return pl.pallas_call(
        body,
        grid_spec=pltpu.PrefetchScalarGridSpec(
            num_scalar_prefetch=npf, grid=grid, in_specs=in_specs,
            out_specs=pl.BlockSpec((tm, tn), wrap(out_map)), scratch_shapes=scratch),
        out_shape=jax.ShapeDtypeStruct((out_rows, N), out_dtype),
        compiler_params=pltpu.CompilerParams(
            dimension_semantics=("parallel", "parallel", "arbitrary"),
            vmem_limit_bytes=_vmem_limit(est)),
        name=name,
    )(*prefetch, *args)


def _first(parts, extras):
    return parts[0]


def _dense(x, w, w_lead, col0, N, *, tm, tn, epilogue=_first, extras=(), out_dtype=F32, w_t=False, name):
    M, K = x.shape
    assert col0 % tn == 0
    c0 = col0 // tn
    nl = len(w_lead)
    if w_t:
        dot = (0, w, (None,) * nl + (tn, K), lambda m, n, k, *pf: tuple(w_lead) + (c0 + n, 0), True)
    else:
        dot = (0, w, (None,) * nl + (K, tn), lambda m, n, k, *pf: tuple(w_lead) + (0, c0 + n))
    return _fused_matmul(
        xs=[(x, lambda m, n, k, *pf: (m, 0))],
        dots=[dot],
        extras=extras, epilogue=epilogue, M=M, N=N, K=K, tm=tm, tn=tn, tk=K,
        out_dtype=out_dtype, name=name)


def _dense_t_unaligned(x, w, w_lead, row0, N, *, tm, tn, epilogue, out_dtype, name):
    M, K = x.shape
    sh = row0 % tn
    assert sh % SUBLANES_F32 == 0 and N % tn == 0 and M % tm == 0
    b0 = row0 // tn
    nl = len(w_lead)

    def body(x_ref, wa_ref, wb_ref, o_ref):
        wv = jnp.concatenate([wa_ref[sh:, :], wb_ref[:sh, :]], axis=0).astype(BF16)
        acc = lax.dot_general(x_ref[...], wv, _NT, preferred_element_type=F32)
        o_ref[...] = epilogue(acc).astype(out_dtype)

    wspec = lambda off: pl.BlockSpec((None,) * nl + (tn, K), lambda m, n: tuple(w_lead) + (b0 + off + n, 0))
    est = (2 * _nbytes((tm, K), x.dtype) + 4 * _nbytes((tn, K), F32) + 2 * _nbytes((tn, K), BF16)
           + 4 * _nbytes((tm, tn), F32))
    return pl.pallas_call(
        body, grid=(M // tm, N // tn),
        in_specs=[pl.BlockSpec((tm, K), lambda m, n: (m, 0)), wspec(0), wspec(1)],
        out_specs=pl.BlockSpec((tm, tn), lambda m, n: (m, n)),
        out_shape=jax.ShapeDtypeStruct((M, N), out_dtype),
        compiler_params=pltpu.CompilerParams(
            dimension_semantics=("parallel", "parallel"), vmem_limit_bytes=_vmem_limit(est)),
        name=name,
    )(x, w, w)


def _rmsnorm(x, g, out_dtype, name, rows=None):
    M, D = x.shape
    M = M if rows is None else rows
    tr = _pick_tile(M, 256, SUBLANES_BF16)

    def body(x_ref, g_ref, o_ref):
        xf = x_ref[...]
        ms = jnp.mean(xf * xf, axis=-1, keepdims=True)
        o_ref[...] = (xf * lax.rsqrt(ms + RMS_EPS) * g_ref[...]).astype(out_dtype)

    return pl.pallas_call(
        body, grid=(M // tr,),
        in_specs=[pl.BlockSpec((tr, D), lambda i: (i, 0)), pl.BlockSpec((1, D), lambda i: (0, 0))],
        out_specs=pl.BlockSpec((tr, D), lambda i: (i, 0)),
        out_shape=jax.ShapeDtypeStruct((M, D), out_dtype),
        compiler_params=pltpu.CompilerParams(dimension_semantics=("parallel",)),
        name=name,
    )(x, g.reshape(1, D))


def _s5_discretize(a_re, a_im, log_dt_b, b_re_t, b_im_t):
    C, GP = b_re_t.shape

    def body(ar_ref, ai_ref, ld_ref, br_ref, bi_ref, lr_ref, li_ref, obr_ref, obi_ref):
        ar, ai = ar_ref[...], ai_ref[...]
        dt = jnp.exp(ld_ref[...])
        mag = jnp.exp(ar * dt)
        ang = ai * dt
        lr = mag * jnp.cos(ang)
        li = mag * jnp.sin(ang)
        pr, pi = lr, li
        for k in range(SUBLANES_F32):
            lr_ref[k:k + 1, :] = pr
            li_ref[k:k + 1, :] = pi
            pr, pi = pr * lr - pi * li, pr * li + pi * lr
        x, y = lr - 1.0, li
        den = ar * ar + ai * ai
        cr = (x * ar + y * ai) / den
        ci = (y * ar - x * ai) / den
        br, bi = br_ref[...], bi_ref[...]
        obr_ref[...] = cr * br - ci * bi
        obi_ref[...] = cr * bi + ci * br

    vec = jax.ShapeDtypeStruct((SUBLANES_F32, GP), F32)
    mat = jax.ShapeDtypeStruct((C, GP), F32)
    return pl.pallas_call(body, out_shape=(vec, vec, mat, mat), name="s5_discretize")(
        a_re, a_im, log_dt_b, b_re_t, b_im_t)


GROUP_TILE = 16
S5_IN = GROUP_TILE * SSM_GROUP


def _s5_prompt(uq, w_b, w_c, pw_re, pw_im, d2, *, n_seq, seq_len, out_rows):
    ngt, kx, ns = w_b.shape[1:]
    sub = SUBLANES_F32
    tc = _pick_tile(seq_len, 256, sub)
    nch = seq_len // tc
    ntile = tc // sub

    def body(u_ref, wb_ref, wc_ref, pr_ref, pi_ref, d_ref, z_ref, fr_ref, fi_ref, st, sr_ref, si_ref):
        c = pl.program_id(2)

        @pl.when(c == 0)
        def _():
            st[...] = jnp.zeros(st.shape, F32)

        u = u_ref[...]
        ub = u.astype(BF16)
        xr = jnp.dot(ub, wb_ref[0], preferred_element_type=F32).reshape(ntile, sub, ns)
        xi = jnp.dot(ub, wb_ref[1], preferred_element_type=F32).reshape(ntile, sub, ns)
        pr, pi = pr_ref[...], pi_ref[...]
        row = lax.broadcasted_iota(jnp.int32, (sub, ns), 0)
        d = 1
        while d < sub:
            mr = jnp.where(row >= d, pr[d - 1:d, :], 0.0)
            mi = jnp.where(row >= d, pi[d - 1:d, :], 0.0)
            rr = pltpu.roll(xr, d, axis=1)
            ri = pltpu.roll(xi, d, axis=1)
            xr, xi = xr + mr * rr - mi * ri, xi + mr * ri + mi * rr
            d *= 2
        sr_ref[...] = xr
        si_ref[...] = xi

        def tile(j, carry):
            cr, ci = carry
            nr = sr_ref[j] + pr * cr - pi * ci
            ni = si_ref[j] + pr * ci + pi * cr
            sr_ref[j] = nr
            si_ref[j] = ni
            return nr[sub - 1:sub, :], ni[sub - 1:sub, :]

        cr, ci = lax.fori_loop(0, ntile, tile, (st[0:1, :], st[1:2, :]))
        st[0:1, :] = cr
        st[1:2, :] = ci
        fr_ref[...] = cr
        fi_ref[...] = ci
        y = (jnp.dot(sr_ref[...].reshape(tc, ns).astype(BF16), wc_ref[0], preferred_element_type=F32)
             + jnp.dot(si_ref[...].reshape(tc, ns).astype(BF16), wc_ref[1], preferred_element_type=F32))
        z_ref[...] = _gelu_tanh(y + d_ref[...] * u)

    u_cols = ngt * kx
    fin = jax.ShapeDtypeStruct((n_seq, ngt, 1, ns), F32)
    return pl.pallas_call(
        body, grid=(n_seq, ngt, nch),
        in_specs=[pl.BlockSpec((tc, kx), lambda b, g, c: (b * nch + c, g)),
                  pl.BlockSpec((2, None, kx, ns), lambda b, g, c: (0, g, 0, 0)),
                  pl.BlockSpec((2, None, ns, kx), lambda b, g, c: (0, g, 0, 0)),
                  pl.BlockSpec((sub, ns), lambda b, g, c: (0, g)),
                  pl.BlockSpec((sub, ns), lambda b, g, c: (0, g)),
                  pl.BlockSpec((1, kx), lambda b, g, c: (0, g))],
        out_specs=[pl.BlockSpec((tc, kx), lambda b, g, c: (b * nch + c, g)),
                   pl.BlockSpec((None, None, 1, ns), lambda b, g, c: (b, g, 0, 0)),
                   pl.BlockSpec((None, None, 1, ns), lambda b, g, c: (b, g, 0, 0))],
        out_shape=(jax.ShapeDtypeStruct((out_rows, u_cols), F32), fin, fin),
        scratch_shapes=[pltpu.VMEM((sub, ns), F32), pltpu.VMEM((ntile, sub, ns), F32),
                        pltpu.VMEM((ntile, sub, ns), F32)],
        compiler_params=pltpu.CompilerParams(
            dimension_semantics=("parallel", "parallel", "arbitrary")),
        name="s5_prompt",
    )(uq, w_b, w_c, pw_re, pw_im, d2)


def _s5_tail(uq, w_b, w_c, pw_re, pw_im, d2, s0_re, s0_im, z_prev, *, row0):
    ngt, kx, ns = w_b.shape[1:]
    n_tail = s0_re.shape[0]
    assert row0 % n_tail == 0
    rb = row0 // n_tail

    def body(u_ref, wb_ref, wc_ref, pr_ref, pi_ref, d_ref, s0r_ref, s0i_ref, zp_ref, z_ref, fr_ref, fi_ref):
        u = u_ref[...]
        ub = u.astype(BF16)
        lr, li = pr_ref[0:1, :], pi_ref[0:1, :]
        s0r, s0i = s0r_ref[...], s0i_ref[...]
        sr = lr * s0r - li * s0i + jnp.dot(ub, wb_ref[0], preferred_element_type=F32)
        si = lr * s0i + li * s0r + jnp.dot(ub, wb_ref[1], preferred_element_type=F32)
        fr_ref[...] = sr
        fi_ref[...] = si
        y = (jnp.dot(sr.astype(BF16), wc_ref[0], preferred_element_type=F32)
             + jnp.dot(si.astype(BF16), wc_ref[1], preferred_element_type=F32))
        z_ref[...] = _gelu_tanh(y + d_ref[...] * u)

    st = jax.ShapeDtypeStruct(s0_re.shape, F32)
    return pl.pallas_call(
        body, grid=(ngt,),
        in_specs=[pl.BlockSpec((n_tail, kx), lambda g: (rb, g)),
                  pl.BlockSpec((2, None, kx, ns), lambda g: (0, g, 0, 0)),
                  pl.BlockSpec((2, None, ns, kx), lambda g: (0, g, 0, 0)),
                  pl.BlockSpec((SUBLANES_F32, ns), lambda g: (0, g)),
                  pl.BlockSpec((SUBLANES_F32, ns), lambda g: (0, g)),
                  pl.BlockSpec((1, kx), lambda g: (0, g)),
                  pl.BlockSpec((n_tail, ns), lambda g: (0, g)),
                  pl.BlockSpec((n_tail, ns), lambda g: (0, g)),
                  pl.BlockSpec(memory_space=pl.ANY)],
        out_specs=[pl.BlockSpec((n_tail, kx), lambda g: (rb, g)),
                   pl.BlockSpec((n_tail, ns), lambda g: (0, g)),
                   pl.BlockSpec((n_tail, ns), lambda g: (0, g))],
        out_shape=(jax.ShapeDtypeStruct(z_prev.shape, F32), st, st),
        input_output_aliases={8: 0},
        compiler_params=pltpu.CompilerParams(dimension_semantics=("parallel",)),
        name="s5_tail",
    )(uq, w_b, w_c, pw_re, pw_im, d2, s0_re, s0_im, z_prev)


def _block_diag_tiles(w):
    eye = jnp.eye(GROUP_TILE, dtype=w.dtype)
    nt, gt, a, b = w.shape
    return jnp.einsum("tgab,gh->tgahb", w, eye).reshape(nt, gt * a, gt * b)


def _s5_branch(uq, u_cols, s0_re, s0_im, n_prompt, n_seq, seq_len, params, layer, tm):
    (a_re, a_im, b_re, b_im, c_re, c_im, d, log_dt, w_glu, b_glu) = params
    Mp = uq.shape[0]
    G, P = a_re.shape[1], a_re.shape[2]
    C = SSM_GROUP
    GP = G * P
    ngt = G // GROUP_TILE

    lam_re, lam_im, bb_re, bb_im = _s5_discretize(
        a_re[layer].reshape(1, GP), a_im[layer].reshape(1, GP),
        jnp.broadcast_to(log_dt[layer][:, None], (G, P)).reshape(1, GP),
        b_re[layer].transpose(2, 0, 1).reshape(C, GP), b_im[layer].transpose(2, 0, 1).reshape(C, GP))

    def b_tiles(bb):
        t = bb.reshape(C, ngt, GROUP_TILE, P).transpose(1, 2, 0, 3)
        return _block_diag_tiles(t)

    w_b =jnp.stack([b_tiles(bb_re), b_tiles(bb_im)]).astype(BF16)

    def c_tiles(cc):
        t = cc.reshape(ngt, GROUP_TILE, C, P).transpose(0, 1, 3, 2)
        return _block_diag_tiles(t)

    w_c = jnp.stack([c_tiles(c_re[layer]), -c_tiles(c_im[layer])]).astype(BF16)

    d2 = d[layer].reshape(1, u_cols)
    z, fp_re, fp_im = _s5_prompt(uq, w_b, w_c, lam_re, lam_im, d2, n_seq=n_seq, seq_len=seq_len,
                                 out_rows=Mp)
    z, fs_re, fs_im = _s5_tail(uq, w_b, w_c, lam_re, lam_im, d2, s0_re, s0_im, z, row0=n_prompt)

    tn = _pick_tile(u_cols, 512, LANES)
    ssm_out = _dense(z, w_glu, (layer,), 0, u_cols, tm=tm, tn=tn,
                     extras=[(z, (tm, tn), lambda m, n, k, *pf: (m, n)),
                             (b_glu[layer].reshape(1, u_cols), (1, tn), lambda m, n, k, *pf: (0, n))],
                     epilogue=lambda p, e: e[0] * _sigmoid(p[0] + e[1]),
                     out_dtype=BF16, name="s5_glu")
    return ssm_out, (fp_re, fp_im), (fs_re, fs_im)


KV_COLS = 2 * N_KV_HEADS * HEAD_DIM
CHUNK_COLS = CMP_STRIDE * KV_COLS


KV_ROWS = 2 * N_KV_HEADS


def _compress_lead_trail(srcs, grid, pe, w1, rows_per_src, n_rows, prefetch=(), token_major=False,
                         name="cmp_lt"):
    ns = len(srcs)
    npf = len(prefetch)
    rows = ns * rows_per_src
    kd = CMP_STRIDE * HEAD_DIM

    def piece(xr, j, c):
        if token_major:
            return xr[pl.ds(j * KV_ROWS + c, rows_per_src, stride=CMP_STRIDE * KV_ROWS), :]
        return xr[:, j * KV_COLS + c * HEAD_DIM: j * KV_COLS + (c + 1) * HEAD_DIM]

    def body(*refs):
        x_refs = refs[npf:npf + ns]
        pe_ref, w_ref, o_ref = refs[npf + ns:]
        for kv in range(2):
            per_head = []
            for h in range(N_KV_HEADS):
                c = kv * N_KV_HEADS + h
                per_src = [jnp.concatenate([piece(xr, j, c) for j in range(CMP_STRIDE)], axis=1)
                           for xr in x_refs]
                per_head.append(per_src[0] if ns == 1 else jnp.concatenate(per_src, axis=0))
            xs = jnp.concatenate(per_head, axis=0)
            for lt in range(2):
                xv = (xs + pe_ref[kv, lt]).astype(BF16)
                r = jnp.dot(xv, w_ref[kv, lt].astype(BF16), preferred_element_type=F32)
                for h in range(N_KV_HEADS):
                    o_ref[kv, lt, h] = r[h * rows:(h + 1) * rows]

    in_specs = [pl.BlockSpec(bs, f) for _, bs, f in srcs]
    in_specs += [pl.BlockSpec((2, 2, 1, kd), lambda *a: (0, 0, 0, 0)),
                 pl.BlockSpec((2, 2, kd, HEAD_DIM), lambda *a: (0, 0, 0, 0))]
    step_of = lambda *a: a[0] * grid[1] + a[1]
    return pl.pallas_call(
        body,
        grid_spec=pltpu.PrefetchScalarGridSpec(
            num_scalar_prefetch=npf, grid=grid, in_specs=in_specs,
            out_specs=pl.BlockSpec((2, 2, N_KV_HEADS, rows, HEAD_DIM),
                                   lambda *a: (0, 0, 0, step_of(*a), 0))),
        out_shape=jax.ShapeDtypeStruct((2, 2, N_KV_HEADS, n_rows, HEAD_DIM), F32),
        compiler_params=pltpu.CompilerParams(
            dimension_semantics=("parallel", "arbitrary"), vmem_limit_bytes=48 << 20),
        name=name,
    )(*prefetch, *[a for a, _, _ in srcs], pe, w1)


def _compress_finish(lt, n_seq, n_chunk, b1, w2, name):
    lead = lt[:, 0]
    trail = lt[:, 1].reshape(2, N_KV_HEADS, n_seq, n_chunk, HEAD_DIM)
    trail = jnp.roll(trail, -1, axis=3).reshape(lead.shape)
    rows = N_KV_HEADS * n_seq * n_chunk
    lead = lead.reshape(2, rows, HEAD_DIM)
    trail = trail.reshape(2, rows, HEAD_DIM)
    rt = _pick_tile(rows, 1024, SUBLANES_F32)

    def body(a_ref, t_ref, b_ref, w_ref, o_ref):
        hid = _gelu_tanh(a_ref[...] + t_ref[...] + b_ref[...])
        o_ref[...] = jnp.dot(hid.astype(BF16), w_ref[...].astype(BF16), preferred_element_type=F32)

    out = pl.pallas_call(
        body, grid=(2, rows // rt),
        in_specs=[pl.BlockSpec((None, rt, HEAD_DIM), lambda kv, r: (kv, r, 0)),
                  pl.BlockSpec((None, rt, HEAD_DIM), lambda kv, r: (kv, r, 0)),
                  pl.BlockSpec((None, 1, HEAD_DIM), lambda kv, r: (kv, 0, 0)),
                  pl.BlockSpec((None, HEAD_DIM, HEAD_DIM), lambda kv, r: (kv, 0, 0))],
        out_specs=pl.BlockSpec((None, rt, HEAD_DIM), lambda kv, r: (kv, r, 0)),
        out_shape=jax.ShapeDtypeStruct((2, rows, HEAD_DIM), F32),
        compiler_params=pltpu.CompilerParams(dimension_semantics=("parallel", "parallel")),
        name=name,
    )(lead, trail, b1.reshape(2, 1, HEAD_DIM), w2)
    return out.reshape(2, N_KV_HEADS, n_seq, n_chunk, HEAD_DIM)


def _overlap_matrix(n_cmp, n_blk, rows, cols):
    start = np.arange(rows)[:, None] * CMP_STRIDE
    js = np.arange(cols)[None, :] * SEL_BLOCK
    ov = (start < js + SEL_BLOCK) & (start + CMP_BLOCK > js)
    ov &= (np.arange(rows)[:, None] < n_cmp) & (np.arange(cols)[None, :] < n_blk)
    return jnp.asarray(ov.astype(np.float32))


def _masked_softmax(s, valid):
    s = jnp.where(valid, s, NEG_INF)
    p = jnp.exp(s - jnp.max(s, axis=-1, keepdims=True)) * jnp.where(valid, 1.0, 0.0)
    return p * (1.0 / jnp.maximum(jnp.sum(p, axis=-1, keepdims=True), 1e-30))


def _softmax_dot(s, valid, v):
    hg, r, n = s.shape
    s = jnp.where(valid, s, NEG_INF)
    p = jnp.exp(s - jnp.max(s, axis=-1, keepdims=True))
    den = jnp.sum(p, axis=-1, keepdims=True)
    o = jnp.dot(p.reshape(hg * r, n).astype(BF16), v, preferred_element_type=F32)
    return o.reshape(hg, r, v.shape[1]) * (1.0 / den)


_NT = (((1,), (1,)), ((), ()))


def _nsa_prompt(uq, q_blk0, glg, cmp_kv, kvs, kvw, *, n_seq, seq_len, out_rows):
    T = seq_len
    QB = Q_BLOCK
    HG = Q_PER_KV
    HD = HEAD_DIM
    nqb = T // QB
    ncp = cmp_kv.shape[3]
    n_cmp = T // CMP_STRIDE - 1
    n_blk = -(-T // SEL_BLOCK)
    n_pick = min(N_SEL, n_blk)
    assert T % QB == 0 and T % SEL_BLOCK == 0 and n_blk <= LANES and ncp <= LANES
    assert n_pick > N_LOCAL_SEL
    CH = min(512, T)
    assert T % CH == 0
    WB = min(WINDOW + QB, T)
    scale = HD ** -0.5
    blk_of_key = np.arange(T) // SEL_BLOCK
    expand = (np.arange(LANES)[:, None] == blk_of_key[None, :]).astype(np.float32)
    expand = jnp.asarray(expand.reshape(LANES, T // CH, CH).transpose(1, 0, 2), dtype=BF16)
    ov = _overlap_matrix(n_cmp, n_blk, ncp, LANES)

    perm = np.zeros((N_KV_HEADS, LANES, LANES), np.float32)
    for hh in range(N_KV_HEADS):
        for g in range(HG):
            for j in range(3):
                perm[hh, (hh * HG + g) * 3 + j, j * HG + g] = 1.0
    perm = jnp.asarray(perm)

    def body(q_ref, gl_ref, perm_ref, kc_ref, vc_ref, ks_ref, vs_ref, kw_ref, vw_ref, e_ref, ov_ref, o_ref,
             m_ref, l_ref, acc_ref):
        i = pl.program_id(2)
        start = i * QB
        q = q_ref[...]
        q_st = jnp.concatenate([q[:, g * HD:(g + 1) * HD] for g in range(HG)], axis=0).astype(BF16)
        qpos = start + lax.broadcasted_iota(jnp.int32, (QB, 1), 0)

        s = lax.dot_general(q_st, kc_ref[...].astype(BF16), _NT, preferred_element_type=F32) * scale
        n_idx = lax.broadcasted_iota(jnp.int32, (QB, ncp), 1)
        cvalid = (n_idx * CMP_STRIDE + (CMP_BLOCK - 1) <= qpos) & (n_idx < n_cmp)
        p_c = _masked_softmax(s.reshape(HG, QB, ncp), cvalid)
        o_c = jnp.dot(p_c.reshape(HG * QB, ncp).astype(BF16), vc_ref[...].astype(BF16),
                      preferred_element_type=F32).reshape(HG, QB, HD)

        imp = jnp.dot(jnp.sum(p_c, axis=0), ov_ref[...], preferred_element_type=F32,
                      precision=lax.Precision.HIGHEST)
        blk = lax.broadcasted_iota(jnp.int32, (QB, LANES), 1)
        cur = qpos // SEL_BLOCK
        forced = (blk == 0) | ((blk <= cur) & (blk > cur - N_LOCAL_SEL))
        score = jnp.where(blk > cur, -BIG, jnp.where(forced, BIG, imp))
        rank = jnp.zeros((QB, LANES), F32)
        for j2 in range(n_blk):
            col = score[:, j2:j2 + 1]
            ahead = (col > score) | ((col == score) & (blk > j2))
            rank = rank + jnp.where(ahead, 1.0, 0.0)
        sel = jnp.where((rank < n_pick) & (blk < n_blk), 1.0, 0.0).astype(BF16)

        m_ref[...] = jnp.full(m_ref.shape, NEG_INF, F32)
        l_ref[...] = jnp.zeros(l_ref.shape, F32)
        acc_ref[...] = jnp.zeros(acc_ref.shape, F32)

        def chunk(c, carry):
            k0 = pl.multiple_of(c * CH, CH)
            kch = ks_ref[pl.ds(k0, CH), :].astype(BF16)
            vch = vs_ref[pl.ds(k0, CH), :].astype(BF16)
            sc = lax.dot_general(q_st, kch, _NT, preferred_element_type=F32) * scale
            picked = jnp.dot(sel, e_ref[c], preferred_element_type=F32)
            kpos = k0 + lax.broadcasted_iota(jnp.int32, (QB, CH), 1)
            valid = (picked > 0.5) & (kpos <= qpos)
            s3 = jnp.where(valid, sc.reshape(HG, QB, CH), NEG_INF)
            m_old = m_ref[...]
            m_new = jnp.maximum(m_old, jnp.max(s3, axis=-1, keepdims=True))
            p = jnp.exp(s3 - m_new)
            alpha = jnp.exp(m_old - m_new)
            l_ref[...] = alpha * l_ref[...] + jnp.sum(p, axis=-1, keepdims=True)
            pv = jnp.dot(p.reshape(HG * QB, CH).astype(BF16), vch, preferred_element_type=F32)
            acc_ref[...] = alpha * acc_ref[...] + pv.reshape(HG, QB, HD)
            m_ref[...] = m_new
            return carry

        lax.fori_loop(0, (start + QB + CH - 1) // CH, chunk, 0)
        o_s = acc_ref[...] * (1.0 / l_ref[...])

        w0 = pl.multiple_of(jnp.minimum(jnp.maximum(start - WINDOW, 0), T - WB), QB)
        sw = lax.dot_general(q_st, kw_ref[pl.ds(w0, WB), :].astype(BF16), _NT,
                             preferred_element_type=F32) * scale
        dist = qpos - (w0 + lax.broadcasted_iota(jnp.int32, (QB, WB), 1))
        o_w = _softmax_dot(sw.reshape(HG, QB, WB), (dist >= 0) & (dist < WINDOW),
                           vw_ref[pl.ds(w0, WB), :].astype(BF16))

        gate = _sigmoid(jnp.dot(gl_ref[...], perm_ref[...], preferred_element_type=F32,
                                precision=lax.Precision.HIGHEST))
        for g in range(HG):
            og = (gate[:, g:g + 1] * o_c[g] + gate[:, HG + g:HG + g + 1] * o_s[g]
                  + gate[:, 2 * HG + g:2 * HG + g + 1] * o_w[g])
            o_ref[:, g * HD:(g + 1) * HD] = og.astype(o_ref.dtype)

    row = lambda b, h, i: b * nqb + i
    kv_spec = lambda kv: pl.BlockSpec((T, HD), lambda b, h, i: (b, kv * N_KV_HEADS + h))
    return pl.pallas_call(
        body, grid=(n_seq, N_KV_HEADS, nqb),
        in_specs=[pl.BlockSpec((QB, HG * HD), lambda b, h, i: (row(b, h, i), q_blk0 + h)),
                  pl.BlockSpec((QB, LANES), lambda b, h, i: (row(b, h, i), 0)),
                  pl.BlockSpec((None, LANES, LANES), lambda b, h, i: (h, 0, 0)),
                  pl.BlockSpec((None, None, None, ncp, HD), lambda b, h, i: (0, h, b, 0, 0)),
                  pl.BlockSpec((None, None, None, ncp, HD), lambda b, h, i: (1, h, b, 0, 0)),
                  kv_spec(0), kv_spec(1), kv_spec(0), kv_spec(1),
                  pl.BlockSpec((T // CH, LANES, CH), lambda b, h, i: (0, 0, 0)),
                  pl.BlockSpec((ncp, LANES), lambda b, h, i: (0, 0))],
        out_specs=pl.BlockSpec((QB, HG * HD), lambda b, h, i: (row(b, h, i), h)),
        out_shape=jax.ShapeDtypeStruct((out_rows, N_HEADS * HD), BF16),
        scratch_shapes=[pltpu.VMEM((HG, QB, 1), F32), pltpu.VMEM((HG, QB, 1), F32),
                        pltpu.VMEM((HG, QB, HD), F32)],
        compiler_params=pltpu.CompilerParams(
            dimension_semantics=("parallel", "parallel", "arbitrary"), vmem_limit_bytes=48 << 20),
        name="nsa_prompt",
    )(uq, glg, perm, cmp_kv, cmp_kv, kvs, kvs, kvw, kvw, expand, ov)


def _nsa_sample_select(q_s, cmp_kv, past_len):
    S = q_s.shape[0]
    HG, HD = Q_PER_KV, HEAD_DIM
    nc = cmp_kv.shape[3]
    n_cmp = nc - 1
    q_pos = past_len
    n_blk = -(-(past_len + 1) // SEL_BLOCK)
    nbp = -(-n_blk // LANES) * LANES
    n_pick = min(N_SEL, n_blk)
    cur = q_pos // SEL_BLOCK
    scale = HD ** -0.5
    ov = _overlap_matrix(n_cmp, n_blk, nc, nbp)

    def body(q_ref, kc_ref, vc_ref, ov_ref, oc_ref, idx_ref):
        q = q_ref[...].astype(BF16)
        s = lax.dot_general(q, kc_ref[...].astype(BF16), _NT, preferred_element_type=F32) * scale
        n_idx = lax.broadcasted_iota(jnp.int32, (HG, nc), 1)
        cvalid = (n_idx * CMP_STRIDE + (CMP_BLOCK - 1) <= q_pos) & (n_idx < n_cmp)
        p_c = _masked_softmax(s, cvalid)
        oc_ref[...] = jnp.dot(p_c.astype(BF16), vc_ref[...].astype(BF16), preferred_element_type=F32)
        imp = jnp.sum(jnp.dot(p_c, ov_ref[...], preferred_element_type=F32,
                              precision=lax.Precision.HIGHEST), axis=0, keepdims=True)
        blk = lax.broadcasted_iota(jnp.int32, (1, nbp), 1)
        forced = (blk == 0) | ((blk <= cur) & (blk > cur - N_LOCAL_SEL))
        score = jnp.where(blk > cur, -BIG, jnp.where(forced, BIG, imp))
        col = jnp.broadcast_to(score, (LANES, nbp)).T[:, 0:1]
        jp = lax.broadcasted_iota(jnp.int32, (nbp, nbp), 0)
        jj = lax.broadcasted_iota(jnp.int32, (nbp, nbp), 1)
        ahead = ((col > score) | ((col == score) & (jp < jj))) & (jp < n_blk)
        rank = jnp.sum(jnp.where(ahead, 1.0, 0.0), axis=0, keepdims=True)
        lane = lax.broadcasted_iota(jnp.int32, (1, LANES), 1)
        out = jnp.zeros((1, LANES), jnp.int32)
        for r in range(n_pick):
            hit = (rank == float(r)) & (blk < n_blk)
            idx_r = jnp.sum(jnp.where(hit, blk, 0), axis=1, keepdims=True)
            out = jnp.where(lane == r, idx_r, out)
        idx_ref[...] = jnp.broadcast_to(out, (SUBLANES_F32, LANES))

    o_c, idx = pl.pallas_call(
        body, grid=(S, N_KV_HEADS),
        in_specs=[pl.BlockSpec((None, None, HG, HD), lambda b, h: (b, h, 0, 0)),
                  pl.BlockSpec((None, None, None, nc, HD), lambda b, h: (0, h, b, 0, 0)),
                  pl.BlockSpec((None, None, None, nc, HD), lambda b, h: (1, h, b, 0, 0)),
                  pl.BlockSpec((nc, nbp), lambda b, h: (0, 0))],
        out_specs=[pl.BlockSpec((None, None, HG, HD), lambda b, h: (b, h, 0, 0)),
                   pl.BlockSpec((None, None, SUBLANES_F32, LANES), lambda b, h: (b, h, 0, 0))],
        out_shape=(jax.ShapeDtypeStruct((S, N_KV_HEADS, HG, HD), F32),
                   jax.ShapeDtypeStruct((S, N_KV_HEADS, SUBLANES_F32, LANES), jnp.int32)),
        compiler_params=pltpu.CompilerParams(dimension_semantics=("parallel", "parallel")),
        name="nsa_sample_select",
    )(q_s, cmp_kv, cmp_kv, ov)
    return o_c, idx[:, :, 0, :n_pick], n_pick


def _nsa_sample_attend(q_s, o_c, gate_s, src, live, cache_sel_v, layer, new_sel, cache_win_v, new_win,
                       n_pick):
    S = q_s.shape[0]
    HG, HD = Q_PER_KV, HEAD_DIM
    wb = cache_win_v.shape[2]
    scale = HD ** -0.5
    nkeys = n_pick * SEL_BLOCK

    def body(src_ref, live_ref, q_ref, oc_ref, g_ref, *rest):
        blk_refs = rest[:n_pick]
        kn_ref, vn_ref, kw_ref, vw_ref, kwn_ref, vwn_ref, o_ref = rest[n_pick:]
        b, h = pl.program_id(0), pl.program_id(1)

        def rows_of(kv):
            per_head = [jnp.concatenate(
                [r[pl.ds(kv * N_KV_HEADS + hh, SEL_BLOCK, stride=KV_ROWS), :] for r in blk_refs], axis=0)
                for hh in range(N_KV_HEADS)]
            out = per_head[0]
            for hh in range(1, N_KV_HEADS):
                out = jnp.where(h == hh, per_head[hh], out)
            return out.astype(BF16)

        base = (b * N_KV_HEADS + h) * n_pick
        q = q_ref[...]
        qb = q.astype(BF16)

        def with_new_key(s, valid, s_new, v_old, v_new):
            sm = jnp.where(valid, s, NEG_INF)
            m = jnp.maximum(jnp.max(sm, axis=-1, keepdims=True), s_new)
            p = jnp.exp(sm - m) * jnp.where(valid, 1.0, 0.0)
            p_new = jnp.exp(s_new - m)
            den = jnp.sum(p, axis=-1, keepdims=True) + p_new
            return (jnp.dot(p.astype(BF16), v_old, preferred_element_type=F32) + p_new * v_new) / den

        keys = rows_of(0)
        vals = rows_of(1)
        s = lax.dot_general(qb, keys, _NT, preferred_element_type=F32) * scale
        lane = lax.broadcasted_iota(jnp.int32, (1, nkeys), 1)
        livef = jnp.zeros((1, nkeys), F32)
        for r in range(n_pick):
            livef = jnp.where((lane >= r * SEL_BLOCK) & (lane < (r + 1) * SEL_BLOCK),
                              live_ref[base + r].astype(F32), livef)
        s_new = jnp.sum(q * kn_ref[...], axis=-1, keepdims=True) * scale
        o_s = with_new_key(s, livef > 0.5, s_new, vals, vn_ref[...])

        sw = lax.dot_general(qb, kw_ref[...].astype(BF16), _NT, preferred_element_type=F32) * scale
        wrow = lax.broadcasted_iota(jnp.int32, (1, wb), 1)
        sw_new = jnp.sum(q * kwn_ref[...], axis=-1, keepdims=True) * scale
        o_w = with_new_key(sw, wrow > wb - WINDOW, sw_new, vw_ref[...].astype(BF16), vwn_ref[...])

        gate = _sigmoid(g_ref[...])
        o_ref[...] = gate[:, 0:1] * oc_ref[...] + gate[:, 1:2] * o_s + gate[:, 2:3] * o_w

    bpp = cache_sel_v.shape[2] // (SEL_BLOCK * KV_ROWS)

    def blk_spec(r):
        def index(b, h, src_ref, live_ref):
            s = src_ref[(b * N_KV_HEADS + h) * n_pick + r]
            return (layer, s // bpp, s % bpp, 0)
        return pl.BlockSpec((None, None, SEL_BLOCK * KV_ROWS, HD), index)

    qspec = pl.BlockSpec((None, None, HG, HD), lambda b, h, *_: (b, h, 0, 0))
    gspec = pl.BlockSpec((None, None, HG, LANES), lambda b, h, *_: (b, h, 0, 0))
    new_spec = lambda kv: pl.BlockSpec((None, None, 1, HD), lambda b, h, *_: (b, kv * N_KV_HEADS + h, 0, 0))
    win_spec = lambda kv: pl.BlockSpec((None, None, wb, HD), lambda b, h, *_: (layer, b, 0, kv * N_KV_HEADS + h))
    in_specs = ([qspec, qspec, gspec] + [blk_spec(r) for r in range(n_pick)]
                + [new_spec(0), new_spec(1), win_spec(0), win_spec(1), new_spec(0), new_spec(1)])
    return pl.pallas_call(
        body,
        grid_spec=pltpu.PrefetchScalarGridSpec(
            num_scalar_prefetch=2, grid=(S, N_KV_HEADS), in_specs=in_specs,
            out_specs=pl.BlockSpec((None, None, HG, HD), lambda b, h, *_: (b, h, 0, 0))),
        out_shape=jax.ShapeDtypeStruct((S, N_KV_HEADS, HG, HD), F32),
        compiler_params=pltpu.CompilerParams(dimension_semantics=("parallel", "parallel")),
        name="nsa_sample_attend",
    )(src, live, q_s, o_c, gate_s, *([cache_sel_v] * n_pick), new_sel, new_sel,
      cache_win_v, cache_win_v, new_win, new_win)


def _nsa_sample(q_s, gate_s, new_sel, new_win, cache_cmp, cache_sel, cache_win, page_table, layer,
                pe, w1, b1, w2):
    S, n_pages = page_table.shape
    depth, n_pool, page = cache_cmp.shape[:3]
    past_len = n_pages * page
    assert page % CMP_STRIDE == 0 and page % SEL_BLOCK == 0
    cpp = page // CMP_STRIDE
    bpp = page // SEL_BLOCK
    pps = _pick_tile(n_pages, 16, 1)
    cache_v = cache_cmp.reshape(depth, n_pool, page * KV_ROWS, HEAD_DIM)
    pt_flat = page_table.reshape(-1)
    srcs = [(cache_v, (None, None, page * KV_ROWS, HEAD_DIM),
             (lambda b, s, pt, p=p: (layer, pt[b * n_pages + s * pps + p], 0, 0))) for p in range(pps)]
    nc = n_pages * cpp
    lt = _compress_lead_trail(srcs, (S, n_pages // pps), pe, w1, cpp, S * nc, prefetch=(pt_flat,),
                              token_major=True, name="cmp_lt_sample")
    cmp_kv = _compress_finish(lt, S, nc, b1, w2, "cmp_finish_sample")
    o_c, idx, n_pick = _nsa_sample_select(q_s, cmp_kv, past_len)

    n_past_blk = past_len // SEL_BLOCK
    idc = jnp.minimum(idx, n_past_blk - 1)
    pages = jnp.take_along_axis(page_table[:, None, :], idc // bpp, axis=2)
    src = (pages * bpp + idc % bpp).astype(jnp.int32).reshape(-1)
    live = (idx < n_past_blk).astype(jnp.int32).reshape(-1)
    wb = cache_win.shape[2]
    out = _nsa_sample_attend(
        q_s, o_c, gate_s, src, live, cache_sel.reshape(depth, n_pool, page * KV_ROWS, HEAD_DIM), layer,
        new_sel, cache_win.reshape(depth, S, wb, KV_COLS), new_win, n_pick)
    return out.reshape(S, N_HEADS * HEAD_DIM)


MOE_ROW_TILE = 512


def _moe_route(x, g, w_router):
    M, D = x.shape
    E = w_router.shape[1]
    tr = _pick_tile(M, 256, SUBLANES_BF16)
    wr = jnp.pad(w_router, ((0, 0), (0, LANES - E)))

    def body(x_ref, g_ref, w_ref, h_ref, r_ref):
        xf = x_ref[...]
        ms = jnp.mean(xf * xf, axis=-1, keepdims=True)
        hf = xf * lax.rsqrt(ms + RMS_EPS) * g_ref[...]
        bits = lax.bitcast_convert_type(hf.astype(BF16).astype(F32), jnp.uint32)
        h_ref[...] = (bits[:, :D // 2] >> 16) | (bits[:, D // 2:] & jnp.uint32(0xFFFF0000))
        logits = jnp.dot(hf, w_ref[...], preferred_element_type=F32, precision=lax.Precision.HIGHEST)
        lane = lax.broadcasted_iota(jnp.int32, (tr, LANES), 1)
        logits = jnp.where(lane < E, logits, -jnp.inf)
        v1 = jnp.max(logits, axis=-1, keepdims=True)
        i1 = jnp.min(jnp.where(logits == v1, lane, LANES), axis=-1, keepdims=True)
        rest = jnp.where(lane == i1, -jnp.inf, logits)
        v2 = jnp.max(rest, axis=-1, keepdims=True)
        i2 = jnp.min(jnp.where(rest == v2, lane, LANES), axis=-1, keepdims=True)
        e2 = jnp.exp(v2 - v1)
        g1 = 1.0 / (1.0 + e2)
        out = jnp.where(lane == 0, i1.astype(F32), jnp.where(lane == 1, i2.astype(F32),
                        jnp.where(lane == 2, g1, jnp.where(lane == 3, e2 * g1, 0.0))))
        r_ref[...] = out

    return pl.pallas_call(
        body, grid=(M // tr,),
        in_specs=[pl.BlockSpec((tr, D), lambda i: (i, 0)), pl.BlockSpec((1, D), lambda i: (0, 0)),
                  pl.BlockSpec((D, LANES), lambda i: (0, 0))],
        out_specs=[pl.BlockSpec((tr, D // 2), lambda i: (i, 0)), pl.BlockSpec((tr, LANES), lambda i: (i, 0))],
        out_shape=(jax.ShapeDtypeStruct((M, D // 2), jnp.uint32), jax.ShapeDtypeStruct((M, LANES), F32)),
        compiler_params=pltpu.CompilerParams(dimension_semantics=("parallel",)),
        name="moe_route",
    )(x, g.reshape(1, D), wr)


MOE_GROUP_TILES = 5


def _moe_down(hmid, wd, li, gate_b, tile_expert, n_used, tm):
    rows, F = hmid.shape
    D = wd.shape[-1]
    mt = MOE_GROUP_TILES
    n_tiles = rows // tm
    tk = _pick_tile(F, 2048, LANES)
    tn = _pick_tile(D, 1024, LANES)
    nk = F // tk
    n_slots = n_tiles * nk

    t = jnp.arange(n_tiles, dtype=jnp.int32)
    used = t < n_used
    same = (tile_expert[:, None] == tile_expert[None, :]) & used[None, :]
    first = jnp.min(jnp.where(same, t[None, :], n_tiles), axis=1)
    cnt = jnp.sum(same.astype(jnp.int32), axis=1)
    j = t - first
    m_local = jnp.where(used, j % mt, 0)
    g_first = t - m_local
    g_cnt = jnp.where(used, jnp.minimum(mt, cnt - (j - m_local)), 1)
    kk = jnp.arange(nk, dtype=jnp.int32)
    pos = (nk * g_first[:, None] + kk[None, :] * g_cnt[:, None] + m_local[:, None]).reshape(-1)
    n_items = (nk * n_used).astype(jnp.int32)
    order = jnp.minimum(jnp.arange(n_slots, dtype=jnp.int32), n_items - 1)
    table = lambda v: jnp.zeros((n_slots,), jnp.int32).at[pos].set(v.reshape(-1))[order]
    it_tile = table(jnp.broadcast_to(t[:, None], (n_tiles, nk)))
    it_k = table(jnp.broadcast_to(kk[None, :], (n_tiles, nk)))
    it_slot = table(jnp.broadcast_to(m_local[:, None], (n_tiles, nk)))
    it_exp = tile_expert[it_tile]
    it_out = jnp.where(it_k == nk - 1, it_tile, it_tile - it_slot)
    run_start = (jnp.arange(n_slots, dtype=jnp.int32) < n_items) & (it_slot == 0)
    it_run = (jnp.cumsum(run_start.astype(jnp.int32)) - 1).astype(jnp.int32)
    n_runs = jnp.sum(run_start.astype(jnp.int32)).reshape(1)
    it_nxt = _next_run_start(run_start)
    nn = D // tn

    def body(tile_ref, k_ref, slot_ref, exp_ref, out_ref, n_ref, run_ref, nxt_ref, nruns_ref,
             h_ref, w_hbm, g_ref, o_ref, acc_ref, wbuf, sem):
        n, i = pl.program_id(0), pl.program_id(1)

        def copy(item, col_tile, wslot):
            src = w_hbm.at[li, exp_ref[item], pl.ds(pl.multiple_of(k_ref[item] * tk, tk), tk),
                           pl.ds(pl.multiple_of(col_tile * tn, tn), tn)]
            return pltpu.make_async_copy(src, wbuf.at[wslot], sem.at[wslot])

        @pl.when(i < n_ref[0])
        def _():
            k = k_ref[i]
            slot = slot_ref[i]
            wslot = (n * nruns_ref[0] + run_ref[i]) % 2

            @pl.when(slot == 0)
            def _():
                @pl.when((n == 0) & (i == 0))
                def _():
                    copy(0, 0, 0).start()

                copy(i, n, wslot).wait()
                nx = nxt_ref[i]

                @pl.when(nx >= 0)
                def _():
                    copy(jnp.maximum(nx, 0), n, 1 - wslot).start()

                @pl.when((nx < 0) & (n + 1 < nn))
                def _():
                    copy(0, n + 1, 1 - wslot).start()

            p = jnp.dot(h_ref[...], wbuf[wslot].astype(BF16), preferred_element_type=F32)

            @pl.when(k == 0)
            def _():
                acc_ref[slot] = p

            @pl.when(k > 0)
            def _():
                acc_ref[slot] += p

            @pl.when(k == nk - 1)
            def _():
                o_ref[...] = acc_ref[slot] * g_ref[...][:, 0:1]

    est = (2 * _nbytes((tm, tk), BF16) + 2 * _nbytes((tk, tn), F32) + _nbytes((tk, tn), BF16)
           + (mt + 3) * _nbytes((tm, tn), F32))
    return pl.pallas_call(
        body,
        grid_spec=pltpu.PrefetchScalarGridSpec(
            num_scalar_prefetch=9, grid=(nn, n_slots),
            in_specs=[pl.BlockSpec((tm, tk), lambda n, i, tr, kr, *_: (tr[i], kr[i])),
                      pl.BlockSpec(memory_space=pl.ANY),
                      pl.BlockSpec((tm, LANES), lambda n, i, tr, *_: (tr[i], 0))],
            out_specs=pl.BlockSpec((tm, tn), lambda n, i, tr, kr, sr, er, orr, *_: (orr[i], n)),
            scratch_shapes=[pltpu.VMEM((mt, tm, tn), F32), pltpu.VMEM((2, tk, tn), F32),
                            pltpu.SemaphoreType.DMA((2,))]),
        out_shape=jax.ShapeDtypeStruct((rows, D), F32),
        compiler_params=pltpu.CompilerParams(
            dimension_semantics=("arbitrary", "arbitrary"), vmem_limit_bytes=_vmem_limit(est)),
        name="moe_down",
    )(it_tile, it_k, it_slot, it_exp, it_out, n_items.reshape(1), it_run, it_nxt, n_runs,
      hmid, wd, gate_b)


def _next_run_start(first):
    n = first.shape[0]
    idx = jnp.where(first, jnp.arange(n, dtype=jnp.int32), n)
    at_or_after = lax.cummin(idx[::-1])[::-1]
    nxt = jnp.concatenate([at_or_after[1:], jnp.full((1,), n, jnp.int32)])
    return jnp.where(nxt >= n, -1, nxt).astype(jnp.int32)


def _moe_up(xs, wg, wu, li, tile_expert, n_used, tm):
    rows = xs.shape[0]
    D, F = wg.shape[2], wg.shape[3]
    n_tiles = rows // tm
    tn = _pick_tile(F, 512, LANES)
    nn = F // tn

    t = jnp.arange(n_tiles, dtype=jnp.int32)
    prev_e = jnp.concatenate([jnp.full((1,), -1, jnp.int32), tile_expert[:-1]])
    first = (t < n_used) & (tile_expert != prev_e)
    rank = (jnp.cumsum(first.astype(jnp.int32)) - 1).astype(jnp.int32)
    n_runs = jnp.sum(first.astype(jnp.int32)).reshape(1)
    nxt = _next_run_start(first)

    def body(te, nu, first_ref, rank_ref, nxt_ref, nruns_ref, x_ref, wg_hbm, wu_hbm, o_ref, wbuf, sem):
        n, m = pl.program_id(0), pl.program_id(1)

        def copies(e, col_tile, slot):
            cols = pl.ds(pl.multiple_of(col_tile * tn, tn), tn)
            return [pltpu.make_async_copy(w.at[li, e, :, cols], wbuf.at[slot, i], sem.at[slot, i])
                    for i, w in enumerate((wg_hbm, wu_hbm))]

        @pl.when(m < nu[0])
        def _():
            slot = (n * nruns_ref[0] + rank_ref[m]) % 2

            @pl.when(first_ref[m] == 1)
            def _():
                @pl.when((n == 0) & (m == 0))
                def _():
                    for c in copies(te[0], 0, 0):
                        c.start()

                for c in copies(te[m], n, slot):
                    c.wait()
                nx = nxt_ref[m]

                @pl.when(nx >= 0)
                def _():
                    for c in copies(te[jnp.maximum(nx, 0)], n, 1 - slot):
                        c.start()

                @pl.when((nx < 0) & (n + 1 < nn))
                def _():
                    for c in copies(te[0], n + 1, 1 - slot):
                        c.start()

            v = x_ref[...]
            lo = lax.bitcast_convert_type(v << 16, F32)
            hi = lax.bitcast_convert_type(v & jnp.uint32(0xFFFF0000), F32)
            xb = jnp.concatenate([lo, hi], axis=1).astype(BF16)
            gate = jnp.dot(xb, wbuf[slot, 0].astype(BF16), preferred_element_type=F32)
            up = jnp.dot(xb, wbuf[slot, 1].astype(BF16), preferred_element_type=F32)
            o_ref[...] = (gate * _sigmoid(gate) * up).astype(BF16)

    est = (2 * _nbytes((tm, D // 2), jnp.uint32) + 3 * _nbytes((tm, D), BF16) + 4 * _nbytes((D, tn), F32)
           + 2 * _nbytes((D, tn), BF16) + 2 * _nbytes((tm, tn), BF16) + 3 * _nbytes((tm, tn), F32))
    return pl.pallas_call(
        body,
        grid_spec=pltpu.PrefetchScalarGridSpec(
            num_scalar_prefetch=6, grid=(nn, n_tiles),
            in_specs=[pl.BlockSpec((tm, D // 2), lambda n, m, *_: (m, 0)),
                      pl.BlockSpec(memory_space=pl.ANY), pl.BlockSpec(memory_space=pl.ANY)],
            out_specs=pl.BlockSpec((tm, tn), lambda n, m, *_: (m, n)),
            scratch_shapes=[pltpu.VMEM((2, 2, D, tn), F32), pltpu.SemaphoreType.DMA((2, 2))]),
        out_shape=jax.ShapeDtypeStruct((rows, F), BF16),
        compiler_params=pltpu.CompilerParams(
            dimension_semantics=("arbitrary", "arbitrary"), vmem_limit_bytes=_vmem_limit(est)),
        name="moe_up",
    )(tile_expert, n_used.reshape(1), first.astype(jnp.int32), rank, nxt, n_runs, xs, wg, wu)


def _moe(x_all, n_real, g, w_router, wg, wu, wd, li):
    Mp, D = x_all.shape
    E, _, F = wg.shape[1:]
    tm = MOE_ROW_TILE
    h, route = _moe_route(x_all, g, w_router[li])

    r = route[:n_real]
    e_flat = r[:, :TOP_K].astype(jnp.int32).reshape(-1)
    gates = r[:, TOP_K:2 * TOP_K].reshape(-1)
    n_pairs = n_real * TOP_K
    onehot = (e_flat[:, None] == jnp.arange(E)[None, :]).astype(jnp.int32)
    counts = jnp.sum(onehot, axis=0)
    within = jnp.take_along_axis(jnp.cumsum(onehot, axis=0) - onehot, e_flat[:, None], axis=1)[:, 0]
    padded = (counts + tm - 1) // tm * tm
    ends = jnp.cumsum(padded)
    dest = (ends - padded)[e_flat] + within
    n_tiles = -(-n_pairs // tm) + E
    rows = n_tiles * tm
    src_tok = jnp.zeros((rows,), jnp.int32).at[dest].set(jnp.arange(n_pairs, dtype=jnp.int32) // TOP_K)
    row_gate = jnp.zeros((rows,), F32).at[dest].set(gates)
    n_used = (ends[-1] // tm).astype(jnp.int32)
    tile_ids = jnp.minimum(jnp.arange(n_tiles, dtype=jnp.int32), n_used - 1)
    tile_expert = jnp.minimum(jnp.searchsorted(ends, tile_ids * tm, side="right"), E - 1).astype(jnp.int32)
    xs = jnp.take(h, src_tok, axis=0, mode="clip")
    hmid = _moe_up(xs, wg, wu, li, tile_expert, n_used, tm)

    gate_b = jnp.broadcast_to(row_gate[:, None], (rows, LANES))
    y = _moe_down(hmid, wd, li, gate_b, tile_expert, n_used, tm)

    d2 = dest.reshape(n_real, TOP_K)
    y_tok = jnp.take(y, d2[:, 0], axis=0, mode="clip")
    for k in range(1, TOP_K):
        y_tok = y_tok + jnp.take(y, d2[:, k], axis=0, mode="clip")
    return x_all.at[:n_real].add(y_tok)


def kernel(x_prompt, x_sample, cache_cmp, cache_sel, cache_win, state_ssm_re, state_ssm_im, page_table,
           norm_mix, w_in, ssm_a_re, ssm_a_im, ssm_b_re, ssm_b_im, ssm_c_re, ssm_c_im, ssm_d, ssm_log_dt,
           ssm_w_glu, ssm_b_glu, cmp_pe, cmp_w1, cmp_b1, cmp_w2, w_br_ssm, w_br_nsa, w_out,
           norm_ffn, ffn_w_gate, ffn_w_up, ffn_w_down, moe_router, moe_w_gate, moe_w_up, moe_w_down,
           norm_final):
    B, T, D = x_prompt.shape
    S = x_sample.shape[0]
    assert x_sample.shape[1] == 1
    depth = w_in.shape[0]
    n_prompt = B * T
    n_real = n_prompt + S
    Mp = -(-n_real // ROW_PAD) * ROW_PAD
    n_tail = Mp - n_prompt
    d_ssm = ssm_d.shape[1]
    d_nsa = N_HEADS * HEAD_DIM
    G, P = ssm_a_re.shape[1], ssm_a_re.shape[2]
    o_kv = d_ssm + d_nsa
    o_gl = o_kv + 3 * KV_COLS
    o_g = o_gl + 3 * N_HEADS
    assert d_ssm % (Q_PER_KV * HEAD_DIM) == 0 and w_in.shape[2] == o_g + 2 * D

    tm_big = _pick_tile(Mp, 1376, SUBLANES_BF16)
    tm_mid = _pick_tile(Mp, 688, SUBLANES_BF16)
    tn = 512

    x_all = jnp.concatenate([x_prompt.reshape(n_prompt, D), x_sample.reshape(S, D),
                             jnp.zeros((Mp - n_real, D), F32)], axis=0)
    w_in_t = jnp.swapaxes(w_in, 1, 2)
    pad_tail = lambda a: jnp.pad(a.reshape(S, -1), ((0, n_tail - S), (0, 0)))

    outs = {k: [] for k in ("cmp_p", "sel_p", "win_p", "sre_p", "sim_p",
                            "cmp_s", "sel_s", "win_s", "sre_s", "sim_s")}
    for l in range(depth):
        h = _rmsnorm(x_all, norm_mix[l], BF16, "norm_mix")
        uq = _dense(h, w_in_t, (l,), 0, o_kv, tm=tm_big, tn=tn, w_t=True, name="in_uq")
        kvc, kvs, kvw = [_dense(h, w_in_t, (l,), o_kv + i * KV_COLS, KV_COLS, tm=tm_big, tn=KV_COLS,
                                w_t=True, name=f"in_kv{i}") for i in range(3)]
        glg = _dense(h, w_in_t, (l,), o_gl, LANES, tm=tm_big, tn=LANES, w_t=True, name="in_gate_logits")
        gates = _dense_t_unaligned(h, w_in_t, (l,), o_g, 2 * D, tm=tm_big, tn=256, epilogue=_sigmoid,
                                   out_dtype=F32, name="in_merge_gates")

        s0 = [pad_tail(s[l]) for s in (state_ssm_re, state_ssm_im)]
        ssm_params = (ssm_a_re, ssm_a_im, ssm_b_re, ssm_b_im, ssm_c_re, ssm_c_im, ssm_d, ssm_log_dt,
                      ssm_w_glu, ssm_b_glu)
        ssm_out, fin_p, fin_s = _s5_branch(uq, d_ssm, s0[0], s0[1], n_prompt, B, T, ssm_params, l, tm_mid)

        kd = CMP_STRIDE * HEAD_DIM
        pe = cmp_pe[l].reshape(2, 2, 1, kd)
        w1 = cmp_w1[l].reshape(2, 2, kd, HEAD_DIM)
        n_chunk = T // CMP_STRIDE
        rb = _pick_tile(n_prompt // CMP_STRIDE, 128, SUBLANES_F32)
        lt = _compress_lead_trail(
            [(kvc.reshape(Mp // CMP_STRIDE, CHUNK_COLS), (rb, CHUNK_COLS), lambda a, s: (s, 0))],
            (1, n_prompt // CMP_STRIDE // rb), pe, w1, rb, n_prompt // CMP_STRIDE, name="cmp_lt_prompt")
        cmp_kv = _compress_finish(lt, B, n_chunk, cmp_b1[l], cmp_w2[l], "cmp_finish_prompt")
        if n_chunk < LANES:
            cmp_kv = jnp.pad(cmp_kv, ((0, 0), (0, 0), (0, 0), (0, LANES - n_chunk), (0, 0)))
        q_blk0 = d_ssm // (Q_PER_KV * HEAD_DIM)
        nsa_out = _nsa_prompt(uq, q_blk0, glg, cmp_kv, kvs, kvw, n_seq=B, seq_len=T, out_rows=Mp)

        tail = lambda a, c0, c1: a[n_prompt:n_real, c0:c1]
        q_s = tail(uq, d_ssm, o_kv).reshape(S, N_KV_HEADS, Q_PER_KV, HEAD_DIM)
        gl_s = tail(glg, 0, 3 * N_HEADS).reshape(S, N_KV_HEADS, Q_PER_KV, 3)
        gate_s = jnp.pad(gl_s, ((0, 0), (0, 0), (0, 0), (0, LANES - 3)))
        new_sel = tail(kvs, 0, KV_COLS).reshape(S, 2 * N_KV_HEADS, 1, HEAD_DIM)
        new_win = tail(kvw, 0, KV_COLS).reshape(S, 2 * N_KV_HEADS, 1, HEAD_DIM)
        nsa_s = _nsa_sample(q_s, gate_s, new_sel, new_win, cache_cmp, cache_sel, cache_win, page_table, l,
                            pe, w1, cmp_b1[l], cmp_w2[l])
        nsa_out = lax.dynamic_update_slice(nsa_out, pad_tail(nsa_s).astype(BF16), (n_prompt, 0))

        ng = D // tn
        y = _fused_matmul(
            xs=[(ssm_out, lambda m, n, k, *pf: (m, 0)), (nsa_out, lambda m, n, k, *pf: (m, 0))],
            dots=[(0, w_br_ssm, (None, d_ssm, tn), lambda m, n, k, *pf: (l, 0, n)),
                  (1, w_br_nsa, (None, d_nsa, tn), lambda m, n, k, *pf: (l, 0, n))],
            extras=[(gates, (tm_mid, tn), lambda m, n, k, *pf: (m, n)),
                    (gates, (tm_mid, tn), lambda m, n, k, *pf: (m, ng + n))],
            epilogue=lambda p, e: e[0] * p[0] + e[1] * p[1],
            M=Mp, N=D, K=d_ssm, tm=tm_mid, tn=tn, tk=d_ssm, out_dtype=BF16, name="merge")
        x_all = _dense(y, w_out, (l,), 0, D, tm=tm_big, tn=tn,
                       extras=[(x_all, (tm_big, tn), lambda m, n, k, *pf: (m, n))],
                       epilogue=lambda p, e: e[0] + p[0], name="out_proj")

        kv5 = lambda a, rows, lead: a[rows].reshape(lead + (2, N_KV_HEADS, HEAD_DIM))
        outs["cmp_p"].append(kv5(kvc, slice(0, n_prompt), (B, T)))
        outs["sel_p"].append(kv5(kvs, slice(0, n_prompt), (B, T)))
        wp = min(WINDOW, T)
        outs["win_p"].append(kv5(kvw, slice(0, n_prompt), (B, T))[:, T - wp:])
        outs["sre_p"].append(fin_p[0].reshape(B, G, P))
        outs["sim_p"].append(fin_p[1].reshape(B, G, P))
        outs["cmp_s"].append(kv5(kvc, slice(n_prompt, n_real), (S, 1)))
        outs["sel_s"].append(kv5(kvs, slice(n_prompt, n_real), (S, 1)))
        win = jnp.concatenate([cache_win[l], kv5(kvw, slice(n_prompt, n_real), (S, 1))], axis=1)
        outs["win_s"].append(win[:, 1:])
        outs["sre_s"].append(fin_s[0][:S].reshape(S, G, P))
        outs["sim_s"].append(fin_s[1][:S].reshape(S, G, P))

        if l % 2 == 0:
            i = l // 2
            h2 = _rmsnorm(x_all, norm_ffn[l], BF16, "norm_ffn")
            f = ffn_w_gate.shape[2]
            tnf = _pick_tile(f, 256, LANES)
            wmap = lambda m, n, k, *pf: (i, 0, n)
            hmid = _fused_matmul(
                xs=[(h2, lambda m, n, k, *pf: (m, 0))],
                dots=[(0, ffn_w_gate, (None, D, tnf), wmap), (0, ffn_w_up, (None, D, tnf), wmap)],
                epilogue=lambda p, e: p[0] * _sigmoid(p[0]) * p[1],
                M=Mp, N=f, K=D, tm=tm_big, tn=tnf, tk=D, out_dtype=BF16, name="ffn_up")
            tk = _pick_tile(f, 1024, LANES)
            tnd = _pick_tile(D, 1024, LANES)
            x_all = _fused_matmul(
                xs=[(hmid, lambda m, n, k, *pf: (m, k))],
                dots=[(0, ffn_w_down, (None, tk, tnd), lambda m, n, k, *pf: (i, k, n))],
                extras=[(x_all, (tm_big, tnd), lambda m, n, k, *pf: (m, n))],
                epilogue=lambda p, e: e[0] + p[0],
                M=Mp, N=D, K=f, tm=tm_big, tn=tnd, tk=tk, out_dtype=F32, name="ffn_down")
        else:
            x_all = _moe(x_all, n_real, norm_ffn[l], moe_router, moe_w_gate, moe_w_up, moe_w_down, l // 2)

    y_p = _rmsnorm(x_all, norm_final, F32, "norm_final", rows=n_prompt)
    y_s = _rmsnorm(x_all[n_prompt:], norm_final, F32, "norm_final_tail")
    st = lambda k: jnp.stack(outs[k])
    return (y_p.reshape(B, T, D), y_s[:S].reshape(S, 1, D),
            st("cmp_p"), st("sel_p"), st("win_p"), st("sre_p"), st("sim_p"),
            st("cmp_s"), st("sel_s"), st("win_s"), st("sre_s"), st("sim_s"))
```

```python
import functools
import math

import numpy as np
import jax
import jax.numpy as jnp
from jax import lax
from jax.experimental import pallas as pl
from jax.experimental.pallas import tpu as pltpu

F32 = jnp.float32
BF16 = jnp.bfloat16

SSM_GROUP = 16
N_HEADS = 16
HEAD_DIM = 128
N_KV_HEADS = 2
Q_PER_KV = N_HEADS // N_KV_HEADS
CMP_STRIDE = 16
CMP_BLOCK = 2 * CMP_STRIDE
SEL_BLOCK = 64
N_SEL = 16
N_LOCAL_SEL = 2
WINDOW = 512
Q_BLOCK = 128
TOP_K = 2
RMS_EPS = 1e-6
NEG_INF = -1e30
BIG = 1e30

LANES = 128
SUBLANES_F32 = 8
SUBLANES_BF16 = 16
VMEM_BYTES_V7X = 64 * 1024 * 1024
VMEM_LIMIT_CAP = VMEM_BYTES_V7X - 6 * 1024 * 1024
ROW_PAD = 64


def _vmem_limit(estimate_bytes):
    return int(min(max(estimate_bytes * 5 // 4 + (4 << 20), 16 << 20), VMEM_LIMIT_CAP))


def _pick_tile(total, target, mult):
    best = None
    for t in range(mult, min(total, target) + 1, mult):
        if total % t == 0:
            best = t
    assert best is not None, (total, target, mult)
    return best


def _nbytes(shape, dtype):
    n = 1
    for s in shape:
        if s is not None:
            n *= s
    return n * jnp.dtype(dtype).itemsize


def _gelu_tanh(x):
    return 0.5 * x * (1.0 + jnp.tanh(math.sqrt(2.0 / math.pi) * (x + 0.044715 * (x * x * x))))


def _sigmoid(x):
    return 1.0 / (1.0 + jnp.exp(-x))


def _fused_matmul(*, xs, dots, extras=(), epilogue, M, N, K, tm, tn, tk, out_dtype, name):
    assert M % tm == 0 and N % tn == 0 and K % tk == 0, (M, N, K, tm, tn, tk)
    nm, nn, nk = M // tm, N // tn, K // tk
    nd, nx, ne = len(dots), len(xs), len(extras)
    use_acc = nk > 1

    def body(*refs):
        x_refs = refs[:nx]
        w_refs = refs[nx:nx + nd]
        e_refs = refs[nx + nd:nx + nd + ne]
        o_ref = refs[nx + nd + ne]
        acc_refs = refs[nx + nd + ne + 1:]
        xv = [xr[...].astype(BF16) for xr in x_refs]
        parts = []
        for i, d in enumerate(dots):
            wv = w_refs[i][...].astype(BF16)
            if len(d) > 4 and d[4]:
                parts.append(lax.dot_general(xv[d[0]], wv, _NT, preferred_element_type=F32))
            else:
                parts.append(jnp.dot(xv[d[0]], wv, preferred_element_type=F32))
        if not use_acc:
            o_ref[...] = epilogue(parts, [e[...] for e in e_refs]).astype(out_dtype)
            return
        k = pl.program_id(2)

        @pl.when(k == 0)
        def _():
            for i in range(nd):
                acc_refs[i][...] = parts[i]

        @pl.when(k > 0)
        def _():
            for i in range(nd):
                acc_refs[i][...] += parts[i]

        @pl.when(k == nk - 1)
        def _():
            o_ref[...] = epilogue([a[...] for a in acc_refs], [e[...] for e in e_refs]).astype(out_dtype)

    in_specs = [pl.BlockSpec((tm, tk), f) for _, f in xs]
    args = [a for a, _ in xs]
    est = sum(2 * _nbytes((tm, tk), a.dtype) for a, _ in xs)
    for d in dots:
        w, bs, f = d[1:4]
        in_specs.append(pl.BlockSpec(bs, f))
        args.append(w)
        est += 2 * _nbytes(bs, w.dtype) + _nbytes(bs, BF16)
    for a, bs, f in extras:
        in_specs.append(pl.BlockSpec(bs, f))
        args.append(a)
        est += 2 * _nbytes(bs, a.dtype)
    est += 2 * _nbytes((tm, tn), out_dtype) + (2 + nd) * _nbytes((tm, tn), F32)
    scratch = [pltpu.VMEM((tm, tn), F32) for _ in range(nd)] if use_acc else []
    return pl.pallas_call(
        body, grid=(nm, nn, nk), in_specs=in_specs,
        out_specs=pl.BlockSpec((tm, tn), lambda m, n, k: (m, n)), scratch_shapes=scratch,
        out_shape=jax.ShapeDtypeStruct((M, N), out_dtype),
        compiler_params=pltpu.CompilerParams(
            dimension_semantics=("parallel", "parallel", "arbitrary"),
            vmem_limit_bytes=_vmem_limit(est)),
        name=name,
    )(*args)


def _first(parts, extras):
    return parts[0]


def _dense(x, w, w_lead, col0, N, *, tm, tn, epilogue=_first, extras=(), out_dtype=F32, w_t=False, name):
    M, K = x.shape
    assert col0 % tn == 0
    c0 = col0 // tn
    nl = len(w_lead)
    if w_t:
        dot = (0, w, (None,) * nl + (tn, K), lambda m, n, k, *pf: tuple(w_lead) + (c0 + n, 0), True)
    else:
        dot = (0, w, (None,) * nl + (K, tn), lambda m, n, k, *pf: tuple(w_lead) + (0, c0 + n))
    return _fused_matmul(
        xs=[(x, lambda m, n, k, *pf: (m, 0))],
        dots=[dot],
        extras=extras, epilogue=epilogue, M=M, N=N, K=K, tm=tm, tn=tn, tk=K,
        out_dtype=out_dtype, name=name)


def _dense_t_unaligned(x, w, w_lead, row0, N, *, tm, tn, epilogue, out_dtype, name):
    M, K = x.shape
    sh = row0 % tn
    assert sh % SUBLANES_F32 == 0 and N % tn == 0 and M % tm == 0
    b0 = row0 // tn
    nl = len(w_lead)

    def body(x_ref, wa_ref, wb_ref, o_ref):
        wv = jnp.concatenate([wa_ref[sh:, :], wb_ref[:sh, :]], axis=0).astype(BF16)
        acc = lax.dot_general(x_ref[...], wv, _NT, preferred_element_type=F32)
        o_ref[...] = epilogue(acc).astype(out_dtype)

    wspec = lambda off: pl.BlockSpec((None,) * nl + (tn, K), lambda m, n: tuple(w_lead) + (b0 + off + n, 0))
    est = (2 * _nbytes((tm, K), x.dtype) + 4 * _nbytes((tn, K), F32) + 2 * _nbytes((tn, K), BF16)
           + 4 * _nbytes((tm, tn), F32))
    return pl.pallas_call(
        body, grid=(M // tm, N // tn),
        in_specs=[pl.BlockSpec((tm, K), lambda m, n: (m, 0)), wspec(0), wspec(1)],
        out_specs=pl.BlockSpec((tm, tn), lambda m, n: (m, n)),
        out_shape=jax.ShapeDtypeStruct((M, N), out_dtype),
        compiler_params=pltpu.CompilerParams(
            dimension_semantics=("parallel", "parallel"), vmem_limit_bytes=_vmem_limit(est)),
        name=name,
    )(x, w, w)


def _rmsnorm(x, g, out_dtype, name, rows=None):
    M, D = x.shape
    M = M if rows is None else rows
    tr = _pick_tile(M, 256, SUBLANES_BF16)

    def body(x_ref, g_ref, o_ref):
        xf = x_ref[...]
        ms = jnp.mean(xf * xf, axis=-1, keepdims=True)
        o_ref[...] = (xf * lax.rsqrt(ms + RMS_EPS) * g_ref[...]).astype(out_dtype)

    return pl.pallas_call(
        body, grid=(M // tr,),
        in_specs=[pl.BlockSpec((tr, D), lambda i: (i, 0)), pl.BlockSpec((1, D), lambda i: (0, 0))],
        out_specs=pl.BlockSpec((tr, D), lambda i: (i, 0)),
        out_shape=jax.ShapeDtypeStruct((M, D), out_dtype),
        compiler_params=pltpu.CompilerParams(dimension_semantics=("parallel",)),
        name=name,
    )(x, g.reshape(1, D))


def _s5_discretize(a_re, a_im, log_dt_b, b_re_t, b_im_t):
    C, GP = b_re_t.shape

    def body(ar_ref, ai_ref, ld_ref, br_ref, bi_ref, lr_ref, li_ref, obr_ref, obi_ref):
        ar, ai = ar_ref[...], ai_ref[...]
        dt = jnp.exp(ld_ref[...])
        mag = jnp.exp(ar * dt)
        ang = ai * dt
        lr = mag * jnp.cos(ang)
        li = mag * jnp.sin(ang)
        pr, pi = lr, li
        for k in range(SUBLANES_F32):
            lr_ref[k:k + 1, :] = pr
            li_ref[k:k + 1, :] = pi
            pr, pi = pr * lr - pi * li, pr * li + pi * lr
        x, y = lr - 1.0, li
        den = ar * ar + ai * ai
        cr = (x * ar + y * ai) / den
        ci = (y * ar - x * ai) / den
        br, bi = br_ref[...], bi_ref[...]
        obr_ref[...] = cr * br - ci * bi
        obi_ref[...] = cr * bi + ci * br

    vec = jax.ShapeDtypeStruct((SUBLANES_F32, GP), F32)
    mat = jax.ShapeDtypeStruct((C, GP), F32)
    return pl.pallas_call(body, out_shape=(vec, vec, mat, mat), name="s5_discretize")(
        a_re, a_im, log_dt_b, b_re_t, b_im_t)


GROUP_TILE = 16
S5_IN = GROUP_TILE * SSM_GROUP


def _s5_prompt(uq, w_b, w_c, pw_re, pw_im, d2, *, n_seq, seq_len, out_rows):
    ngt, kx, ns = w_b.shape[1:]
    sub = SUBLANES_F32
    tc = _pick_tile(seq_len, 256, sub)
    nch = seq_len // tc
    ntile = tc // sub

    def body(u_ref, wb_ref, wc_ref, pr_ref, pi_ref, d_ref, z_ref, fr_ref, fi_ref, st, sr_ref, si_ref):
        c = pl.program_id(2)

        @pl.when(c == 0)
        def _():
            st[...] = jnp.zeros(st.shape, F32)

        u = u_ref[...]
        ub = u.astype(BF16)
        xr = jnp.dot(ub, wb_ref[0], preferred_element_type=F32).reshape(ntile, sub, ns)
        xi = jnp.dot(ub, wb_ref[1], preferred_element_type=F32).reshape(ntile, sub, ns)
        pr, pi = pr_ref[...], pi_ref[...]
        row = lax.broadcasted_iota(jnp.int32, (sub, ns), 0)
        d = 1
        while d < sub:
            mr = jnp.where(row >= d, pr[d - 1:d, :], 0.0)
            mi = jnp.where(row >= d, pi[d - 1:d, :], 0.0)
            rr = pltpu.roll(xr, d, axis=1)
            ri = pltpu.roll(xi, d, axis=1)
            xr, xi = xr + mr * rr - mi * ri, xi + mr * ri + mi * rr
            d *= 2
        sr_ref[...] = xr
        si_ref[...] = xi

        def tile(j, carry):
            cr, ci = carry
            nr = sr_ref[j] + pr * cr - pi * ci
            ni = si_ref[j] + pr * ci + pi * cr
            sr_ref[j] = nr
            si_ref[j] = ni
            return nr[sub - 1:sub, :], ni[sub - 1:sub, :]

        cr, ci = lax.fori_loop(0, ntile, tile, (st[0:1, :], st[1:2, :]))
        st[0:1, :] = cr
        st[1:2, :] = ci
        fr_ref[...] = cr
        fi_ref[...] = ci
        y = (jnp.dot(sr_ref[...].reshape(tc, ns).astype(BF16), wc_ref[0], preferred_element_type=F32)
             + jnp.dot(si_ref[...].reshape(tc, ns).astype(BF16), wc_ref[1], preferred_element_type=F32))
        z_ref[...] = _gelu_tanh(y + d_ref[...] * u)

    u_cols = ngt * kx
    fin = jax.ShapeDtypeStruct((n_seq, ngt, 1, ns), F32)
    return pl.pallas_call(
        body, grid=(n_seq, ngt, nch),
        in_specs=[pl.BlockSpec((tc, kx), lambda b, g, c: (b * nch + c, g)),
                  pl.BlockSpec((2, None, kx, ns), lambda b, g, c: (0, g, 0, 0)),
                  pl.BlockSpec((2, None, ns, kx), lambda b, g, c: (0, g, 0, 0)),
                  pl.BlockSpec((sub, ns), lambda b, g, c: (0, g)),
                  pl.BlockSpec((sub, ns), lambda b, g, c: (0, g)),
                  pl.BlockSpec((1, kx), lambda b, g, c: (0, g))],
        out_specs=[pl.BlockSpec((tc, kx), lambda b, g, c: (b * nch + c, g)),
                   pl.BlockSpec((None, None, 1, ns), lambda b, g, c: (b, g, 0, 0)),
                   pl.BlockSpec((None, None, 1, ns), lambda b, g, c: (b, g, 0, 0))],
        out_shape=(jax.ShapeDtypeStruct((out_rows, u_cols), F32), fin, fin),
        scratch_shapes=[pltpu.VMEM((sub, ns), F32), pltpu.VMEM((ntile, sub, ns), F32),
                        pltpu.VMEM((ntile, sub, ns), F32)],
        compiler_params=pltpu.CompilerParams(
            dimension_semantics=("parallel", "parallel", "arbitrary")),
        name="s5_prompt",
    )(uq, w_b, w_c, pw_re, pw_im, d2)


def _s5_tail(uq, w_b, w_c, pw_re, pw_im, d2, s0_re, s0_im, z_prev, *, row0):
    ngt, kx, ns = w_b.shape[1:]
    n_tail = s0_re.shape[0]
    assert row0 % n_tail == 0
    rb = row0 // n_tail

    def body(u_ref, wb_ref, wc_ref, pr_ref, pi_ref, d_ref, s0r_ref, s0i_ref, zp_ref, z_ref, fr_ref, fi_ref):
        u = u_ref[...]
        ub = u.astype(BF16)
        lr, li = pr_ref[0:1, :], pi_ref[0:1, :]
        s0r, s0i = s0r_ref[...], s0i_ref[...]
        sr = lr * s0r - li * s0i + jnp.dot(ub, wb_ref[0], preferred_element_type=F32)
        si = lr * s0i + li * s0r + jnp.dot(ub, wb_ref[1], preferred_element_type=F32)
        fr_ref[...] = sr
        fi_ref[...] = si
        y = (jnp.dot(sr.astype(BF16), wc_ref[0], preferred_element_type=F32)
             + jnp.dot(si.astype(BF16), wc_ref[1], preferred_element_type=F32))
        z_ref[...] = _gelu_tanh(y + d_ref[...] * u)

    st = jax.ShapeDtypeStruct(s0_re.shape, F32)
    return pl.pallas_call(
        body, grid=(ngt,),
        in_specs=[pl.BlockSpec((n_tail, kx), lambda g: (rb, g)),
                  pl.BlockSpec((2, None, kx, ns), lambda g: (0, g, 0, 0)),
                  pl.BlockSpec((2, None, ns, kx), lambda g: (0, g, 0, 0)),
                  pl.BlockSpec((SUBLANES_F32, ns), lambda g: (0, g)),
                  pl.BlockSpec((SUBLANES_F32, ns), lambda g: (0, g)),
                  pl.BlockSpec((1, kx), lambda g: (0, g)),
                  pl.BlockSpec((n_tail, ns), lambda g: (0, g)),
                  pl.BlockSpec((n_tail, ns), lambda g: (0, g)),
                  pl.BlockSpec(memory_space=pl.ANY)],
        out_specs=[pl.BlockSpec((n_tail, kx), lambda g: (rb, g)),
                   pl.BlockSpec((n_tail, ns), lambda g: (0, g)),
                   pl.BlockSpec((n_tail, ns), lambda g: (0, g))],
        out_shape=(jax.ShapeDtypeStruct(z_prev.shape, F32), st, st),
        input_output_aliases={8: 0},
        compiler_params=pltpu.CompilerParams(dimension_semantics=("parallel",)),
        name="s5_tail",
    )(uq, w_b, w_c, pw_re, pw_im, d2, s0_re, s0_im, z_prev)


def _block_diag_tiles(w):
    eye = jnp.eye(GROUP_TILE, dtype=w.dtype)
    nt, gt, a, b = w.shape
    return jnp.einsum("tgab,gh->tgahb", w, eye).reshape(nt, gt * a, gt * b)


def _s5_branch(uq, u_cols, s0_re, s0_im, n_prompt, n_seq, seq_len, params, layer, tm):
    (a_re, a_im, b_re, b_im, c_re, c_im, d, log_dt, w_glu, b_glu) = params
    Mp = uq.shape[0]
    G, P = a_re.shape[1], a_re.shape[2]
    C = SSM_GROUP
    GP = G * P
    ngt = G // GROUP_TILE

    lam_re, lam_im, bb_re, bb_im = _s5_discretize(
        a_re[layer].reshape(1, GP), a_im[layer].reshape(1, GP),
        jnp.broadcast_to(log_dt[layer][:, None], (G, P)).reshape(1, GP),
        b_re[layer].transpose(2, 0, 1).reshape(C, GP), b_im[layer].transpose(2, 0, 1).reshape(C, GP))

    def b_tiles(bb):
        t = bb.reshape(C, ngt, GROUP_TILE, P).transpose(1, 2, 0, 3)
        return _block_diag_tiles(t)

    w_b =jnp.stack([b_tiles(bb_re), b_tiles(bb_im)]).astype(BF16)

    def c_tiles(cc):
        t = cc.reshape(ngt, GROUP_TILE, C, P).transpose(0, 1, 3, 2)
        return _block_diag_tiles(t)

    w_c = jnp.stack([c_tiles(c_re[layer]), -c_tiles(c_im[layer])]).astype(BF16)

    d2 = d[layer].reshape(1, u_cols)
    z, fp_re, fp_im = _s5_prompt(uq, w_b, w_c, lam_re, lam_im, d2, n_seq=n_seq, seq_len=seq_len,
                                 out_rows=Mp)
    z, fs_re, fs_im = _s5_tail(uq, w_b, w_c, lam_re, lam_im, d2, s0_re, s0_im, z, row0=n_prompt)

    tn = _pick_tile(u_cols, 512, LANES)
    ssm_out = _dense(z, w_glu, (layer,), 0, u_cols, tm=tm, tn=tn,
                     extras=[(z, (tm, tn), lambda m, n, k, *pf: (m, n)),
                             (b_glu[layer].reshape(1, u_cols), (1, tn), lambda m, n, k, *pf: (0, n))],
                     epilogue=lambda p, e: e[0] * _sigmoid(p[0] + e[1]),
                     out_dtype=BF16, name="s5_glu")
    return ssm_out, (fp_re, fp_im), (fs_re, fs_im)


KV_COLS = 2 * N_KV_HEADS * HEAD_DIM
CHUNK_COLS = CMP_STRIDE * KV_COLS


KV_ROWS = 2 * N_KV_HEADS


def _compress_lead_trail(srcs, grid, pe, w1, rows_per_src, n_rows, prefetch=(), token_major=False,
                         name="cmp_lt"):
    ns = len(srcs)
    npf = len(prefetch)
    rows = ns * rows_per_src
    kd = CMP_STRIDE * HEAD_DIM

    def piece(xr, j, c):
        if token_major:
            return xr[pl.ds(j * KV_ROWS + c, rows_per_src, stride=CMP_STRIDE * KV_ROWS), :]
        return xr[:, j * KV_COLS + c * HEAD_DIM: j * KV_COLS + (c + 1) * HEAD_DIM]

    def body(*refs):
        x_refs = refs[npf:npf + ns]
        pe_ref, w_ref, o_ref = refs[npf + ns:]
        for kv in range(2):
            per_head = []
            for h in range(N_KV_HEADS):
                c = kv * N_KV_HEADS + h
                per_src = [jnp.concatenate([piece(xr, j, c) for j in range(CMP_STRIDE)], axis=1)
                           for xr in x_refs]
                per_head.append(per_src[0] if ns == 1 else jnp.concatenate(per_src, axis=0))
            xs = jnp.concatenate(per_head, axis=0)
            for lt in range(2):
                xv = (xs + pe_ref[kv, lt]).astype(BF16)
                r = jnp.dot(xv, w_ref[kv, lt].astype(BF16), preferred_element_type=F32)
                for h in range(N_KV_HEADS):
                    o_ref[kv, lt, h] = r[h * rows:(h + 1) * rows]

    in_specs = [pl.BlockSpec(bs, f) for _, bs, f in srcs]
    in_specs += [pl.BlockSpec((2, 2, 1, kd), lambda *a: (0, 0, 0, 0)),
                 pl.BlockSpec((2, 2, kd, HEAD_DIM), lambda *a: (0, 0, 0, 0))]
    step_of = lambda *a: a[0] * grid[1] + a[1]
    return pl.pallas_call(
        body,
        grid_spec=pltpu.PrefetchScalarGridSpec(
            num_scalar_prefetch=npf, grid=grid, in_specs=in_specs,
            out_specs=pl.BlockSpec((2, 2, N_KV_HEADS, rows, HEAD_DIM),
                                   lambda *a: (0, 0, 0, step_of(*a), 0))),
        out_shape=jax.ShapeDtypeStruct((2, 2, N_KV_HEADS, n_rows, HEAD_DIM), F32),
        compiler_params=pltpu.CompilerParams(
            dimension_semantics=("parallel", "arbitrary"), vmem_limit_bytes=48 << 20),
        name=name,
    )(*prefetch, *[a for a, _, _ in srcs], pe, w1)


def _compress_finish(lt, n_seq, n_chunk, b1, w2, name):
    lead = lt[:, 0]
    trail = lt[:, 1].reshape(2, N_KV_HEADS, n_seq, n_chunk, HEAD_DIM)
    trail = jnp.roll(trail, -1, axis=3).reshape(lead.shape)
    rows = N_KV_HEADS * n_seq * n_chunk
    lead = lead.reshape(2, rows, HEAD_DIM)
    trail = trail.reshape(2, rows, HEAD_DIM)
    rt = _pick_tile(rows, 1024, SUBLANES_F32)

    def body(a_ref, t_ref, b_ref, w_ref, o_ref):
        hid = _gelu_tanh(a_ref[...] + t_ref[...] + b_ref[...])
        o_ref[...] = jnp.dot(hid.astype(BF16), w_ref[...].astype(BF16), preferred_element_type=F32)

    out = pl.pallas_call(
        body, grid=(2, rows // rt),
        in_specs=[pl.BlockSpec((None, rt, HEAD_DIM), lambda kv, r: (kv, r, 0)),
                  pl.BlockSpec((None, rt, HEAD_DIM), lambda kv, r: (kv, r, 0)),
                  pl.BlockSpec((None, 1, HEAD_DIM), lambda kv, r: (kv, 0, 0)),
                  pl.BlockSpec((None, HEAD_DIM, HEAD_DIM), lambda kv, r: (kv, 0, 0))],
        out_specs=pl.BlockSpec((None, rt, HEAD_DIM), lambda kv, r: (kv, r, 0)),
        out_shape=jax.ShapeDtypeStruct((2, rows, HEAD_DIM), F32),
        compiler_params=pltpu.CompilerParams(dimension_semantics=("parallel", "parallel")),
        name=name,
    )(lead, trail, b1.reshape(2, 1, HEAD_DIM), w2)
    return out.reshape(2, N_KV_HEADS, n_seq, n_chunk, HEAD_DIM)


def _overlap_matrix(n_cmp, n_blk, rows, cols):
    start = np.arange(rows)[:, None] * CMP_STRIDE
    js = np.arange(cols)[None, :] * SEL_BLOCK
    ov = (start < js + SEL_BLOCK) & (start + CMP_BLOCK > js)
    ov &= (np.arange(rows)[:, None] < n_cmp) & (np.arange(cols)[None, :] < n_blk)
    return jnp.asarray(ov.astype(np.float32))


def _masked_softmax(s, valid):
    s = jnp.where(valid, s, NEG_INF)
    p = jnp.exp(s - jnp.max(s, axis=-1, keepdims=True)) * jnp.where(valid, 1.0, 0.0)
    return p * (1.0 / jnp.maximum(jnp.sum(p, axis=-1, keepdims=True), 1e-30))


def _softmax_dot(s, valid, v):
    hg, r, n = s.shape
    s = jnp.where(valid, s, NEG_INF)
    p = jnp.exp(s - jnp.max(s, axis=-1, keepdims=True))
    den = jnp.sum(p, axis=-1, keepdims=True)
    o = jnp.dot(p.reshape(hg * r, n).astype(BF16), v, preferred_element_type=F32)
    return o.reshape(hg, r, v.shape[1]) * (1.0 / den)


_NT = (((1,), (1,)), ((), ()))


def _nsa_prompt(uq, q_blk0, glg, cmp_kv, kvs, kvw, *, n_seq, seq_len, out_rows):
    T = seq_len
    QB = Q_BLOCK
    HG = Q_PER_KV
    HD = HEAD_DIM
    nqb = T // QB
    ncp = cmp_kv.shape[3]
    n_cmp = T // CMP_STRIDE - 1
    n_blk = -(-T // SEL_BLOCK)
    n_pick = min(N_SEL, n_blk)
    assert T % QB == 0 and T % SEL_BLOCK == 0 and n_blk <= LANES and ncp <= LANES
    assert n_pick > N_LOCAL_SEL
    CH = min(512, T)
    assert T % CH == 0
    WB = min(WINDOW + QB, T)
    scale = HD ** -0.5
    blk_of_key = np.arange(T) // SEL_BLOCK
    expand = (np.arange(LANES)[:, None] == blk_of_key[None, :]).astype(np.float32)
    expand = jnp.asarray(expand.reshape(LANES, T // CH, CH).transpose(1, 0, 2), dtype=BF16)
    ov = _overlap_matrix(n_cmp, n_blk, ncp, LANES)

    perm = np.zeros((N_KV_HEADS, LANES, LANES), np.float32)
    for hh in range(N_KV_HEADS):
        for g in range(HG):
            for j in range(3):
                perm[hh, (hh * HG + g) * 3 + j, j * HG + g] = 1.0
    perm = jnp.asarray(perm)

    def body(q_ref, gl_ref, perm_ref, kc_ref, vc_ref, ks_ref, vs_ref, kw_ref, vw_ref, e_ref, ov_ref, o_ref,
             m_ref, l_ref, acc_ref):
        i = pl.program_id(2)
        start = i * QB
        q = q_ref[...]
        q_st = jnp.concatenate([q[:, g * HD:(g + 1) * HD] for g in range(HG)], axis=0).astype(BF16)
        qpos = start + lax.broadcasted_iota(jnp.int32, (QB, 1), 0)

        s = lax.dot_general(q_st, kc_ref[...].astype(BF16), _NT, preferred_element_type=F32) * scale
        n_idx = lax.broadcasted_iota(jnp.int32, (QB, ncp), 1)
        cvalid = (n_idx * CMP_STRIDE + (CMP_BLOCK - 1) <= qpos) & (n_idx < n_cmp)
        p_c = _masked_softmax(s.reshape(HG, QB, ncp), cvalid)
        o_c = jnp.dot(p_c.reshape(HG * QB, ncp).astype(BF16), vc_ref[...].astype(BF16),
                      preferred_element_type=F32).reshape(HG, QB, HD)

        imp = jnp.dot(jnp.sum(p_c, axis=0), ov_ref[...], preferred_element_type=F32,
                      precision=lax.Precision.HIGHEST)
        blk = lax.broadcasted_iota(jnp.int32, (QB, LANES), 1)
        cur = qpos // SEL_BLOCK
        forced = (blk == 0) | ((blk <= cur) & (blk > cur - N_LOCAL_SEL))
        score = jnp.where(blk > cur, -BIG, jnp.where(forced, BIG, imp))
        rank = jnp.zeros((QB, LANES), F32)
        for j2 in range(n_blk):
            col = score[:, j2:j2 + 1]
            ahead = (col > score) | ((col == score) & (blk > j2))
            rank = rank + jnp.where(ahead, 1.0, 0.0)
        sel = jnp.where((rank < n_pick) & (blk < n_blk), 1.0, 0.0).astype(BF16)

        m_ref[...] = jnp.full(m_ref.shape, NEG_INF, F32)
        l_ref[...] = jnp.zeros(l_ref.shape, F32)
        acc_ref[...] = jnp.zeros(acc_ref.shape, F32)

        def chunk(c, carry):
            k0 = pl.multiple_of(c * CH, CH)
            kch = ks_ref[pl.ds(k0, CH), :].astype(BF16)
            vch = vs_ref[pl.ds(k0, CH), :].astype(BF16)
            sc = lax.dot_general(q_st, kch, _NT, preferred_element_type=F32) * scale
            picked = jnp.dot(sel, e_ref[c], preferred_element_type=F32)
            kpos = k0 + lax.broadcasted_iota(jnp.int32, (QB, CH), 1)
            valid = (picked > 0.5) & (kpos <= qpos)
            s3 = jnp.where(valid, sc.reshape(HG, QB, CH), NEG_INF)
            m_old = m_ref[...]
            m_new = jnp.maximum(m_old, jnp.max(s3, axis=-1, keepdims=True))
            p = jnp.exp(s3 - m_new)
            alpha = jnp.exp(m_old - m_new)
            l_ref[...] = alpha * l_ref[...] + jnp.sum(p, axis=-1, keepdims=True)
            pv = jnp.dot(p.reshape(HG * QB, CH).astype(BF16), vch, preferred_element_type=F32)
            acc_ref[...] = alpha * acc_ref[...] + pv.reshape(HG, QB, HD)
            m_ref[...] = m_new
            return carry

        lax.fori_loop(0, (start + QB + CH - 1) // CH, chunk, 0)
        o_s = acc_ref[...] * (1.0 / l_ref[...])

        w0 = pl.multiple_of(jnp.minimum(jnp.maximum(start - WINDOW, 0), T - WB), QB)
        sw = lax.dot_general(q_st, kw_ref[pl.ds(w0, WB), :].astype(BF16), _NT,
                             preferred_element_type=F32) * scale
        dist = qpos - (w0 + lax.broadcasted_iota(jnp.int32, (QB, WB), 1))
        o_w = _softmax_dot(sw.reshape(HG, QB, WB), (dist >= 0) & (dist < WINDOW),
                           vw_ref[pl.ds(w0, WB), :].astype(BF16))

        gate = _sigmoid(jnp.dot(gl_ref[...], perm_ref[...], preferred_element_type=F32,
                                precision=lax.Precision.HIGHEST))
        for g in range(HG):
            og = (gate[:, g:g + 1] * o_c[g] + gate[:, HG + g:HG + g + 1] * o_s[g]
                  + gate[:, 2 * HG + g:2 * HG + g + 1] * o_w[g])
            o_ref[:, g * HD:(g + 1) * HD] = og.astype(o_ref.dtype)

    row = lambda b, h, i: b * nqb + i
    kv_spec = lambda kv: pl.BlockSpec((T, HD), lambda b, h, i: (b, kv * N_KV_HEADS + h))
    return pl.pallas_call(
        body, grid=(n_seq, N_KV_HEADS, nqb),
        in_specs=[pl.BlockSpec((QB, HG * HD), lambda b, h, i: (row(b, h, i), q_blk0 + h)),
                  pl.BlockSpec((QB, LANES), lambda b, h, i: (row(b, h, i), 0)),
                  pl.BlockSpec((None, LANES, LANES), lambda b, h, i: (h, 0, 0)),
                  pl.BlockSpec((None, None, None, ncp, HD), lambda b, h, i: (0, h, b, 0, 0)),
                  pl.BlockSpec((None, None, None, ncp, HD), lambda b, h, i: (1, h, b, 0, 0)),
                  kv_spec(0), kv_spec(1), kv_spec(0), kv_spec(1),
                  pl.BlockSpec((T // CH, LANES, CH), lambda b, h, i: (0, 0, 0)),
                  pl.BlockSpec((ncp, LANES), lambda b, h, i: (0, 0))],
        out_specs=pl.BlockSpec((QB, HG * HD), lambda b, h, i: (row(b, h, i), h)),
        out_shape=jax.ShapeDtypeStruct((out_rows, N_HEADS * HD), BF16),
        scratch_shapes=[pltpu.VMEM((HG, QB, 1), F32), pltpu.VMEM((HG, QB, 1), F32),
                        pltpu.VMEM((HG, QB, HD), F32)],
        compiler_params=pltpu.CompilerParams(
            dimension_semantics=("parallel", "parallel", "arbitrary"), vmem_limit_bytes=48 << 20),
        name="nsa_prompt",
    )(uq, glg, perm, cmp_kv, cmp_kv, kvs, kvs, kvw, kvw, expand, ov)


def _nsa_sample_select(q_s, cmp_kv, past_len):
    S = q_s.shape[0]
    HG, HD = Q_PER_KV, HEAD_DIM
    nc = cmp_kv.shape[3]
    n_cmp = nc - 1
    q_pos = past_len
    n_blk = -(-(past_len + 1) // SEL_BLOCK)
    nbp = -(-n_blk // LANES) * LANES
    n_pick = min(N_SEL, n_blk)
    cur = q_pos // SEL_BLOCK
    scale = HD ** -0.5
    ov = _overlap_matrix(n_cmp, n_blk, nc, nbp)

    def body(q_ref, kc_ref, vc_ref, ov_ref, oc_ref, idx_ref):
        q = q_ref[...].astype(BF16)
        s = lax.dot_general(q, kc_ref[...].astype(BF16), _NT, preferred_element_type=F32) * scale
        n_idx = lax.broadcasted_iota(jnp.int32, (HG, nc), 1)
        cvalid = (n_idx * CMP_STRIDE + (CMP_BLOCK - 1) <= q_pos) & (n_idx < n_cmp)
        p_c = _masked_softmax(s, cvalid)
        oc_ref[...] = jnp.dot(p_c.astype(BF16), vc_ref[...].astype(BF16), preferred_element_type=F32)
        imp = jnp.sum(jnp.dot(p_c, ov_ref[...], preferred_element_type=F32,
                              precision=lax.Precision.HIGHEST), axis=0, keepdims=True)
        blk = lax.broadcasted_iota(jnp.int32, (1, nbp), 1)
        forced = (blk == 0) | ((blk <= cur) & (blk > cur - N_LOCAL_SEL))
        score = jnp.where(blk > cur, -BIG, jnp.where(forced, BIG, imp))
        col = jnp.broadcast_to(score, (LANES, nbp)).T[:, 0:1]
        jp = lax.broadcasted_iota(jnp.int32, (nbp, nbp), 0)
        jj = lax.broadcasted_iota(jnp.int32, (nbp, nbp), 1)
        ahead = ((col > score) | ((col == score) & (jp < jj))) & (jp < n_blk)
        rank = jnp.sum(jnp.where(ahead, 1.0, 0.0), axis=0, keepdims=True)
        lane = lax.broadcasted_iota(jnp.int32, (1, LANES), 1)
        out = jnp.zeros((1, LANES), jnp.int32)
        for r in range(n_pick):
            hit = (rank == float(r)) & (blk < n_blk)
            idx_r = jnp.sum(jnp.where(hit, blk, 0), axis=1, keepdims=True)
            out = jnp.where(lane == r, idx_r, out)
        idx_ref[...] = jnp.broadcast_to(out, (SUBLANES_F32, LANES))

    o_c, idx = pl.pallas_call(
        body, grid=(S, N_KV_HEADS),
        in_specs=[pl.BlockSpec((None, None, HG, HD), lambda b, h: (b, h, 0, 0)),
                  pl.BlockSpec((None, None, None, nc, HD), lambda b, h: (0, h, b, 0, 0)),
                  pl.BlockSpec((None, None, None, nc, HD), lambda b, h: (1, h, b, 0, 0)),
                  pl.BlockSpec((nc, nbp), lambda b, h: (0, 0))],
        out_specs=[pl.BlockSpec((None, None, HG, HD), lambda b, h: (b, h, 0, 0)),
                   pl.BlockSpec((None, None, SUBLANES_F32, LANES), lambda b, h: (b, h, 0, 0))],
        out_shape=(jax.ShapeDtypeStruct((S, N_KV_HEADS, HG, HD), F32),
                   jax.ShapeDtypeStruct((S, N_KV_HEADS, SUBLANES_F32, LANES), jnp.int32)),
        compiler_params=pltpu.CompilerParams(dimension_semantics=("parallel", "parallel")),
        name="nsa_sample_select",
    )(q_s, cmp_kv, cmp_kv, ov)
    return o_c, idx[:, :, 0, :n_pick], n_pick


def _nsa_sample_attend(q_s, o_c, gate_s, src, live, cache_sel_v, layer, new_sel, cache_win_v, new_win,
                       n_pick):
    S = q_s.shape[0]
    HG, HD = Q_PER_KV, HEAD_DIM
    wb = cache_win_v.shape[2]
    scale = HD ** -0.5
    nkeys = n_pick * SEL_BLOCK

    def body(src_ref, live_ref, q_ref, oc_ref, g_ref, *rest):
        blk_refs = rest[:n_pick]
        kn_ref, vn_ref, kw_ref, vw_ref, kwn_ref, vwn_ref, o_ref = rest[n_pick:]
        b, h = pl.program_id(0), pl.program_id(1)

        def rows_of(kv):
            per_head = [jnp.concatenate(
                [r[pl.ds(kv * N_KV_HEADS + hh, SEL_BLOCK, stride=KV_ROWS), :] for r in blk_refs], axis=0)
                for hh in range(N_KV_HEADS)]
            out = per_head[0]
            for hh in range(1, N_KV_HEADS):
                out = jnp.where(h == hh, per_head[hh], out)
            return out.astype(BF16)

        base = (b * N_KV_HEADS + h) * n_pick
        q = q_ref[...]
        qb = q.astype(BF16)

        def with_new_key(s, valid, s_new, v_old, v_new):
            sm = jnp.where(valid, s, NEG_INF)
            m = jnp.maximum(jnp.max(sm, axis=-1, keepdims=True), s_new)
            p = jnp.exp(sm - m) * jnp.where(valid, 1.0, 0.0)
            p_new = jnp.exp(s_new - m)
            den = jnp.sum(p, axis=-1, keepdims=True) + p_new
            return (jnp.dot(p.astype(BF16), v_old, preferred_element_type=F32) + p_new * v_new) / den

        keys = rows_of(0)
        vals = rows_of(1)
        s = lax.dot_general(qb, keys, _NT, preferred_element_type=F32) * scale
        lane = lax.broadcasted_iota(jnp.int32, (1, nkeys), 1)
        livef = jnp.zeros((1, nkeys), F32)
        for r in range(n_pick):
            livef = jnp.where((lane >= r * SEL_BLOCK) & (lane < (r + 1) * SEL_BLOCK),
                              live_ref[base + r].astype(F32), livef)
        s_new = jnp.sum(q * kn_ref[...], axis=-1, keepdims=True) * scale
        o_s = with_new_key(s, livef > 0.5, s_new, vals, vn_ref[...])

        sw = lax.dot_general(qb, kw_ref[...].astype(BF16), _NT, preferred_element_type=F32) * scale
        wrow = lax.broadcasted_iota(jnp.int32, (1, wb), 1)
        sw_new = jnp.sum(q * kwn_ref[...], axis=-1, keepdims=True) * scale
        o_w = with_new_key(sw, wrow > wb - WINDOW, sw_new, vw_ref[...].astype(BF16), vwn_ref[...])

        gate = _sigmoid(g_ref[...])
        o_ref[...] = gate[:, 0:1] * oc_ref[...] + gate[:, 1:2] * o_s + gate[:, 2:3] * o_w

    bpp = cache_sel_v.shape[2] // (SEL_BLOCK * KV_ROWS)

    def blk_spec(r):
        def index(b, h, src_ref, live_ref):
            s = src_ref[(b * N_KV_HEADS + h) * n_pick + r]
            return (layer, s // bpp, s % bpp, 0)
        return pl.BlockSpec((None, None, SEL_BLOCK * KV_ROWS, HD), index)

    qspec = pl.BlockSpec((None, None, HG, HD), lambda b, h, *_: (b, h, 0, 0))
    gspec = pl.BlockSpec((None, None, HG, LANES), lambda b, h, *_: (b, h, 0, 0))
    new_spec = lambda kv: pl.BlockSpec((None, None, 1, HD), lambda b, h, *_: (b, kv * N_KV_HEADS + h, 0, 0))
    win_spec = lambda kv: pl.BlockSpec((None, None, wb, HD), lambda b, h, *_: (layer, b, 0, kv * N_KV_HEADS + h))
    in_specs = ([qspec, qspec, gspec] + [blk_spec(r) for r in range(n_pick)]
                + [new_spec(0), new_spec(1), win_spec(0), win_spec(1), new_spec(0), new_spec(1)])
    return pl.pallas_call(
        body,
        grid_spec=pltpu.PrefetchScalarGridSpec(
            num_scalar_prefetch=2, grid=(S, N_KV_HEADS), in_specs=in_specs,
            out_specs=pl.BlockSpec((None, None, HG, HD), lambda b, h, *_: (b, h, 0, 0))),
        out_shape=jax.ShapeDtypeStruct((S, N_KV_HEADS, HG, HD), F32),
        compiler_params=pltpu.CompilerParams(dimension_semantics=("parallel", "parallel")),
        name="nsa_sample_attend",
    )(src, live, q_s, o_c, gate_s, *([cache_sel_v] * n_pick), new_sel, new_sel,
      cache_win_v, cache_win_v, new_win, new_win)


def _nsa_sample(q_s, gate_s, new_sel, new_win, cache_cmp, cache_sel, cache_win, page_table, layer,
                pe, w1, b1, w2):
    S, n_pages = page_table.shape
    depth, n_pool, page = cache_cmp.shape[:3]
    past_len = n_pages * page
    assert page % CMP_STRIDE == 0 and page % SEL_BLOCK == 0
    cpp = page // CMP_STRIDE
    bpp = page // SEL_BLOCK
    pps = _pick_tile(n_pages, 16, 1)
    cache_v = cache_cmp.reshape(depth, n_pool, page * KV_ROWS, HEAD_DIM)
    pt_flat = page_table.reshape(-1)
    srcs = [(cache_v, (None, None, page * KV_ROWS, HEAD_DIM),
             (lambda b, s, pt, p=p: (layer, pt[b * n_pages + s * pps + p], 0, 0))) for p in range(pps)]
    nc = n_pages * cpp
    lt = _compress_lead_trail(srcs, (S, n_pages // pps), pe, w1, cpp, S * nc, prefetch=(pt_flat,),
                              token_major=True, name="cmp_lt_sample")
    cmp_kv = _compress_finish(lt, S, nc, b1, w2, "cmp_finish_sample")
    o_c, idx, n_pick = _nsa_sample_select(q_s, cmp_kv, past_len)

    n_past_blk = past_len // SEL_BLOCK
    idc = jnp.minimum(idx, n_past_blk - 1)
    pages = jnp.take_along_axis(page_table[:, None, :], idc // bpp, axis=2)
    src = (pages * bpp + idc % bpp).astype(jnp.int32).reshape(-1)
    live = (idx < n_past_blk).astype(jnp.int32).reshape(-1)
    wb = cache_win.shape[2]
    out = _nsa_sample_attend(
        q_s, o_c, gate_s, src, live, cache_sel.reshape(depth, n_pool, page * KV_ROWS, HEAD_DIM), layer,
        new_sel, cache_win.reshape(depth, S, wb, KV_COLS), new_win, n_pick)
    return out.reshape(S, N_HEADS * HEAD_DIM)


MOE_ROW_TILE = 512


def _moe_route(x, g, w_router):
    M, D = x.shape
    E = w_router.shape[1]
    tr = _pick_tile(M, 256, SUBLANES_BF16)
    wr = jnp.pad(w_router, ((0, 0), (0, LANES - E)))

    def body(x_ref, g_ref, w_ref, h_ref, r_ref):
        xf = x_ref[...]
        ms = jnp.mean(xf * xf, axis=-1, keepdims=True)
        hf = xf * lax.rsqrt(ms + RMS_EPS) * g_ref[...]
        bits = lax.bitcast_convert_type(hf.astype(BF16).astype(F32), jnp.uint32)
        h_ref[...] = (bits[:, :D // 2] >> 16) | (bits[:, D // 2:] & jnp.uint32(0xFFFF0000))
        logits = jnp.dot(hf, w_ref[...], preferred_element_type=F32, precision=lax.Precision.HIGHEST)
        lane = lax.broadcasted_iota(jnp.int32, (tr, LANES), 1)
        logits = jnp.where(lane < E, logits, -jnp.inf)
        v1 = jnp.max(logits, axis=-1, keepdims=True)
        i1 = jnp.min(jnp.where(logits == v1, lane, LANES), axis=-1, keepdims=True)
        rest = jnp.where(lane == i1, -jnp.inf, logits)
        v2 = jnp.max(rest, axis=-1, keepdims=True)
        i2 = jnp.min(jnp.where(rest == v2, lane, LANES), axis=-1, keepdims=True)
        e2 = jnp.exp(v2 - v1)
        g1 = 1.0 / (1.0 + e2)
        out = jnp.where(lane == 0, i1.astype(F32), jnp.where(lane == 1, i2.astype(F32),
                        jnp.where(lane == 2, g1, jnp.where(lane == 3, e2 * g1, 0.0))))
        r_ref[...] = out

    return pl.pallas_call(
        body, grid=(M // tr,),
        in_specs=[pl.BlockSpec((tr, D), lambda i: (i, 0)), pl.BlockSpec((1, D), lambda i: (0, 0)),
                  pl.BlockSpec((D, LANES), lambda i: (0, 0))],
        out_specs=[pl.BlockSpec((tr, D // 2), lambda i: (i, 0)), pl.BlockSpec((tr, LANES), lambda i: (i, 0))],
        out_shape=(jax.ShapeDtypeStruct((M, D // 2), jnp.uint32), jax.ShapeDtypeStruct((M, LANES), F32)),
        compiler_params=pltpu.CompilerParams(dimension_semantics=("parallel",)),
        name="moe_route",
    )(x, g.reshape(1, D), wr)


MOE_GROUP_TILES = 5
MOE_DISPATCH_PARTS = 4


def _moe_down(hmid, wd, li, gate_b, tile_expert, n_used, tm):
    rows, F = hmid.shape
    D = wd.shape[-1]
    mt = MOE_GROUP_TILES
    n_tiles = rows // tm
    tk = _pick_tile(F, 2048, LANES)
    tn = _pick_tile(D, 1024, LANES)
    nk = F // tk
    n_slots = n_tiles * nk

    t = jnp.arange(n_tiles, dtype=jnp.int32)
    used = t < n_used
    same = (tile_expert[:, None] == tile_expert[None, :]) & used[None, :]
    first = jnp.min(jnp.where(same, t[None, :], n_tiles), axis=1)
    cnt = jnp.sum(same.astype(jnp.int32), axis=1)
    j = t - first
    m_local = jnp.where(used, j % mt, 0)
    g_first = t - m_local
    g_cnt = jnp.where(used, jnp.minimum(mt, cnt - (j - m_local)), 1)
    kk = jnp.arange(nk, dtype=jnp.int32)
    pos = (nk * g_first[:, None] + kk[None, :] * g_cnt[:, None] + m_local[:, None]).reshape(-1)
    n_items = (nk * n_used).astype(jnp.int32)
    order = jnp.minimum(jnp.arange(n_slots, dtype=jnp.int32), n_items - 1)
    table = lambda v: jnp.zeros((n_slots,), jnp.int32).at[pos].set(v.reshape(-1))[order]
    it_tile = table(jnp.broadcast_to(t[:, None], (n_tiles, nk)))
    it_k = table(jnp.broadcast_to(kk[None, :], (n_tiles, nk)))
    it_slot = table(jnp.broadcast_to(m_local[:, None], (n_tiles, nk)))
    it_exp = tile_expert[it_tile]
    it_out = jnp.where(it_k == nk - 1, it_tile, it_tile - it_slot)
    run_start = (jnp.arange(n_slots, dtype=jnp.int32) < n_items) & (it_slot == 0)
    it_run = (jnp.cumsum(run_start.astype(jnp.int32)) - 1).astype(jnp.int32)
    n_runs = jnp.sum(run_start.astype(jnp.int32)).reshape(1)
    it_nxt = _next_run_start(run_start)
    nn = D // tn

    def body(tile_ref, k_ref, slot_ref, exp_ref, out_ref, n_ref, run_ref, nxt_ref, nruns_ref,
             h_ref, w_hbm, g_ref, o_ref, acc_ref, wbuf, sem):
        n, i = pl.program_id(0), pl.program_id(1)

        def copy(item, col_tile, wslot):
            src = w_hbm.at[li, exp_ref[item], pl.ds(pl.multiple_of(k_ref[item] * tk, tk), tk),
                           pl.ds(pl.multiple_of(col_tile * tn, tn), tn)]
            return pltpu.make_async_copy(src, wbuf.at[wslot], sem.at[wslot])

        @pl.when(i < n_ref[0])
        def _():
            k = k_ref[i]
            slot = slot_ref[i]
            wslot = (n * nruns_ref[0] + run_ref[i]) % 2

            @pl.when(slot == 0)
            def _():
                @pl.when((n == 0) & (i == 0))
                def _():
                    copy(0, 0, 0).start()

                copy(i, n, wslot).wait()
                nx = nxt_ref[i]

                @pl.when(nx >= 0)
                def _():
                    copy(jnp.maximum(nx, 0), n, 1 - wslot).start()

                @pl.when((nx < 0) & (n + 1 < nn))
                def _():
                    copy(0, n + 1, 1 - wslot).start()

            p = jnp.dot(h_ref[...], wbuf[wslot].astype(BF16), preferred_element_type=F32)

            @pl.when(k == 0)
            def _():
                acc_ref[slot] = p

            @pl.when(k > 0)
            def _():
                acc_ref[slot] += p

            @pl.when(k == nk - 1)
            def _():
                o_ref[...] = acc_ref[slot] * g_ref[...][:, 0:1]

    est = (2 * _nbytes((tm, tk), BF16) + 2 * _nbytes((tk, tn), F32) + _nbytes((tk, tn), BF16)
           + (mt + 3) * _nbytes((tm, tn), F32))
    return pl.pallas_call(
        body,
        grid_spec=pltpu.PrefetchScalarGridSpec(
            num_scalar_prefetch=9, grid=(nn, n_slots),
            in_specs=[pl.BlockSpec((tm, tk), lambda n, i, tr, kr, *_: (tr[i], kr[i])),
                      pl.BlockSpec(memory_space=pl.ANY),
                      pl.BlockSpec((tm, LANES), lambda n, i, tr, *_: (tr[i], 0))],
            out_specs=pl.BlockSpec((tm, tn), lambda n, i, tr, kr, sr, er, orr, *_: (orr[i], n)),
            scratch_shapes=[pltpu.VMEM((mt, tm, tn), F32), pltpu.VMEM((2, tk, tn), F32),
                            pltpu.SemaphoreType.DMA((2,))]),
        out_shape=jax.ShapeDtypeStruct((rows, D), F32),
        compiler_params=pltpu.CompilerParams(
            dimension_semantics=("arbitrary", "arbitrary"), vmem_limit_bytes=_vmem_limit(est)),
        name="moe_down",
    )(it_tile, it_k, it_slot, it_exp, it_out, n_items.reshape(1), it_run, it_nxt, n_runs,
      hmid, wd, gate_b)


def _next_run_start(first):
    n = first.shape[0]
    idx = jnp.where(first, jnp.arange(n, dtype=jnp.int32), n)
    at_or_after = lax.cummin(idx[::-1])[::-1]
    nxt = jnp.concatenate([at_or_after[1:], jnp.full((1,), n, jnp.int32)])
    return jnp.where(nxt >= n, -1, nxt).astype(jnp.int32)


def _moe_up(xs, wg, wu, li, tile_expert, n_used, tm, tile0, rows_total, prev):
    rows = xs.shape[0]
    D, F = wg.shape[2], wg.shape[3]
    n_tiles = rows // tm
    tn = _pick_tile(F, 512, LANES)
    nn = F // tn

    t = jnp.arange(n_tiles, dtype=jnp.int32)
    prev_e = jnp.concatenate([jnp.full((1,), -1, jnp.int32), tile_expert[:-1]])
    first = (t < n_used) & (tile_expert != prev_e)
    rank = (jnp.cumsum(first.astype(jnp.int32)) - 1).astype(jnp.int32)
    n_runs = jnp.sum(first.astype(jnp.int32)).reshape(1)
    nxt = _next_run_start(first)

    def body(te, nu, first_ref, rank_ref, nxt_ref, nruns_ref, x_ref, wg_hbm, wu_hbm, *rest):
        o_ref, wbuf, sem = rest[-3:]
        n, m = pl.program_id(0), pl.program_id(1)

        def copies(e, col_tile, slot):
            cols = pl.ds(pl.multiple_of(col_tile * tn, tn), tn)
            return [pltpu.make_async_copy(w.at[li, e, :, cols], wbuf.at[slot, i], sem.at[slot, i])
                    for i, w in enumerate((wg_hbm, wu_hbm))]

        @pl.when(m < nu[0])
        def _():
            slot = (n * nruns_ref[0] + rank_ref[m]) % 2

            @pl.when(first_ref[m] == 1)
            def _():
                @pl.when((n == 0) & (m == 0))
                def _():
                    for c in copies(te[0], 0, 0):
                        c.start()

                for c in copies(te[m], n, slot):
                    c.wait()
                nx = nxt_ref[m]

                @pl.when(nx >= 0)
                def _():
                    for c in copies(te[jnp.maximum(nx, 0)], n, 1 - slot):
                        c.start()

                @pl.when((nx < 0) & (n + 1 < nn))
                def _():
                    for c in copies(te[0], n + 1, 1 - slot):
                        c.start()

            v = x_ref[...]
            lo = lax.bitcast_convert_type(v << 16, F32)
            hi = lax.bitcast_convert_type(v & jnp.uint32(0xFFFF0000), F32)
            xb = jnp.concatenate([lo, hi], axis=1).astype(BF16)
            gate = jnp.dot(xb, wbuf[slot, 0].astype(BF16), preferred_element_type=F32)
            up = jnp.dot(xb, wbuf[slot, 1].astype(BF16), preferred_element_type=F32)
            o_ref[...] = (gate * _sigmoid(gate) * up).astype(BF16)

    est = (2 * _nbytes((tm, D // 2), jnp.uint32) + 3 * _nbytes((tm, D), BF16) + 4 * _nbytes((D, tn), F32)
           + 2 * _nbytes((D, tn), BF16) + 2 * _nbytes((tm, tn), BF16) + 3 * _nbytes((tm, tn), F32))
    in_specs = [pl.BlockSpec((tm, D // 2), lambda n, m, *_: (m, 0)),
                pl.BlockSpec(memory_space=pl.ANY), pl.BlockSpec(memory_space=pl.ANY)]
    args = [xs, wg, wu]
    aliases = {}
    if prev is not None:
        in_specs.append(pl.BlockSpec(memory_space=pl.ANY))
        args.append(prev)
        aliases = {6 + 3: 0}
    return pl.pallas_call(
        body,
        grid_spec=pltpu.PrefetchScalarGridSpec(
            num_scalar_prefetch=6, grid=(nn, n_tiles), in_specs=in_specs,
            out_specs=pl.BlockSpec((tm, tn), lambda n, m, *_: (tile0 + m, n)),
            scratch_shapes=[pltpu.VMEM((2, 2, D, tn), F32), pltpu.SemaphoreType.DMA((2, 2))]),
        out_shape=jax.ShapeDtypeStruct((rows_total, F), BF16),
        input_output_aliases=aliases,
        compiler_params=pltpu.CompilerParams(
            dimension_semantics=("arbitrary", "arbitrary"), vmem_limit_bytes=_vmem_limit(est)),
        name="moe_up",
    )(tile_expert, n_used.reshape(1), first.astype(jnp.int32), rank, nxt, n_runs, *args)


def _moe(x_all, n_real, g, w_router, wg, wu, wd, li):
    Mp, D = x_all.shape
    E, _, F = wg.shape[1:]
    tm = MOE_ROW_TILE
    h, route = _moe_route(x_all, g, w_router[li])

    r = route[:n_real]
    e_flat = r[:, :TOP_K].astype(jnp.int32).reshape(-1)
    gates = r[:, TOP_K:2 * TOP_K].reshape(-1)
    n_pairs = n_real * TOP_K
    onehot = (e_flat[:, None] == jnp.arange(E)[None, :]).astype(jnp.int32)
    counts = jnp.sum(onehot, axis=0)
    within = jnp.take_along_axis(jnp.cumsum(onehot, axis=0) - onehot, e_flat[:, None], axis=1)[:, 0]
    padded = (counts + tm - 1) // tm * tm
    ends = jnp.cumsum(padded)
    dest = (ends - padded)[e_flat] + within
    parts = MOE_DISPATCH_PARTS
    n_tiles = -(-(-(-n_pairs // tm) + E) // parts) * parts
    rows = n_tiles * tm
    src_tok = jnp.zeros((rows,), jnp.int32).at[dest].set(jnp.arange(n_pairs, dtype=jnp.int32) // TOP_K)
    row_gate = jnp.zeros((rows,), F32).at[dest].set(gates)
    n_used = (ends[-1] // tm).astype(jnp.int32)
    tile_ids = jnp.minimum(jnp.arange(n_tiles, dtype=jnp.int32), n_used - 1)
    tile_expert = jnp.minimum(jnp.searchsorted(ends, tile_ids * tm, side="right"), E - 1).astype(jnp.int32)
    tpp = n_tiles // parts
    hmid = None
    for p in range(parts):
        xs = jnp.take(h, src_tok[p * tpp * tm:(p + 1) * tpp * tm], axis=0, mode="clip")
        hmid = _moe_up(xs, wg, wu, li, tile_expert[p * tpp:(p + 1) * tpp],
                       jnp.clip(n_used - p * tpp, 0, tpp), tm, p * tpp, rows, hmid)

    gate_b = jnp.broadcast_to(row_gate[:, None], (rows, LANES))
    y = _moe_down(hmid, wd, li, gate_b, tile_expert, n_used, tm)

    d2 = dest.reshape(n_real, TOP_K)
    y_tok = jnp.take(y, d2[:, 0], axis=0, mode="clip")
    for k in range(1, TOP_K):
        y_tok = y_tok + jnp.take(y, d2[:, k], axis=0, mode="clip")
    return x_all.at[:n_real].add(y_tok)


def kernel(x_prompt, x_sample, cache_cmp, cache_sel, cache_win, state_ssm_re, state_ssm_im, page_table,
           norm_mix, w_in, ssm_a_re, ssm_a_im, ssm_b_re, ssm_b_im, ssm_c_re, ssm_c_im, ssm_d, ssm_log_dt,
           ssm_w_glu, ssm_b_glu, cmp_pe, cmp_w1, cmp_b1, cmp_w2, w_br_ssm, w_br_nsa, w_out,
           norm_ffn, ffn_w_gate, ffn_w_up, ffn_w_down, moe_router, moe_w_gate, moe_w_up, moe_w_down,
           norm_final):
    B, T, D = x_prompt.shape
    S = x_sample.shape[0]
    assert x_sample.shape[1] == 1
    depth = w_in.shape[0]
    n_prompt = B * T
    n_real = n_prompt + S
    Mp = -(-n_real // ROW_PAD) * ROW_PAD
    n_tail = Mp - n_prompt
    d_ssm = ssm_d.shape[1]
    d_nsa = N_HEADS * HEAD_DIM
    G, P = ssm_a_re.shape[1], ssm_a_re.shape[2]
    o_kv = d_ssm + d_nsa
    o_gl = o_kv + 3 * KV_COLS
    o_g = o_gl + 3 * N_HEADS
    assert d_ssm % (Q_PER_KV * HEAD_DIM) == 0 and w_in.shape[2] == o_g + 2 * D

    tm_big = _pick_tile(Mp, 1376, SUBLANES_BF16)
    tm_mid = _pick_tile(Mp, 688, SUBLANES_BF16)
    tn = 512

    x_all = jnp.concatenate([x_prompt.reshape(n_prompt, D), x_sample.reshape(S, D),
                             jnp.zeros((Mp - n_real, D), F32)], axis=0)
    w_in_t = jnp.swapaxes(w_in, 1, 2)
    pad_tail = lambda a: jnp.pad(a.reshape(S, -1), ((0, n_tail - S), (0, 0)))

    outs = {k: [] for k in ("cmp_p", "sel_p", "win_p", "sre_p", "sim_p",
                            "cmp_s", "sel_s", "win_s", "sre_s", "sim_s")}
    for l in range(depth):
        h = _rmsnorm(x_all, norm_mix[l], BF16, "norm_mix")
        uq = _dense(h, w_in_t, (l,), 0, o_kv, tm=tm_big, tn=tn, w_t=True, name="in_uq")
        kvc, kvs, kvw = [_dense(h, w_in_t, (l,), o_kv + i * KV_COLS, KV_COLS, tm=tm_big, tn=KV_COLS,
                                w_t=True, name=f"in_kv{i}") for i in range(3)]
        glg = _dense(h, w_in_t, (l,), o_gl, LANES, tm=tm_big, tn=LANES, w_t=True, name="in_gate_logits")
        gates = _dense_t_unaligned(h, w_in_t, (l,), o_g, 2 * D, tm=tm_big, tn=256, epilogue=_sigmoid,
                                   out_dtype=F32, name="in_merge_gates")

        s0 = [pad_tail(s[l]) for s in (state_ssm_re, state_ssm_im)]
        ssm_params = (ssm_a_re, ssm_a_im, ssm_b_re, ssm_b_im, ssm_c_re, ssm_c_im, ssm_d, ssm_log_dt,
                      ssm_w_glu, ssm_b_glu)
        ssm_out, fin_p, fin_s = _s5_branch(uq, d_ssm, s0[0], s0[1], n_prompt, B, T, ssm_params, l, tm_mid)

        kd = CMP_STRIDE * HEAD_DIM
        pe = cmp_pe[l].reshape(2, 2, 1, kd)
        w1 = cmp_w1[l].reshape(2, 2, kd, HEAD_DIM)
        n_chunk = T // CMP_STRIDE
        rb = _pick_tile(n_prompt // CMP_STRIDE, 128, SUBLANES_F32)
        lt = _compress_lead_trail(
            [(kvc.reshape(Mp // CMP_STRIDE, CHUNK_COLS), (rb, CHUNK_COLS), lambda a, s: (s, 0))],
            (1, n_prompt // CMP_STRIDE // rb), pe, w1, rb, n_prompt // CMP_STRIDE, name="cmp_lt_prompt")
        cmp_kv = _compress_finish(lt, B, n_chunk, cmp_b1[l], cmp_w2[l], "cmp_finish_prompt")
        if n_chunk < LANES:
            cmp_kv = jnp.pad(cmp_kv, ((0, 0), (0, 0), (0, 0), (0, LANES - n_chunk), (0, 0)))
        q_blk0 = d_ssm // (Q_PER_KV * HEAD_DIM)
        nsa_out = _nsa_prompt(uq, q_blk0, glg, cmp_kv, kvs, kvw, n_seq=B, seq_len=T, out_rows=Mp)

        tail = lambda a, c0, c1: a[n_prompt:n_real, c0:c1]
        q_s = tail(uq, d_ssm, o_kv).reshape(S, N_KV_HEADS, Q_PER_KV, HEAD_DIM)
        gl_s = tail(glg, 0, 3 * N_HEADS).reshape(S, N_KV_HEADS, Q_PER_KV, 3)
        gate_s = jnp.pad(gl_s, ((0, 0), (0, 0), (0, 0), (0, LANES - 3)))
        new_sel = tail(kvs, 0, KV_COLS).reshape(S, 2 * N_KV_HEADS, 1, HEAD_DIM)
        new_win = tail(kvw, 0, KV_COLS).reshape(S, 2 * N_KV_HEADS, 1, HEAD_DIM)
        nsa_s = _nsa_sample(q_s, gate_s, new_sel, new_win, cache_cmp, cache_sel, cache_win, page_table, l,
                            pe, w1, cmp_b1[l], cmp_w2[l])
        nsa_out = lax.dynamic_update_slice(nsa_out, pad_tail(nsa_s).astype(BF16), (n_prompt, 0))

        ng = D // tn
        y = _fused_matmul(
            xs=[(ssm_out, lambda m, n, k, *pf: (m, 0)), (nsa_out, lambda m, n, k, *pf: (m, 0))],
            dots=[(0, w_br_ssm, (None, d_ssm, tn), lambda m, n, k, *pf: (l, 0, n)),
                  (1, w_br_nsa, (None, d_nsa, tn), lambda m, n, k, *pf: (l, 0, n))],
            extras=[(gates, (tm_mid, tn), lambda m, n, k, *pf: (m, n)),
                    (gates, (tm_mid, tn), lambda m, n, k, *pf: (m, ng + n))],
            epilogue=lambda p, e: e[0] * p[0] + e[1] * p[1],
            M=Mp, N=D, K=d_ssm, tm=tm_mid, tn=tn, tk=d_ssm, out_dtype=BF16, name="merge")
        x_all = _dense(y, w_out, (l,), 0, D, tm=tm_big, tn=tn,
                       extras=[(x_all, (tm_big, tn), lambda m, n, k, *pf: (m, n))],
                       epilogue=lambda p, e: e[0] + p[0], name="out_proj")

        kv5 = lambda a, rows, lead: a[rows].reshape(lead + (2, N_KV_HEADS, HEAD_DIM))
        outs["cmp_p"].append(kv5(kvc, slice(0, n_prompt), (B, T)))
        outs["sel_p"].append(kv5(kvs, slice(0, n_prompt), (B, T)))
        wp = min(WINDOW, T)
        outs["win_p"].append(kv5(kvw, slice(0, n_prompt), (B, T))[:, T - wp:])
        outs["sre_p"].append(fin_p[0].reshape(B, G, P))
        outs["sim_p"].append(fin_p[1].reshape(B, G, P))
        outs["cmp_s"].append(kv5(kvc, slice(n_prompt, n_real), (S, 1)))
        outs["sel_s"].append(kv5(kvs, slice(n_prompt, n_real), (S, 1)))
        win = jnp.concatenate([cache_win[l], kv5(kvw, slice(n_prompt, n_real), (S, 1))], axis=1)
        outs["win_s"].append(win[:, 1:])
        outs["sre_s"].append(fin_s[0][:S].reshape(S, G, P))
        outs["sim_s"].append(fin_s[1][:S].reshape(S, G, P))

        if l % 2 == 0:
            i = l // 2
            h2 = _rmsnorm(x_all, norm_ffn[l], BF16, "norm_ffn")
            f = ffn_w_gate.shape[2]
            tnf = _pick_tile(f, 256, LANES)
            wmap = lambda m, n, k, *pf: (i, 0, n)
            hmid = _fused_matmul(
                xs=[(h2, lambda m, n, k, *pf: (m, 0))],
                dots=[(0, ffn_w_gate, (None, D, tnf), wmap), (0, ffn_w_up, (None, D, tnf), wmap)],
                epilogue=lambda p, e: p[0] * _sigmoid(p[0]) * p[1],
                M=Mp, N=f, K=D, tm=tm_big, tn=tnf, tk=D, out_dtype=BF16, name="ffn_up")
            tk = _pick_tile(f, 1024, LANES)
            tnd = _pick_tile(D, 1024, LANES)
            x_all = _fused_matmul(
                xs=[(hmid, lambda m, n, k, *pf: (m, k))],
                dots=[(0, ffn_w_down, (None, tk, tnd), lambda m, n, k, *pf: (i, k, n))],
                extras=[(x_all, (tm_big, tnd), lambda m, n, k, *pf: (m, n))],
                epilogue=lambda p, e: e[0] + p[0],
                M=Mp, N=D, K=f, tm=tm_big, tn=tnd, tk=tk, out_dtype=F32, name="ffn_down")
        else:
            x_all = _moe(x_all, n_real, norm_ffn[l], moe_router, moe_w_gate, moe_w_up, moe_w_down, l // 2)

    y_p = _rmsnorm(x_all, norm_final, F32, "norm_final", rows=n_prompt)
    y_s = _rmsnorm(x_all[n_prompt:], norm_final, F32, "norm_final_tail")
    st = lambda k: jnp.stack(outs[k])
    return (y_p.reshape(B, T, D), y_s[:S].reshape(S, 1, D),
            st("cmp_p"), st("sel_p"), st("win_p"), st("sre_p"), st("sim_p"),
            st("cmp_s"), st("sel_s"), st("win_s"), st("sre_s"), st("sim_s"))
```

```python
import functools
import math

import numpy as np
import jax
import jax.numpy as jnp
from jax import lax
from jax.experimental import pallas as pl
from jax.experimental.pallas import tpu as pltpu

F32 = jnp.float32
BF16 = jnp.bfloat16

SSM_GROUP = 16
N_HEADS = 16
HEAD_DIM = 128
N_KV_HEADS = 2
Q_PER_KV = N_HEADS // N_KV_HEADS
CMP_STRIDE = 16
CMP_BLOCK = 2 * CMP_STRIDE
SEL_BLOCK = 64
N_SEL = 16
N_LOCAL_SEL = 2
WINDOW = 512
Q_BLOCK = 128
TOP_K = 2
RMS_EPS = 1e-6
NEG_INF = -1e30
BIG = 1e30

LANES = 128
SUBLANES_F32 = 8
SUBLANES_BF16 = 16
VMEM_BYTES_V7X = 64 * 1024 * 1024
VMEM_LIMIT_CAP = VMEM_BYTES_V7X - 6 * 1024 * 1024
ROW_PAD = 64


def _vmem_limit(estimate_bytes):
    return int(min(max(estimate_bytes * 5 // 4 + (4 << 20), 16 << 20), VMEM_LIMIT_CAP))


def _pick_tile(total, target, mult):
    best = None
    for t in range(mult, min(total, target) + 1, mult):
        if total % t == 0:
            best = t
    assert best is not None, (total, target, mult)
    return best


def _nbytes(shape, dtype):
    n = 1
    for s in shape:
        if s is not None:
            n *= s
    return n * jnp.dtype(dtype).itemsize


def _gelu_tanh(x):
    return 0.5 * x * (1.0 + jnp.tanh(math.sqrt(2.0 / math.pi) * (x + 0.044715 * (x * x * x))))


def _sigmoid(x):
    return 1.0 / (1.0 + jnp.exp(-x))


def _fused_matmul(*, xs, dots, extras=(), epilogue, M, N, K, tm, tn, tk, out_dtype, name):
    assert M % tm == 0 and N % tn == 0 and K % tk == 0, (M, N, K, tm, tn, tk)
    nm, nn, nk = M // tm, N // tn, K // tk
    nd, nx, ne = len(dots), len(xs), len(extras)
    use_acc = nk > 1

    def body(*refs):
        x_refs = refs[:nx]
        w_refs = refs[nx:nx + nd]
        e_refs = refs[nx + nd:nx + nd + ne]
        o_ref = refs[nx + nd + ne]
        acc_refs = refs[nx + nd + ne + 1:]
        xv = [xr[...].astype(BF16) for xr in x_refs]
        parts = []
        for i, d in enumerate(dots):
            wv = w_refs[i][...].astype(BF16)
            if len(d) > 4 and d[4]:
                parts.append(lax.dot_general(xv[d[0]], wv, _NT, preferred_element_type=F32))
            else:
                parts.append(jnp.dot(xv[d[0]], wv, preferred_element_type=F32))
        if not use_acc:
            o_ref[...] = epilogue(parts, [e[...] for e in e_refs]).astype(out_dtype)
            return
        k = pl.program_id(2)

        @pl.when(k == 0)
        def _():
            for i in range(nd):
                acc_refs[i][...] = parts[i]

        @pl.when(k > 0)
        def _():
            for i in range(nd):
                acc_refs[i][...] += parts[i]

        @pl.when(k == nk - 1)
        def _():
            o_ref[...] = epilogue([a[...] for a in acc_refs], [e[...] for e in e_refs]).astype(out_dtype)

    in_specs = [pl.BlockSpec((tm, tk), f) for _, f in xs]
    args = [a for a, _ in xs]
    est = sum(2 * _nbytes((tm, tk), a.dtype) for a, _ in xs)
    for d in dots:
        w, bs, f = d[1:4]
        in_specs.append(pl.BlockSpec(bs, f))
        args.append(w)
        est += 2 * _nbytes(bs, w.dtype) + _nbytes(bs, BF16)
    for a, bs, f in extras:
        in_specs.append(pl.BlockSpec(bs, f))
        args.append(a)
        est += 2 * _nbytes(bs, a.dtype)
    est += 2 * _nbytes((tm, tn), out_dtype) + (2 + nd) * _nbytes((tm, tn), F32)
    scratch = [pltpu.VMEM((tm, tn), F32) for _ in range(nd)] if use_acc else []
    return pl.pallas_call(
        body, grid=(nm, nn, nk), in_specs=in_specs,
        out_specs=pl.BlockSpec((tm, tn), lambda m, n, k: (m, n)), scratch_shapes=scratch,
        out_shape=jax.ShapeDtypeStruct((M, N), out_dtype),
        compiler_params=pltpu.CompilerParams(
            dimension_semantics=("parallel", "parallel", "arbitrary"),
            vmem_limit_bytes=_vmem_limit(est)),
        name=name,
    )(*args)


def _first(parts, extras):
    return parts[0]


def _dense(x, w, w_lead, col0, N, *, tm, tn, epilogue=_first, extras=(), out_dtype=F32, w_t=False, name):
    M, K = x.shape
    assert col0 % tn == 0
    c0 = col0 // tn
    nl = len(w_lead)
    if w_t:
        dot = (0, w, (None,) * nl + (tn, K), lambda m, n, k, *pf: tuple(w_lead) + (c0 + n, 0), True)
    else:
        dot = (0, w, (None,) * nl + (K, tn), lambda m, n, k, *pf: tuple(w_lead) + (0, c0 + n))
    return _fused_matmul(
        xs=[(x, lambda m, n, k, *pf: (m, 0))],
        dots=[dot],
        extras=extras, epilogue=epilogue, M=M, N=N, K=K, tm=tm, tn=tn, tk=K,
        out_dtype=out_dtype, name=name)


def _dense_t_unaligned(x, w, w_lead, row0, N, *, tm, tn, epilogue, out_dtype, name):
    M, K = x.shape
    sh = row0 % tn
    assert sh % SUBLANES_F32 == 0 and N % tn == 0 and M % tm == 0
    b0 = row0 // tn
    nl = len(w_lead)

    def body(x_ref, wa_ref, wb_ref, o_ref):
        wv = jnp.concatenate([wa_ref[sh:, :], wb_ref[:sh, :]], axis=0).astype(BF16)
        acc = lax.dot_general(x_ref[...], wv, _NT, preferred_element_type=F32)
        o_ref[...] = epilogue(acc).astype(out_dtype)

    wspec = lambda off: pl.BlockSpec((None,) * nl + (tn, K), lambda m, n: tuple(w_lead) + (b0 + off + n, 0))
    est = (2 * _nbytes((tm, K), x.dtype) + 4 * _nbytes((tn, K), F32) + 2 * _nbytes((tn, K), BF16)
           + 4 * _nbytes((tm, tn), F32))
    return pl.pallas_call(
        body, grid=(M // tm, N // tn),
        in_specs=[pl.BlockSpec((tm, K), lambda m, n: (m, 0)), wspec(0), wspec(1)],
        out_specs=pl.BlockSpec((tm, tn), lambda m, n: (m, n)),
        out_shape=jax.ShapeDtypeStruct((M, N), out_dtype),
        compiler_params=pltpu.CompilerParams(
            dimension_semantics=("parallel", "parallel"), vmem_limit_bytes=_vmem_limit(est)),
        name=name,
    )(x, w, w)


def _rmsnorm(x, g, out_dtype, name, rows=None):
    M, D = x.shape
    M = M if rows is None else rows
    tr = _pick_tile(M, 256, SUBLANES_BF16)

    def body(x_ref, g_ref, o_ref):
        xf = x_ref[...]
        ms = jnp.mean(xf * xf, axis=-1, keepdims=True)
        o_ref[...] = (xf * lax.rsqrt(ms + RMS_EPS) * g_ref[...]).astype(out_dtype)

    return pl.pallas_call(
        body, grid=(M // tr,),
        in_specs=[pl.BlockSpec((tr, D), lambda i: (i, 0)), pl.BlockSpec((1, D), lambda i: (0, 0))],
        out_specs=pl.BlockSpec((tr, D), lambda i: (i, 0)),
        out_shape=jax.ShapeDtypeStruct((M, D), out_dtype),
        compiler_params=pltpu.CompilerParams(dimension_semantics=("parallel",)),
        name=name,
    )(x, g.reshape(1, D))


def _s5_discretize(a_re, a_im, log_dt_b, b_re_t, b_im_t):
    C, GP = b_re_t.shape

    def body(ar_ref, ai_ref, ld_ref, br_ref, bi_ref, lr_ref, li_ref, obr_ref, obi_ref):
        ar, ai = ar_ref[...], ai_ref[...]
        dt = jnp.exp(ld_ref[...])
        mag = jnp.exp(ar * dt)
        ang = ai * dt
        lr = mag * jnp.cos(ang)
        li = mag * jnp.sin(ang)
        pr, pi = lr, li
        for k in range(SUBLANES_F32):
            lr_ref[k:k + 1, :] = pr
            li_ref[k:k + 1, :] = pi
            pr, pi = pr * lr - pi * li, pr * li + pi * lr
        x, y = lr - 1.0, li
        den = ar * ar + ai * ai
        cr = (x * ar + y * ai) / den
        ci = (y * ar - x * ai) / den
        br, bi = br_ref[...], bi_ref[...]
        obr_ref[...] = cr * br - ci * bi
        obi_ref[...] = cr * bi + ci * br

    vec = jax.ShapeDtypeStruct((SUBLANES_F32, GP), F32)
    mat = jax.ShapeDtypeStruct((C, GP), F32)
    return pl.pallas_call(body, out_shape=(vec, vec, mat, mat), name="s5_discretize")(
        a_re, a_im, log_dt_b, b_re_t, b_im_t)


GROUP_TILE = 16
S5_IN = GROUP_TILE * SSM_GROUP


def _s5_prompt(uq, w_b, w_c, pw_re, pw_im, d2, *, n_seq, seq_len, out_rows):
    ngt, kx, ns = w_b.shape[1:]
    sub = SUBLANES_F32
    tc = _pick_tile(seq_len, 256, sub)
    nch = seq_len // tc
    ntile = tc // sub

    def body(u_ref, wb_ref, wc_ref, pr_ref, pi_ref, d_ref, z_ref, fr_ref, fi_ref, st, sr_ref, si_ref):
        c = pl.program_id(2)

        @pl.when(c == 0)
        def _():
            st[...] = jnp.zeros(st.shape, F32)

        u = u_ref[...]
        ub = u.astype(BF16)
        xr = jnp.dot(ub, wb_ref[0], preferred_element_type=F32).reshape(ntile, sub, ns)
        xi = jnp.dot(ub, wb_ref[1], preferred_element_type=F32).reshape(ntile, sub, ns)
        pr, pi = pr_ref[...], pi_ref[...]
        row = lax.broadcasted_iota(jnp.int32, (sub, ns), 0)
        d = 1
        while d < sub:
            mr = jnp.where(row >= d, pr[d - 1:d, :], 0.0)
            mi = jnp.where(row >= d, pi[d - 1:d, :], 0.0)
            rr = pltpu.roll(xr, d, axis=1)
            ri = pltpu.roll(xi, d, axis=1)
            xr, xi = xr + mr * rr - mi * ri, xi + mr * ri + mi * rr
            d *= 2
        sr_ref[...] = xr
        si_ref[...] = xi

        def tile(j, carry):
            cr, ci = carry
            nr = sr_ref[j] + pr * cr - pi * ci
            ni = si_ref[j] + pr * ci + pi * cr
            sr_ref[j] = nr
            si_ref[j] = ni
            return nr[sub - 1:sub, :], ni[sub - 1:sub, :]

        cr, ci = lax.fori_loop(0, ntile, tile, (st[0:1, :], st[1:2, :]))
        st[0:1, :] = cr
        st[1:2, :] = ci
        fr_ref[...] = cr
        fi_ref[...] = ci
        y = (jnp.dot(sr_ref[...].reshape(tc, ns).astype(BF16), wc_ref[0], preferred_element_type=F32)
             + jnp.dot(si_ref[...].reshape(tc, ns).astype(BF16), wc_ref[1], preferred_element_type=F32))
        z_ref[...] = _gelu_tanh(y + d_ref[...] * u)

    u_cols = ngt * kx
    fin = jax.ShapeDtypeStruct((n_seq, ngt, 1, ns), F32)
    return pl.pallas_call(
        body, grid=(n_seq, ngt, nch),
        in_specs=[pl.BlockSpec((tc, kx), lambda b, g, c: (b * nch + c, g)),
                  pl.BlockSpec((2, None, kx, ns), lambda b, g, c: (0, g, 0, 0)),
                  pl.BlockSpec((2, None, ns, kx), lambda b, g, c: (0, g, 0, 0)),
                  pl.BlockSpec((sub, ns), lambda b, g, c: (0, g)),
                  pl.BlockSpec((sub, ns), lambda b, g, c: (0, g)),
                  pl.BlockSpec((1, kx), lambda b, g, c: (0, g))],
        out_specs=[pl.BlockSpec((tc, kx), lambda b, g, c: (b * nch + c, g)),
                   pl.BlockSpec((None, None, 1, ns), lambda b, g, c: (b, g, 0, 0)),
                   pl.BlockSpec((None, None, 1, ns), lambda b, g, c: (b, g, 0, 0))],
        out_shape=(jax.ShapeDtypeStruct((out_rows, u_cols), F32), fin, fin),
        scratch_shapes=[pltpu.VMEM((sub, ns), F32), pltpu.VMEM((ntile, sub, ns), F32),
                        pltpu.VMEM((ntile, sub, ns), F32)],
        compiler_params=pltpu.CompilerParams(
            dimension_semantics=("parallel", "parallel", "arbitrary")),
        name="s5_prompt",
    )(uq, w_b, w_c, pw_re, pw_im, d2)


def _s5_tail(uq, w_b, w_c, pw_re, pw_im, d2, s0_re, s0_im, z_prev, *, row0):
    ngt, kx, ns = w_b.shape[1:]
    n_tail = s0_re.shape[0]
    assert row0 % n_tail == 0
    rb = row0 // n_tail

    def body(u_ref, wb_ref, wc_ref, pr_ref, pi_ref, d_ref, s0r_ref, s0i_ref, zp_ref, z_ref, fr_ref, fi_ref):
        u = u_ref[...]
        ub = u.astype(BF16)
        lr, li = pr_ref[0:1, :], pi_ref[0:1, :]
        s0r, s0i = s0r_ref[...], s0i_ref[...]
        sr = lr * s0r - li * s0i + jnp.dot(ub, wb_ref[0], preferred_element_type=F32)
        si = lr * s0i + li * s0r + jnp.dot(ub, wb_ref[1], preferred_element_type=F32)
        fr_ref[...] = sr
        fi_ref[...] = si
        y = (jnp.dot(sr.astype(BF16), wc_ref[0], preferred_element_type=F32)
             + jnp.dot(si.astype(BF16), wc_ref[1], preferred_element_type=F32))
        z_ref[...] = _gelu_tanh(y + d_ref[...] * u)

    st = jax.ShapeDtypeStruct(s0_re.shape, F32)
    return pl.pallas_call(
        body, grid=(ngt,),
        in_specs=[pl.BlockSpec((n_tail, kx), lambda g: (rb, g)),
                  pl.BlockSpec((2, None, kx, ns), lambda g: (0, g, 0, 0)),
                  pl.BlockSpec((2, None, ns, kx), lambda g: (0, g, 0, 0)),
                  pl.BlockSpec((SUBLANES_F32, ns), lambda g: (0, g)),
                  pl.BlockSpec((SUBLANES_F32, ns), lambda g: (0, g)),
                  pl.BlockSpec((1, kx), lambda g: (0, g)),
                  pl.BlockSpec((n_tail, ns), lambda g: (0, g)),
                  pl.BlockSpec((n_tail, ns), lambda g: (0, g)),
                  pl.BlockSpec(memory_space=pl.ANY)],
        out_specs=[pl.BlockSpec((n_tail, kx), lambda g: (rb, g)),
                   pl.BlockSpec((n_tail, ns), lambda g: (0, g)),
                   pl.BlockSpec((n_tail, ns), lambda g: (0, g))],
        out_shape=(jax.ShapeDtypeStruct(z_prev.shape, F32), st, st),
        input_output_aliases={8: 0},
        compiler_params=pltpu.CompilerParams(dimension_semantics=("parallel",)),
        name="s5_tail",
    )(uq, w_b, w_c, pw_re, pw_im, d2, s0_re, s0_im, z_prev)


def _block_diag_tiles(w):
    eye = jnp.eye(GROUP_TILE, dtype=w.dtype)
    nt, gt, a, b = w.shape
    return jnp.einsum("tgab,gh->tgahb", w, eye).reshape(nt, gt * a, gt * b)


def _s5_branch(uq, u_cols, s0_re, s0_im, n_prompt, n_seq, seq_len, params, layer, tm):
    (a_re, a_im, b_re, b_im, c_re, c_im, d, log_dt, w_glu, b_glu) = params
    Mp = uq.shape[0]
    G, P = a_re.shape[1], a_re.shape[2]
    C = SSM_GROUP
    GP = G * P
    ngt = G // GROUP_TILE

    lam_re, lam_im, bb_re, bb_im = _s5_discretize(
        a_re[layer].reshape(1, GP), a_im[layer].reshape(1, GP),
        jnp.broadcast_to(log_dt[layer][:, None], (G, P)).reshape(1, GP),
        b_re[layer].transpose(2, 0, 1).reshape(C, GP), b_im[layer].transpose(2, 0, 1).reshape(C, GP))

    def b_tiles(bb):
        t = bb.reshape(C, ngt, GROUP_TILE, P).transpose(1, 2, 0, 3)
        return _block_diag_tiles(t)

    w_b =jnp.stack([b_tiles(bb_re), b_tiles(bb_im)]).astype(BF16)

    def c_tiles(cc):
        t = cc.reshape(ngt, GROUP_TILE, C, P).transpose(0, 1, 3, 2)
        return _block_diag_tiles(t)

    w_c = jnp.stack([c_tiles(c_re[layer]), -c_tiles(c_im[layer])]).astype(BF16)

    d2 = d[layer].reshape(1, u_cols)
    z, fp_re, fp_im = _s5_prompt(uq, w_b, w_c, lam_re, lam_im, d2, n_seq=n_seq, seq_len=seq_len,
                                 out_rows=Mp)
    z, fs_re, fs_im = _s5_tail(uq, w_b, w_c, lam_re, lam_im, d2, s0_re, s0_im, z, row0=n_prompt)

    tn = _pick_tile(u_cols, 512, LANES)
    ssm_out = _dense(z, w_glu, (layer,), 0, u_cols, tm=tm, tn=tn,
                     extras=[(z, (tm, tn), lambda m, n, k, *pf: (m, n)),
                             (b_glu[layer].reshape(1, u_cols), (1, tn), lambda m, n, k, *pf: (0, n))],
                     epilogue=lambda p, e: e[0] * _sigmoid(p[0] + e[1]),
                     out_dtype=BF16, name="s5_glu")
    return ssm_out, (fp_re, fp_im), (fs_re, fs_im)


KV_COLS = 2 * N_KV_HEADS * HEAD_DIM
CHUNK_COLS = CMP_STRIDE * KV_COLS


KV_ROWS = 2 * N_KV_HEADS


def _compress_lead_trail(srcs, grid, pe, w1, rows_per_src, n_rows, prefetch=(), token_major=False,
                         name="cmp_lt"):
    ns = len(srcs)
    npf = len(prefetch)
    rows = ns * rows_per_src
    kd = CMP_STRIDE * HEAD_DIM

    def piece(xr, j, c):
        if token_major:
            return xr[pl.ds(j * KV_ROWS + c, rows_per_src, stride=CMP_STRIDE * KV_ROWS), :]
        return xr[:, j * KV_COLS + c * HEAD_DIM: j * KV_COLS + (c + 1) * HEAD_DIM]

    def body(*refs):
        x_refs = refs[npf:npf + ns]
        pe_ref, w_ref, o_ref = refs[npf + ns:]
        for kv in range(2):
            per_head = []
            for h in range(N_KV_HEADS):
                c = kv * N_KV_HEADS + h
                per_src = [jnp.concatenate([piece(xr, j, c) for j in range(CMP_STRIDE)], axis=1)
                           for xr in x_refs]
                per_head.append(per_src[0] if ns == 1 else jnp.concatenate(per_src, axis=0))
            xs = jnp.concatenate(per_head, axis=0)
            for lt in range(2):
                xv = (xs + pe_ref[kv, lt]).astype(BF16)
                r = jnp.dot(xv, w_ref[kv, lt].astype(BF16), preferred_element_type=F32)
                for h in range(N_KV_HEADS):
                    o_ref[kv, lt, h] = r[h * rows:(h + 1) * rows]

    in_specs = [pl.BlockSpec(bs, f) for _, bs, f in srcs]
    in_specs += [pl.BlockSpec((2, 2, 1, kd), lambda *a: (0, 0, 0, 0)),
                 pl.BlockSpec((2, 2, kd, HEAD_DIM), lambda *a: (0, 0, 0, 0))]
    step_of = lambda *a: a[0] * grid[1] + a[1]
    return pl.pallas_call(
        body,
        grid_spec=pltpu.PrefetchScalarGridSpec(
            num_scalar_prefetch=npf, grid=grid, in_specs=in_specs,
            out_specs=pl.BlockSpec((2, 2, N_KV_HEADS, rows, HEAD_DIM),
                                   lambda *a: (0, 0, 0, step_of(*a), 0))),
        out_shape=jax.ShapeDtypeStruct((2, 2, N_KV_HEADS, n_rows, HEAD_DIM), F32),
        compiler_params=pltpu.CompilerParams(
            dimension_semantics=("parallel", "arbitrary"), vmem_limit_bytes=48 << 20),
        name=name,
    )(*prefetch, *[a for a, _, _ in srcs], pe, w1)


def _compress_finish(lt, n_seq, n_chunk, b1, w2, name):
    lead = lt[:, 0]
    trail = lt[:, 1].reshape(2, N_KV_HEADS, n_seq, n_chunk, HEAD_DIM)
    trail = jnp.roll(trail, -1, axis=3).reshape(lead.shape)
    rows = N_KV_HEADS * n_seq * n_chunk
    lead = lead.reshape(2, rows, HEAD_DIM)
    trail = trail.reshape(2, rows, HEAD_DIM)
    rt = _pick_tile(rows, 1024, SUBLANES_F32)

    def body(a_ref, t_ref, b_ref, w_ref, o_ref):
        hid = _gelu_tanh(a_ref[...] + t_ref[...] + b_ref[...])
        o_ref[...] = jnp.dot(hid.astype(BF16), w_ref[...].astype(BF16), preferred_element_type=F32)

    out = pl.pallas_call(
        body, grid=(2, rows // rt),
        in_specs=[pl.BlockSpec((None, rt, HEAD_DIM), lambda kv, r: (kv, r, 0)),
                  pl.BlockSpec((None, rt, HEAD_DIM), lambda kv, r: (kv, r, 0)),
                  pl.BlockSpec((None, 1, HEAD_DIM), lambda kv, r: (kv, 0, 0)),
                  pl.BlockSpec((None, HEAD_DIM, HEAD_DIM), lambda kv, r: (kv, 0, 0))],
        out_specs=pl.BlockSpec((None, rt, HEAD_DIM), lambda kv, r: (kv, r, 0)),
        out_shape=jax.ShapeDtypeStruct((2, rows, HEAD_DIM), F32),
        compiler_params=pltpu.CompilerParams(dimension_semantics=("parallel", "parallel")),
        name=name,
    )(lead, trail, b1.reshape(2, 1, HEAD_DIM), w2)
    return out.reshape(2, N_KV_HEADS, n_seq, n_chunk, HEAD_DIM)


def _overlap_matrix(n_cmp, n_blk, rows, cols):
    start = np.arange(rows)[:, None] * CMP_STRIDE
    js = np.arange(cols)[None, :] * SEL_BLOCK
    ov = (start < js + SEL_BLOCK) & (start + CMP_BLOCK > js)
    ov &= (np.arange(rows)[:, None] < n_cmp) & (np.arange(cols)[None, :] < n_blk)
    return jnp.asarray(ov.astype(np.float32))


def _masked_softmax(s, valid):
    s = jnp.where(valid, s, NEG_INF)
    p = jnp.exp(s - jnp.max(s, axis=-1, keepdims=True)) * jnp.where(valid, 1.0, 0.0)
    return p * (1.0 / jnp.maximum(jnp.sum(p, axis=-1, keepdims=True), 1e-30))


def _softmax_dot(s, valid, v):
    hg, r, n = s.shape
    s = jnp.where(valid, s, NEG_INF)
    p = jnp.exp(s - jnp.max(s, axis=-1, keepdims=True))
    den = jnp.sum(p, axis=-1, keepdims=True)
    o = jnp.dot(p.reshape(hg * r, n).astype(BF16), v, preferred_element_type=F32)
    return o.reshape(hg, r, v.shape[1]) * (1.0 / den)


_NT = (((1,), (1,)), ((), ()))


def _nsa_prompt(uq, q_blk0, glg, cmp_kv, kvs, kvw, *, n_seq, seq_len, out_rows):
    T = seq_len
    QB = Q_BLOCK
    HG = Q_PER_KV
    HD = HEAD_DIM
    nqb = T // QB
    ncp = cmp_kv.shape[3]
    n_cmp = T // CMP_STRIDE - 1
    n_blk = -(-T // SEL_BLOCK)
    n_pick = min(N_SEL, n_blk)
    assert T % QB == 0 and T % SEL_BLOCK == 0 and n_blk <= LANES and ncp <= LANES
    assert n_pick > N_LOCAL_SEL
    CH = min(512, T)
    assert T % CH == 0
    WB = min(WINDOW + QB, T)
    scale = HD ** -0.5
    blk_of_key = np.arange(T) // SEL_BLOCK
    expand = (np.arange(LANES)[:, None] == blk_of_key[None, :]).astype(np.float32)
    expand = jnp.asarray(expand.reshape(LANES, T // CH, CH).transpose(1, 0, 2), dtype=BF16)
    ov = _overlap_matrix(n_cmp, n_blk, ncp, LANES)

    perm = np.zeros((N_KV_HEADS, LANES, LANES), np.float32)
    for hh in range(N_KV_HEADS):
        for g in range(HG):
            for j in range(3):
                perm[hh, (hh * HG + g) * 3 + j, j * HG + g] = 1.0
    perm = jnp.asarray(perm)

    def body(q_ref, gl_ref, perm_ref, kc_ref, vc_ref, ks_ref, vs_ref, kw_ref, vw_ref, e_ref, ov_ref, o_ref,
             m_ref, l_ref, acc_ref):
        i = pl.program_id(2)
        start = i * QB
        q = q_ref[...]
        q_st = jnp.concatenate([q[:, g * HD:(g + 1) * HD] for g in range(HG)], axis=0).astype(BF16)
        qpos = start + lax.broadcasted_iota(jnp.int32, (QB, 1), 0)

        s = lax.dot_general(q_st, kc_ref[...].astype(BF16), _NT, preferred_element_type=F32) * scale
        n_idx = lax.broadcasted_iota(jnp.int32, (QB, ncp), 1)
        cvalid = (n_idx * CMP_STRIDE + (CMP_BLOCK - 1) <= qpos) & (n_idx < n_cmp)
        p_c = _masked_softmax(s.reshape(HG, QB, ncp), cvalid)
        o_c = jnp.dot(p_c.reshape(HG * QB, ncp).astype(BF16), vc_ref[...].astype(BF16),
                      preferred_element_type=F32).reshape(HG, QB, HD)

        imp = jnp.dot(jnp.sum(p_c, axis=0), ov_ref[...], preferred_element_type=F32,
                      precision=lax.Precision.HIGHEST)
        blk = lax.broadcasted_iota(jnp.int32, (QB, LANES), 1)
        cur = qpos // SEL_BLOCK
        forced = (blk == 0) | ((blk <= cur) & (blk > cur - N_LOCAL_SEL))
        score = jnp.where(blk > cur, -BIG, jnp.where(forced, BIG, imp))
        rank = jnp.zeros((QB, LANES), F32)
        for j2 in range(n_blk):
            col = score[:, j2:j2 + 1]
            ahead = (col > score) | ((col == score) & (blk > j2))
            rank = rank + jnp.where(ahead, 1.0, 0.0)
        sel = jnp.where((rank < n_pick) & (blk < n_blk), 1.0, 0.0).astype(BF16)

        m_ref[...] = jnp.full(m_ref.shape, NEG_INF, F32)
        l_ref[...] = jnp.zeros(l_ref.shape, F32)
        acc_ref[...] = jnp.zeros(acc_ref.shape, F32)

        def chunk(c, carry):
            k0 = pl.multiple_of(c * CH, CH)
            kch = ks_ref[pl.ds(k0, CH), :].astype(BF16)
            vch = vs_ref[pl.ds(k0, CH), :].astype(BF16)
            sc = lax.dot_general(q_st, kch, _NT, preferred_element_type=F32) * scale
            picked = jnp.dot(sel, e_ref[c], preferred_element_type=F32)
            kpos = k0 + lax.broadcasted_iota(jnp.int32, (QB, CH), 1)
            valid = (picked > 0.5) & (kpos <= qpos)
            s3 = jnp.where(valid, sc.reshape(HG, QB, CH), NEG_INF)
            m_old = m_ref[...]
            m_new = jnp.maximum(m_old, jnp.max(s3, axis=-1, keepdims=True))
            p = jnp.exp(s3 - m_new)
            alpha = jnp.exp(m_old - m_new)
            l_ref[...] = alpha * l_ref[...] + jnp.sum(p, axis=-1, keepdims=True)
            pv = jnp.dot(p.reshape(HG * QB, CH).astype(BF16), vch, preferred_element_type=F32)
            acc_ref[...] = alpha * acc_ref[...] + pv.reshape(HG, QB, HD)
            m_ref[...] = m_new
            return carry

        lax.fori_loop(0, (start + QB + CH - 1) // CH, chunk, 0)
        o_s = acc_ref[...] * (1.0 / l_ref[...])

        w0 = pl.multiple_of(jnp.minimum(jnp.maximum(start - WINDOW, 0), T - WB), QB)
        sw = lax.dot_general(q_st, kw_ref[pl.ds(w0, WB), :].astype(BF16), _NT,
                             preferred_element_type=F32) * scale
        dist = qpos - (w0 + lax.broadcasted_iota(jnp.int32, (QB, WB), 1))
        o_w = _softmax_dot(sw.reshape(HG, QB, WB), (dist >= 0) & (dist < WINDOW),
                           vw_ref[pl.ds(w0, WB), :].astype(BF16))

        gate = _sigmoid(jnp.dot(gl_ref[...], perm_ref[...], preferred_element_type=F32,
                                precision=lax.Precision.HIGHEST))
        for g in range(HG):
            og = (gate[:, g:g + 1] * o_c[g] + gate[:, HG + g:HG + g + 1] * o_s[g]
                  + gate[:, 2 * HG + g:2 * HG + g + 1] * o_w[g])
            o_ref[:, g * HD:(g + 1) * HD] = og.astype(o_ref.dtype)

    row = lambda b, h, i: b * nqb + i
    kv_spec = lambda kv: pl.BlockSpec((T, HD), lambda b, h, i: (b, kv * N_KV_HEADS + h))
    return pl.pallas_call(
        body, grid=(n_seq, N_KV_HEADS, nqb),
        in_specs=[pl.BlockSpec((QB, HG * HD), lambda b, h, i: (row(b, h, i), q_blk0 + h)),
                  pl.BlockSpec((QB, LANES), lambda b, h, i: (row(b, h, i), 0)),
                  pl.BlockSpec((None, LANES, LANES), lambda b, h, i: (h, 0, 0)),
                  pl.BlockSpec((None, None, None, ncp, HD), lambda b, h, i: (0, h, b, 0, 0)),
                  pl.BlockSpec((None, None, None, ncp, HD), lambda b, h, i: (1, h, b, 0, 0)),
                  kv_spec(0), kv_spec(1), kv_spec(0), kv_spec(1),
                  pl.BlockSpec((T // CH, LANES, CH), lambda b, h, i: (0, 0, 0)),
                  pl.BlockSpec((ncp, LANES), lambda b, h, i: (0, 0))],
        out_specs=pl.BlockSpec((QB, HG * HD), lambda b, h, i: (row(b, h, i), h)),
        out_shape=jax.ShapeDtypeStruct((out_rows, N_HEADS * HD), BF16),
        scratch_shapes=[pltpu.VMEM((HG, QB, 1), F32), pltpu.VMEM((HG, QB, 1), F32),
                        pltpu.VMEM((HG, QB, HD), F32)],
        compiler_params=pltpu.CompilerParams(
            dimension_semantics=("parallel", "parallel", "arbitrary"), vmem_limit_bytes=48 << 20),
        name="nsa_prompt",
    )(uq, glg, perm, cmp_kv, cmp_kv, kvs, kvs, kvw, kvw, expand, ov)


def _nsa_sample_select(q_s, cmp_kv, past_len):
    S = q_s.shape[0]
    HG, HD = Q_PER_KV, HEAD_DIM
    nc = cmp_kv.shape[3]
    n_cmp = nc - 1
    q_pos = past_len
    n_blk = -(-(past_len + 1) // SEL_BLOCK)
    nbp = -(-n_blk // LANES) * LANES
    n_pick = min(N_SEL, n_blk)
    cur = q_pos // SEL_BLOCK
    scale = HD ** -0.5
    ov = _overlap_matrix(n_cmp, n_blk, nc, nbp)

    def body(q_ref, kc_ref, vc_ref, ov_ref, oc_ref, idx_ref):
        q = q_ref[...].astype(BF16)
        s = lax.dot_general(q, kc_ref[...].astype(BF16), _NT, preferred_element_type=F32) * scale
        n_idx = lax.broadcasted_iota(jnp.int32, (HG, nc), 1)
        cvalid = (n_idx * CMP_STRIDE + (CMP_BLOCK - 1) <= q_pos) & (n_idx < n_cmp)
        p_c = _masked_softmax(s, cvalid)
        oc_ref[...] = jnp.dot(p_c.astype(BF16), vc_ref[...].astype(BF16), preferred_element_type=F32)
        imp = jnp.sum(jnp.dot(p_c, ov_ref[...], preferred_element_type=F32,
                              precision=lax.Precision.HIGHEST), axis=0, keepdims=True)
        blk = lax.broadcasted_iota(jnp.int32, (1, nbp), 1)
        forced = (blk == 0) | ((blk <= cur) & (blk > cur - N_LOCAL_SEL))
        score = jnp.where(blk > cur, -BIG, jnp.where(forced, BIG, imp))
        col = jnp.broadcast_to(score, (LANES, nbp)).T[:, 0:1]
        jp = lax.broadcasted_iota(jnp.int32, (nbp, nbp), 0)
        jj = lax.broadcasted_iota(jnp.int32, (nbp, nbp), 1)
        ahead = ((col > score) | ((col == score) & (jp < jj))) & (jp < n_blk)
        rank = jnp.sum(jnp.where(ahead, 1.0, 0.0), axis=0, keepdims=True)
        lane = lax.broadcasted_iota(jnp.int32, (1, LANES), 1)
        out = jnp.zeros((1, LANES), jnp.int32)
        for r in range(n_pick):
            hit = (rank == float(r)) & (blk < n_blk)
            idx_r = jnp.sum(jnp.where(hit, blk, 0), axis=1, keepdims=True)
            out = jnp.where(lane == r, idx_r, out)
        idx_ref[...] = jnp.broadcast_to(out, (SUBLANES_F32, LANES))

    o_c, idx = pl.pallas_call(
        body, grid=(S, N_KV_HEADS),
        in_specs=[pl.BlockSpec((None, None, HG, HD), lambda b, h: (b, h, 0, 0)),
                  pl.BlockSpec((None, None, None, nc, HD), lambda b, h: (0, h, b, 0, 0)),
                  pl.BlockSpec((None, None, None, nc, HD), lambda b, h: (1, h, b, 0, 0)),
                  pl.BlockSpec((nc, nbp), lambda b, h: (0, 0))],
        out_specs=[pl.BlockSpec((None, None, HG, HD), lambda b, h: (b, h, 0, 0)),
                   pl.BlockSpec((None, None, SUBLANES_F32, LANES), lambda b, h: (b, h, 0, 0))],
        out_shape=(jax.ShapeDtypeStruct((S, N_KV_HEADS, HG, HD), F32),
                   jax.ShapeDtypeStruct((S, N_KV_HEADS, SUBLANES_F32, LANES), jnp.int32)),
        compiler_params=pltpu.CompilerParams(dimension_semantics=("parallel", "parallel")),
        name="nsa_sample_select",
    )(q_s, cmp_kv, cmp_kv, ov)
    return o_c, idx[:, :, 0, :n_pick], n_pick


def _nsa_sample_attend(q_s, o_c, gate_s, src, live, cache_sel_v, layer, new_sel, cache_win_v, new_win,
                       n_pick):
    S = q_s.shape[0]
    HG, HD = Q_PER_KV, HEAD_DIM
    wb = cache_win_v.shape[2]
    scale = HD ** -0.5
    nkeys = n_pick * SEL_BLOCK

    def body(src_ref, live_ref, q_ref, oc_ref, g_ref, *rest):
        blk_refs = rest[:n_pick]
        kn_ref, vn_ref, kw_ref, vw_ref, kwn_ref, vwn_ref, o_ref = rest[n_pick:]
        b, h = pl.program_id(0), pl.program_id(1)

        def rows_of(kv):
            per_head = [jnp.concatenate(
                [r[pl.ds(kv * N_KV_HEADS + hh, SEL_BLOCK, stride=KV_ROWS), :] for r in blk_refs], axis=0)
                for hh in range(N_KV_HEADS)]
            out = per_head[0]
            for hh in range(1, N_KV_HEADS):
                out = jnp.where(h == hh, per_head[hh], out)
            return out.astype(BF16)

        base = (b * N_KV_HEADS + h) * n_pick
        q = q_ref[...]
        qb = q.astype(BF16)

        def with_new_key(s, valid, s_new, v_old, v_new):
            sm = jnp.where(valid, s, NEG_INF)
            m = jnp.maximum(jnp.max(sm, axis=-1, keepdims=True), s_new)
            p = jnp.exp(sm - m) * jnp.where(valid, 1.0, 0.0)
            p_new = jnp.exp(s_new - m)
            den = jnp.sum(p, axis=-1, keepdims=True) + p_new
            return (jnp.dot(p.astype(BF16), v_old, preferred_element_type=F32) + p_new * v_new) / den

        keys = rows_of(0)
        vals = rows_of(1)
        s = lax.dot_general(qb, keys, _NT, preferred_element_type=F32) * scale
        lane = lax.broadcasted_iota(jnp.int32, (1, nkeys), 1)
        livef = jnp.zeros((1, nkeys), F32)
        for r in range(n_pick):
            livef = jnp.where((lane >= r * SEL_BLOCK) & (lane < (r + 1) * SEL_BLOCK),
                              live_ref[base + r].astype(F32), livef)
        s_new = jnp.sum(q * kn_ref[...], axis=-1, keepdims=True) * scale
        o_s = with_new_key(s, livef > 0.5, s_new, vals, vn_ref[...])

        sw = lax.dot_general(qb, kw_ref[...].astype(BF16), _NT, preferred_element_type=F32) * scale
        wrow = lax.broadcasted_iota(jnp.int32, (1, wb), 1)
        sw_new = jnp.sum(q * kwn_ref[...], axis=-1, keepdims=True) * scale
        o_w = with_new_key(sw, wrow > wb - WINDOW, sw_new, vw_ref[...].astype(BF16), vwn_ref[...])

        gate = _sigmoid(g_ref[...])
        o_ref[...] = gate[:, 0:1] * oc_ref[...] + gate[:, 1:2] * o_s + gate[:, 2:3] * o_w

    bpp = cache_sel_v.shape[2] // (SEL_BLOCK * KV_ROWS)

    def blk_spec(r):
        def index(b, h, src_ref, live_ref):
            s = src_ref[(b * N_KV_HEADS + h) * n_pick + r]
            return (layer, s // bpp, s % bpp, 0)
        return pl.BlockSpec((None, None, SEL_BLOCK * KV_ROWS, HD), index)

    qspec = pl.BlockSpec((None, None, HG, HD), lambda b, h, *_: (b, h, 0, 0))
    gspec = pl.BlockSpec((None, None, HG, LANES), lambda b, h, *_: (b, h, 0, 0))
    new_spec = lambda kv: pl.BlockSpec((None, None, 1, HD), lambda b, h, *_: (b, kv * N_KV_HEADS + h, 0, 0))
    win_spec = lambda kv: pl.BlockSpec((None, None, wb, HD), lambda b, h, *_: (layer, b, 0, kv * N_KV_HEADS + h))
    in_specs = ([qspec, qspec, gspec] + [blk_spec(r) for r in range(n_pick)]
                + [new_spec(0), new_spec(1), win_spec(0), win_spec(1), new_spec(0), new_spec(1)])
    return pl.pallas_call(
        body,
        grid_spec=pltpu.PrefetchScalarGridSpec(
            num_scalar_prefetch=2, grid=(S, N_KV_HEADS), in_specs=in_specs,
            out_specs=pl.BlockSpec((None, None, HG, HD), lambda b, h, *_: (b, h, 0, 0))),
        out_shape=jax.ShapeDtypeStruct((S, N_KV_HEADS, HG, HD), F32),
        compiler_params=pltpu.CompilerParams(dimension_semantics=("parallel", "parallel")),
        name="nsa_sample_attend",
    )(src, live, q_s, o_c, gate_s, *([cache_sel_v] * n_pick), new_sel, new_sel,
      cache_win_v, cache_win_v, new_win, new_win)


def _nsa_sample(q_s, gate_s, new_sel, new_win, cache_cmp, cache_sel, cache_win, page_table, layer,
                pe, w1, b1, w2):
    S, n_pages = page_table.shape
    depth, n_pool, page = cache_cmp.shape[:3]
    past_len = n_pages * page
    assert page % CMP_STRIDE == 0 and page % SEL_BLOCK == 0
    cpp = page // CMP_STRIDE
    bpp = page // SEL_BLOCK
    pps = _pick_tile(n_pages, 16, 1)
    cache_v = cache_cmp.reshape(depth, n_pool, page * KV_ROWS, HEAD_DIM)
    pt_flat = page_table.reshape(-1)
    srcs = [(cache_v, (None, None, page * KV_ROWS, HEAD_DIM),
             (lambda b, s, pt, p=p: (layer, pt[b * n_pages + s * pps + p], 0, 0))) for p in range(pps)]
    nc = n_pages * cpp
    lt = _compress_lead_trail(srcs, (S, n_pages // pps), pe, w1, cpp, S * nc, prefetch=(pt_flat,),
                              token_major=True, name="cmp_lt_sample")
    cmp_kv = _compress_finish(lt, S, nc, b1, w2, "cmp_finish_sample")
    o_c, idx, n_pick = _nsa_sample_select(q_s, cmp_kv, past_len)

    n_past_blk = past_len // SEL_BLOCK
    idc = jnp.minimum(idx, n_past_blk - 1)
    pages = jnp.take_along_axis(page_table[:, None, :], idc // bpp, axis=2)
    src = (pages * bpp + idc % bpp).astype(jnp.int32).reshape(-1)
    live = (idx < n_past_blk).astype(jnp.int32).reshape(-1)
    wb = cache_win.shape[2]
    out = _nsa_sample_attend(
        q_s, o_c, gate_s, src, live, cache_sel.reshape(depth, n_pool, page * KV_ROWS, HEAD_DIM), layer,
        new_sel, cache_win.reshape(depth, S, wb, KV_COLS), new_win, n_pick)
    return out.reshape(S, N_HEADS * HEAD_DIM)


MOE_ROW_TILE = 512


def _moe_route(x, g, w_router):
    M, D = x.shape
    E = w_router.shape[1]
    tr = _pick_tile(M, 256, SUBLANES_BF16)
    wr = jnp.pad(w_router, ((0, 0), (0, LANES - E)))

    def body(x_ref, g_ref, w_ref, h_ref, r_ref):
        xf = x_ref[...]
        ms = jnp.mean(xf * xf, axis=-1, keepdims=True)
        hf = xf * lax.rsqrt(ms + RMS_EPS) * g_ref[...]
        bits = lax.bitcast_convert_type(hf.astype(BF16).astype(F32), jnp.uint32)
        h_ref[...] = (bits[:, :D // 2] >> 16) | (bits[:, D // 2:] & jnp.uint32(0xFFFF0000))
        logits = jnp.dot(hf, w_ref[...], preferred_element_type=F32, precision=lax.Precision.HIGHEST)
        lane = lax.broadcasted_iota(jnp.int32, (tr, LANES), 1)
        logits = jnp.where(lane < E, logits, -jnp.inf)
        v1 = jnp.max(logits, axis=-1, keepdims=True)
        i1 = jnp.min(jnp.where(logits == v1, lane, LANES), axis=-1, keepdims=True)
        rest = jnp.where(lane == i1, -jnp.inf, logits)
        v2 = jnp.max(rest, axis=-1, keepdims=True)
        i2 = jnp.min(jnp.where(rest == v2, lane, LANES), axis=-1, keepdims=True)
        e2 = jnp.exp(v2 - v1)
        g1 = 1.0 / (1.0 + e2)
        out = jnp.where(lane == 0, i1.astype(F32), jnp.where(lane == 1, i2.astype(F32),
                        jnp.where(lane == 2, g1, jnp.where(lane == 3, e2 * g1, 0.0))))
        r_ref[...] = out

    return pl.pallas_call(
        body, grid=(M // tr,),
        in_specs=[pl.BlockSpec((tr, D), lambda i: (i, 0)), pl.BlockSpec((1, D), lambda i: (0, 0)),
                  pl.BlockSpec((D, LANES), lambda i: (0, 0))],
        out_specs=[pl.BlockSpec((tr, D // 2), lambda i: (i, 0)), pl.BlockSpec((tr, LANES), lambda i: (i, 0))],
        out_shape=(jax.ShapeDtypeStruct((M, D // 2), jnp.uint32), jax.ShapeDtypeStruct((M, LANES), F32)),
        compiler_params=pltpu.CompilerParams(dimension_semantics=("parallel",)),
        name="moe_route",
    )(x, g.reshape(1, D), wr)


MOE_GROUP_TILES = 5
MOE_DISPATCH_PARTS = 4


def _moe_down(hmid, wd, li, gate_b, tile_expert, n_used, tm):
    rows, F = hmid.shape
    D = wd.shape[-1]
    mt = MOE_GROUP_TILES
    n_tiles = rows // tm
    tk = _pick_tile(F, 2048, LANES)
    tn = _pick_tile(D, 1024, LANES)
    nk = F // tk
    n_slots = n_tiles * nk

    t = jnp.arange(n_tiles, dtype=jnp.int32)
    used = t < n_used
    same = (tile_expert[:, None] == tile_expert[None, :]) & used[None, :]
    first = jnp.min(jnp.where(same, t[None, :], n_tiles), axis=1)
    cnt = jnp.sum(same.astype(jnp.int32), axis=1)
    j = t - first
    m_local = jnp.where(used, j % mt, 0)
    g_first = t - m_local
    g_cnt = jnp.where(used, jnp.minimum(mt, cnt - (j - m_local)), 1)
    kk = jnp.arange(nk, dtype=jnp.int32)
    pos = (nk * g_first[:, None] + kk[None, :] * g_cnt[:, None] + m_local[:, None]).reshape(-1)
    n_items = (nk * n_used).astype(jnp.int32)
    order = jnp.minimum(jnp.arange(n_slots, dtype=jnp.int32), n_items - 1)
    table = lambda v: jnp.zeros((n_slots,), jnp.int32).at[pos].set(v.reshape(-1))[order]
    it_tile = table(jnp.broadcast_to(t[:, None], (n_tiles, nk)))
    it_k = table(jnp.broadcast_to(kk[None, :], (n_tiles, nk)))
    it_slot = table(jnp.broadcast_to(m_local[:, None], (n_tiles, nk)))
    it_exp = tile_expert[it_tile]
    it_out = jnp.where(it_k == nk - 1, it_tile, it_tile - it_slot)
    run_start = (jnp.arange(n_slots, dtype=jnp.int32) < n_items) & (it_slot == 0)
    it_run = (jnp.cumsum(run_start.astype(jnp.int32)) - 1).astype(jnp.int32)
    n_runs = jnp.sum(run_start.astype(jnp.int32)).reshape(1)
    it_nxt = _next_run_start(run_start)
    nn = D // tn

    def body(tile_ref, k_ref, slot_ref, exp_ref, out_ref, n_ref, run_ref, nxt_ref, nruns_ref,
             h_ref, w_hbm, g_ref, o_ref, acc_ref, wbuf, sem):
        n, i = pl.program_id(0), pl.program_id(1)

        def copy(item, col_tile, wslot):
            src = w_hbm.at[li, exp_ref[item], pl.ds(pl.multiple_of(k_ref[item] * tk, tk), tk),
                           pl.ds(pl.multiple_of(col_tile * tn, tn), tn)]
            return pltpu.make_async_copy(src, wbuf.at[wslot], sem.at[wslot])

        @pl.when(i < n_ref[0])
        def _():
            k = k_ref[i]
            slot = slot_ref[i]
            wslot = (n * nruns_ref[0] + run_ref[i]) % 2

            @pl.when(slot == 0)
            def _():
                @pl.when((n == 0) & (i == 0))
                def _():
                    copy(0, 0, 0).start()

                copy(i, n, wslot).wait()
                nx = nxt_ref[i]

                @pl.when(nx >= 0)
                def _():
                    copy(jnp.maximum(nx, 0), n, 1 - wslot).start()

                @pl.when((nx < 0) & (n + 1 < nn))
                def _():
                    copy(0, n + 1, 1 - wslot).start()

            p = jnp.dot(h_ref[...], wbuf[wslot].astype(BF16), preferred_element_type=F32)

            @pl.when(k == 0)
            def _():
                acc_ref[slot] = p

            @pl.when(k > 0)
            def _():
                acc_ref[slot] += p

            @pl.when(k == nk - 1)
            def _():
                o_ref[...] = acc_ref[slot] * g_ref[...][:, 0:1]

    est = (2 * _nbytes((tm, tk), BF16) + 2 * _nbytes((tk, tn), F32) + _nbytes((tk, tn), BF16)
           + (mt + 3) * _nbytes((tm, tn), F32))
    return pl.pallas_call(
        body,
        grid_spec=pltpu.PrefetchScalarGridSpec(
            num_scalar_prefetch=9, grid=(nn, n_slots),
            in_specs=[pl.BlockSpec((tm, tk), lambda n, i, tr, kr, *_: (tr[i], kr[i])),
                      pl.BlockSpec(memory_space=pl.ANY),
                      pl.BlockSpec((tm, LANES), lambda n, i, tr, *_: (tr[i], 0))],
            out_specs=pl.BlockSpec((tm, tn), lambda n, i, tr, kr, sr, er, orr, *_: (orr[i], n)),
            scratch_shapes=[pltpu.VMEM((mt, tm, tn), F32), pltpu.VMEM((2, tk, tn), F32),
                            pltpu.SemaphoreType.DMA((2,))]),
        out_shape=jax.ShapeDtypeStruct((rows, D), F32),
        compiler_params=pltpu.CompilerParams(
            dimension_semantics=("arbitrary", "arbitrary"), vmem_limit_bytes=_vmem_limit(est)),
        name="moe_down",
    )(it_tile, it_k, it_slot, it_exp, it_out, n_items.reshape(1), it_run, it_nxt, n_runs,
      hmid, wd, gate_b)


def _next_run_start(first):
    n = first.shape[0]
    idx = jnp.where(first, jnp.arange(n, dtype=jnp.int32), n)
    at_or_after = lax.cummin(idx[::-1])[::-1]
    nxt = jnp.concatenate([at_or_after[1:], jnp.full((1,), n, jnp.int32)])
    return jnp.where(nxt >= n, -1, nxt).astype(jnp.int32)


def _moe_up(xs, wg, wu, li, tile_expert, n_used, tm, tile0, rows_total, prev):
    rows = xs.shape[0]
    D, F = wg.shape[2], wg.shape[3]
    n_tiles = rows // tm
    tn = _pick_tile(F, 512, LANES)
    nn = F // tn

    t = jnp.arange(n_tiles, dtype=jnp.int32)
    prev_e = jnp.concatenate([jnp.full((1,), -1, jnp.int32), tile_expert[:-1]])
    first = (t < n_used) & (tile_expert != prev_e)
    rank = (jnp.cumsum(first.astype(jnp.int32)) - 1).astype(jnp.int32)
    n_runs = jnp.sum(first.astype(jnp.int32)).reshape(1)
    nxt = _next_run_start(first)

    def body(te, nu, first_ref, rank_ref, nxt_ref, nruns_ref, x_ref, wg_hbm, wu_hbm, *rest):
        o_ref, wbuf, sem = rest[-3:]
        n, m = pl.program_id(0), pl.program_id(1)

        def copies(e, col_tile, slot):
            cols = pl.ds(pl.multiple_of(col_tile * tn, tn), tn)
            return [pltpu.make_async_copy(w.at[li, e, :, cols], wbuf.at[slot, i], sem.at[slot, i])
                    for i, w in enumerate((wg_hbm, wu_hbm))]

        @pl.when(m < nu[0])
        def _():
            slot = (n * nruns_ref[0] + rank_ref[m]) % 2

            @pl.when(first_ref[m] == 1)
            def _():
                @pl.when((n == 0) & (m == 0))
                def _():
                    for c in copies(te[0], 0, 0):
                        c.start()

                for c in copies(te[m], n, slot):
                    c.wait()
                nx = nxt_ref[m]

                @pl.when(nx >= 0)
                def _():
                    for c in copies(te[jnp.maximum(nx, 0)], n, 1 - slot):
                        c.start()

                @pl.when((nx < 0) & (n + 1 < nn))
                def _():
                    for c in copies(te[0], n + 1, 1 - slot):
                        c.start()

            v = x_ref[...]
            lo = lax.bitcast_convert_type(v << 16, F32)
            hi = lax.bitcast_convert_type(v & jnp.uint32(0xFFFF0000), F32)
            xb = jnp.concatenate([lo, hi], axis=1).astype(BF16)
            gate = jnp.dot(xb, wbuf[slot, 0].astype(BF16), preferred_element_type=F32)
            up = jnp.dot(xb, wbuf[slot, 1].astype(BF16), preferred_element_type=F32)
            o_ref[...] = (gate * _sigmoid(gate) * up).astype(BF16)

    est = (2 * _nbytes((tm, D // 2), jnp.uint32) + 3 * _nbytes((tm, D), BF16) + 4 * _nbytes((D, tn), F32)
           + 2 * _nbytes((D, tn), BF16) + 2 * _nbytes((tm, tn), BF16) + 3 * _nbytes((tm, tn), F32))
    live_tile = lambda m, nu: jnp.minimum(m, jnp.maximum(nu[0] - 1, 0))
    in_specs = [pl.BlockSpec((tm, D // 2), lambda n, m, te, nu, *_: (live_tile(m, nu), 0)),
                pl.BlockSpec(memory_space=pl.ANY), pl.BlockSpec(memory_space=pl.ANY)]
    args = [xs, wg, wu]
    aliases = {}
    if prev is not None:
        in_specs.append(pl.BlockSpec(memory_space=pl.ANY))
        args.append(prev)
        aliases = {6 + 3: 0}
    return pl.pallas_call(
        body,
        grid_spec=pltpu.PrefetchScalarGridSpec(
            num_scalar_prefetch=6, grid=(nn, n_tiles), in_specs=in_specs,
            out_specs=pl.BlockSpec((tm, tn), lambda n, m, te, nu, *_: (tile0 + live_tile(m, nu), n)),
            scratch_shapes=[pltpu.VMEM((2, 2, D, tn), F32), pltpu.SemaphoreType.DMA((2, 2))]),
        out_shape=jax.ShapeDtypeStruct((rows_total, F), BF16),
        input_output_aliases=aliases,
        compiler_params=pltpu.CompilerParams(
            dimension_semantics=("arbitrary", "arbitrary"), vmem_limit_bytes=_vmem_limit(est)),
        name="moe_up",
    )(tile_expert, n_used.reshape(1), first.astype(jnp.int32), rank, nxt, n_runs, *args)


def _moe(x_all, n_real, g, w_router, wg, wu, wd, li):
    Mp, D = x_all.shape
    E, _, F = wg.shape[1:]
    tm = MOE_ROW_TILE
    h, route = _moe_route(x_all, g, w_router[li])

    r = route[:n_real]
    e_flat = r[:, :TOP_K].astype(jnp.int32).reshape(-1)
    gates = r[:, TOP_K:2 * TOP_K].reshape(-1)
    n_pairs = n_real * TOP_K
    onehot = (e_flat[:, None] == jnp.arange(E)[None, :]).astype(jnp.int32)
    counts = jnp.sum(onehot, axis=0)
    within = jnp.take_along_axis(jnp.cumsum(onehot, axis=0) - onehot, e_flat[:, None], axis=1)[:, 0]
    padded = (counts + tm - 1) // tm * tm
    ends = jnp.cumsum(padded)
    dest = (ends - padded)[e_flat] + within
    parts = MOE_DISPATCH_PARTS
    n_tiles = -(-(-(-n_pairs // tm) + E) // parts) * parts
    rows = n_tiles * tm
    src_tok = jnp.zeros((rows,), jnp.int32).at[dest].set(jnp.arange(n_pairs, dtype=jnp.int32) // TOP_K)
    row_gate = jnp.zeros((rows,), F32).at[dest].set(gates)
    n_used = (ends[-1] // tm).astype(jnp.int32)
    tile_ids = jnp.minimum(jnp.arange(n_tiles, dtype=jnp.int32), n_used - 1)
    tile_expert = jnp.minimum(jnp.searchsorted(ends, tile_ids * tm, side="right"), E - 1).astype(jnp.int32)
    tpp = n_tiles // parts
    hmid = None
    for p in range(parts):
        xs = jnp.take(h, src_tok[p * tpp * tm:(p + 1) * tpp * tm], axis=0, mode="clip")
        hmid = _moe_up(xs, wg, wu, li, tile_expert[p * tpp:(p + 1) * tpp],
                       jnp.clip(n_used - p * tpp, 0, tpp), tm, p * tpp, rows, hmid)

    gate_b = jnp.broadcast_to(row_gate[:, None], (rows, LANES))
    y = _moe_down(hmid, wd, li, gate_b, tile_expert, n_used, tm)

    d2 = dest.reshape(n_real, TOP_K)
    y_tok = jnp.take(y, d2[:, 0], axis=0, mode="clip")
    for k in range(1, TOP_K):
        y_tok = y_tok + jnp.take(y, d2[:, k], axis=0, mode="clip")
    return x_all.at[:n_real].add(y_tok)


def kernel(x_prompt, x_sample, cache_cmp, cache_sel, cache_win, state_ssm_re, state_ssm_im, page_table,
           norm_mix, w_in, ssm_a_re, ssm_a_im, ssm_b_re, ssm_b_im, ssm_c_re, ssm_c_im, ssm_d, ssm_log_dt,
           ssm_w_glu, ssm_b_glu, cmp_pe, cmp_w1, cmp_b1, cmp_w2, w_br_ssm, w_br_nsa, w_out,
           norm_ffn, ffn_w_gate, ffn_w_up, ffn_w_down, moe_router, moe_w_gate, moe_w_up, moe_w_down,
           norm_final):
    B, T, D = x_prompt.shape
    S = x_sample.shape[0]
    assert x_sample.shape[1] == 1
    depth = w_in.shape[0]
    n_prompt = B * T
    n_real = n_prompt + S
    Mp = -(-n_real // ROW_PAD) * ROW_PAD
    n_tail = Mp - n_prompt
    d_ssm = ssm_d.shape[1]
    d_nsa = N_HEADS * HEAD_DIM
    G, P = ssm_a_re.shape[1], ssm_a_re.shape[2]
    o_kv = d_ssm + d_nsa
    o_gl = o_kv + 3 * KV_COLS
    o_g = o_gl + 3 * N_HEADS
    assert d_ssm % (Q_PER_KV * HEAD_DIM) == 0 and w_in.shape[2] == o_g + 2 * D

    tm_big = _pick_tile(Mp, 1376, SUBLANES_BF16)
    tm_mid = _pick_tile(Mp, 688, SUBLANES_BF16)
    tn = 512

    x_all = jnp.concatenate([x_prompt.reshape(n_prompt, D), x_sample.reshape(S, D),
                             jnp.zeros((Mp - n_real, D), F32)], axis=0)
    w_in_t = jnp.swapaxes(w_in, 1, 2)
    pad_tail = lambda a: jnp.pad(a.reshape(S, -1), ((0, n_tail - S), (0, 0)))

    outs = {k: [] for k in ("cmp_p", "sel_p", "win_p", "sre_p", "sim_p",
                            "cmp_s", "sel_s", "win_s", "sre_s", "sim_s")}
    for l in range(depth):
        h = _rmsnorm(x_all, norm_mix[l], BF16, "norm_mix")
        uq = _dense(h, w_in_t, (l,), 0, o_kv, tm=tm_big, tn=tn, w_t=True, name="in_uq")
        kvc, kvs, kvw = [_dense(h, w_in_t, (l,), o_kv + i * KV_COLS, KV_COLS, tm=tm_big, tn=KV_COLS,
                                w_t=True, name=f"in_kv{i}") for i in range(3)]
        glg = _dense(h, w_in_t, (l,), o_gl, LANES, tm=tm_big, tn=LANES, w_t=True, name="in_gate_logits")
        gates = _dense_t_unaligned(h, w_in_t, (l,), o_g, 2 * D, tm=tm_big, tn=256, epilogue=_sigmoid,
                                   out_dtype=F32, name="in_merge_gates")

        s0 = [pad_tail(s[l]) for s in (state_ssm_re, state_ssm_im)]
        ssm_params = (ssm_a_re, ssm_a_im, ssm_b_re, ssm_b_im, ssm_c_re, ssm_c_im, ssm_d, ssm_log_dt,
                      ssm_w_glu, ssm_b_glu)
        ssm_out, fin_p, fin_s = _s5_branch(uq, d_ssm, s0[0], s0[1], n_prompt, B, T, ssm_params, l, tm_mid)

        kd = CMP_STRIDE * HEAD_DIM
        pe = cmp_pe[l].reshape(2, 2, 1, kd)
        w1 = cmp_w1[l].reshape(2, 2, kd, HEAD_DIM)
        n_chunk = T // CMP_STRIDE
        rb = _pick_tile(n_prompt // CMP_STRIDE, 128, SUBLANES_F32)
        lt = _compress_lead_trail(
            [(kvc.reshape(Mp // CMP_STRIDE, CHUNK_COLS), (rb, CHUNK_COLS), lambda a, s: (s, 0))],
            (1, n_prompt // CMP_STRIDE // rb), pe, w1, rb, n_prompt // CMP_STRIDE, name="cmp_lt_prompt")
        cmp_kv = _compress_finish(lt, B, n_chunk, cmp_b1[l], cmp_w2[l], "cmp_finish_prompt")
        if n_chunk < LANES:
            cmp_kv = jnp.pad(cmp_kv, ((0, 0), (0, 0), (0, 0), (0, LANES - n_chunk), (0, 0)))
        q_blk0 = d_ssm // (Q_PER_KV * HEAD_DIM)
        nsa_out = _nsa_prompt(uq, q_blk0, glg, cmp_kv, kvs, kvw, n_seq=B, seq_len=T, out_rows=Mp)

        tail = lambda a, c0, c1: a[n_prompt:n_real, c0:c1]
        q_s = tail(uq, d_ssm, o_kv).reshape(S, N_KV_HEADS, Q_PER_KV, HEAD_DIM)
        gl_s = tail(glg, 0, 3 * N_HEADS).reshape(S, N_KV_HEADS, Q_PER_KV, 3)
        gate_s = jnp.pad(gl_s, ((0, 0), (0, 0), (0, 0), (0, LANES - 3)))
        new_sel = tail(kvs, 0, KV_COLS).reshape(S, 2 * N_KV_HEADS, 1, HEAD_DIM)
        new_win = tail(kvw, 0, KV_COLS).reshape(S, 2 * N_KV_HEADS, 1, HEAD_DIM)
        nsa_s = _nsa_sample(q_s, gate_s, new_sel, new_win, cache_cmp, cache_sel, cache_win, page_table, l,
                            pe, w1, cmp_b1[l], cmp_w2[l])
        nsa_out = lax.dynamic_update_slice(nsa_out, pad_tail(nsa_s).astype(BF16), (n_prompt, 0))

        ng = D // tn
        y = _fused_matmul(
            xs=[(ssm_out, lambda m, n, k, *pf: (m, 0)), (nsa_out, lambda m, n, k, *pf: (m, 0))],
            dots=[(0, w_br_ssm, (None, d_ssm, tn), lambda m, n, k, *pf: (l, 0, n)),
                  (1, w_br_nsa, (None, d_nsa, tn), lambda m, n, k, *pf: (l, 0, n))],
            extras=[(gates, (tm_mid, tn), lambda m, n, k, *pf: (m, n)),
                    (gates, (tm_mid, tn), lambda m, n, k, *pf: (m, ng + n))],
            epilogue=lambda p, e: e[0] * p[0] + e[1] * p[1],
            M=Mp, N=D, K=d_ssm, tm=tm_mid, tn=tn, tk=d_ssm, out_dtype=BF16, name="merge")
        x_all = _dense(y, w_out, (l,), 0, D, tm=tm_big, tn=tn,
                       extras=[(x_all, (tm_big, tn), lambda m, n, k, *pf: (m, n))],
                       epilogue=lambda p, e: e[0] + p[0], name="out_proj")

        kv5 = lambda a, rows, lead: a[rows].reshape(lead + (2, N_KV_HEADS, HEAD_DIM))
        outs["cmp_p"].append(kv5(kvc, slice(0, n_prompt), (B, T)))
        outs["sel_p"].append(kv5(kvs, slice(0, n_prompt), (B, T)))
        wp = min(WINDOW, T)
        outs["win_p"].append(kv5(kvw, slice(0, n_prompt), (B, T))[:, T - wp:])
        outs["sre_p"].append(fin_p[0].reshape(B, G, P))
        outs["sim_p"].append(fin_p[1].reshape(B, G, P))
        outs["cmp_s"].append(kv5(kvc, slice(n_prompt, n_real), (S, 1)))
        outs["sel_s"].append(kv5(kvs, slice(n_prompt, n_real), (S, 1)))
        win = jnp.concatenate([cache_win[l], kv5(kvw, slice(n_prompt, n_real), (S, 1))], axis=1)
        outs["win_s"].append(win[:, 1:])
        outs["sre_s"].append(fin_s[0][:S].reshape(S, G, P))
        outs["sim_s"].append(fin_s[1][:S].reshape(S, G, P))

        if l % 2 == 0:
            i = l // 2
            h2 = _rmsnorm(x_all, norm_ffn[l], BF16, "norm_ffn")
            f = ffn_w_gate.shape[2]
            tnf = _pick_tile(f, 256, LANES)
            wmap = lambda m, n, k, *pf: (i, 0, n)
            hmid = _fused_matmul(
                xs=[(h2, lambda m, n, k, *pf: (m, 0))],
                dots=[(0, ffn_w_gate, (None, D, tnf), wmap), (0, ffn_w_up, (None, D, tnf), wmap)],
                epilogue=lambda p, e: p[0] * _sigmoid(p[0]) * p[1],
                M=Mp, N=f, K=D, tm=tm_big, tn=tnf, tk=D, out_dtype=BF16, name="ffn_up")
            tk = _pick_tile(f, 1024, LANES)
            tnd = _pick_tile(D, 1024, LANES)
            x_all = _fused_matmul(
                xs=[(hmid, lambda m, n, k, *pf: (m, k))],
                dots=[(0, ffn_w_down, (None, tk, tnd), lambda m, n, k, *pf: (i, k, n))],
                extras=[(x_all, (tm_big, tnd), lambda m, n, k, *pf: (m, n))],
                epilogue=lambda p, e: e[0] + p[0],
                M=Mp, N=D, K=f, tm=tm_big, tn=tnd, tk=tk, out_dtype=F32, name="ffn_down")
        else:
            x_all = _moe(x_all, n_real, norm_ffn[l], moe_router, moe_w_gate, moe_w_up, moe_w_down, l // 2)

    y_p = _rmsnorm(x_all, norm_final, F32, "norm_final", rows=n_prompt)
    y_s = _rmsnorm(x_all[n_prompt:], norm_final, F32, "norm_final_tail")
    st = lambda k: jnp.stack(outs[k])
    return (y_p.reshape(B, T, D), y_s[:S].reshape(S, 1, D),
            st("cmp_p"), st("sel_p"), st("win_p"), st("sre_p"), st("sim_p"),
            st("cmp_s"), st("sel_s"), st("win_s"), st("sre_s"), st("sim_s"))
```
